```python
import math
import jax
import jax.numpy as jnp
from jax import lax
import numpy as np

D_MODEL = 1024
BATCH = 8
SEQ = 4096
DEPTH = 4

CTX_LEN = 256
GRID_W = 64
MIX_WIDTH = D_MODEL
ATTN_WIDTH = MIX_WIDTH // 2
SGU_WIDTH = MIX_WIDTH - ATTN_WIDTH
DIFF_HEAD_DIM = 64
DIFF_HEADS = ATTN_WIDTH // (2 * DIFF_HEAD_DIM)
SGU_HEADS = 8
SGU_HEAD_DIM = SGU_WIDTH // SGU_HEADS
CHUNK = 128
Q_BLOCK = 128
N_EXPERTS = 16
CAPACITY_FACTOR = 2
EXPERT_FF = 2 * D_MODEL
ROPE_THETA = 10000.0
EPS = 1e-6
PROJ_WIDTH = 3 * ATTN_WIDTH + 2 * SGU_WIDTH
SPLITS = (ATTN_WIDTH, 2 * ATTN_WIDTH, 3 * ATTN_WIDTH, 3 * ATTN_WIDTH + SGU_WIDTH)

kernel_name = 'hybrid_diffattn_sgu_ecmoe_dit'


def rmsnorm(x, g):
    xf = x.astype(jnp.float32)
    y = xf * lax.rsqrt(jnp.mean(xf * xf, axis=-1, keepdims=True) + EPS)
    return (y * g.astype(jnp.float32)).astype(x.dtype)


def modulate(h, shift, scale):
    return h * (1 + scale) + shift


def axial_rope_tables(n_tokens):
    rows = n_tokens // GRID_W
    row = jnp.repeat(jnp.arange(rows), GRID_W).astype(jnp.float32)
    col = jnp.tile(jnp.arange(GRID_W), rows).astype(jnp.float32)
    n_freq = DIFF_HEAD_DIM // 4
    freqs = ROPE_THETA ** (-jnp.arange(n_freq, dtype=jnp.float32) / n_freq)
    ang_r = row[:, None] * freqs
    ang_c = col[:, None] * freqs
    ang = jnp.concatenate([ang_r, ang_r, ang_c, ang_c], axis=-1)
    return jnp.cos(ang), jnp.sin(ang)


def apply_axial_rope(t, cos, sin):
    tr = t.reshape(t.shape[:-1] + (2, 2, DIFF_HEAD_DIM // 4))
    rot = jnp.concatenate([-tr[..., 1:2, :], tr[..., 0:1, :]], axis=-2).reshape(t.shape)
    cos = cos.astype(t.dtype)[:, None, None, :]
    sin = sin.astype(t.dtype)[:, None, None, :]
    return t * cos + rot * sin


def diff_attn_core(q, k, v, lam):
    s = jnp.einsum('bqchd,bkchd->bchqk', q, k).astype(jnp.float32)
    p = jax.nn.softmax(s, axis=-1)
    a = p[:, 0] - lam * p[:, 1]
    return jnp.einsum('bhqk,bkhe->bqhe', a.astype(v.dtype), v)


def spatial_gating(u, gv, norm_g, w_s, b_s):
    b_, n, _ = u.shape
    u = jax.nn.gelu(u, approximate=False)
    gv = jax.nn.gelu(gv, approximate=False).reshape(b_, n, SGU_HEADS, SGU_HEAD_DIM)
    gv = rmsnorm(gv, norm_g.reshape(SGU_HEADS, SGU_HEAD_DIM))
    gv = gv.reshape(b_, n // CHUNK, CHUNK, SGU_HEADS, SGU_HEAD_DIM)
    s = jnp.einsum('hpq,bnqhe->bnphe', w_s, gv) + b_s.T[:, :, None]
    return u * s.reshape(b_, n, SGU_WIDTH)


def ec_moe(h, w_router, w_gate, w_up, w_down):
    b_, n, d = h.shape
    cap = CAPACITY_FACTOR * n // N_EXPERTS
    aff = jax.nn.softmax(jnp.einsum('bnd,de->bne', h, w_router).astype(jnp.float32), axis=-1)
    g, idx = lax.top_k(jnp.swapaxes(aff, 1, 2), cap)
    xs = jax.vmap(lambda hb, ib: hb[ib])(h, idx)
    hid = jax.nn.silu(jnp.einsum('becd,edf->becf', xs, w_gate)) * jnp.einsum('becd,edf->becf', xs, w_up)
    y = jnp.einsum('becf,efd->becd', hid, w_down) * g[..., None].astype(h.dtype)
    return jax.vmap(lambda yb, ib: jnp.zeros((n, d), yb.dtype).at[ib.reshape(-1)].add(yb.reshape(-1, d)))(y, idx)


def hybrid_mixer(h_lat, h_ctx, w_in, w_out, lq1, lk1, lq2, lk2, subln_g, sgu_norm_g, sgu_w, sgu_b, cos, sin, lam_init, ctx_out):
    b_, n, _ = h_lat.shape
    m = h_ctx.shape[1]
    hd = DIFF_HEAD_DIM
    scale = hd ** -0.5
    lam = (jnp.exp(jnp.sum(lq1 * lk1).astype(jnp.float32))
           - jnp.exp(jnp.sum(lq2 * lk2).astype(jnp.float32)) + lam_init)
    q, k, v, u, gv = jnp.split(h_lat @ w_in, SPLITS, axis=-1)
    q = apply_axial_rope(q.reshape(b_, n, 2, DIFF_HEADS, hd), cos, sin) * scale
    k = apply_axial_rope(k.reshape(b_, n, 2, DIFF_HEADS, hd), cos, sin)
    v = v.reshape(b_, n, DIFF_HEADS, 2 * hd)
    if ctx_out:
        qc, kc, vc, uc, gvc = jnp.split(h_ctx @ w_in, SPLITS, axis=-1)
    else:
        kc, vc = jnp.split(h_ctx @ w_in[:, ATTN_WIDTH:3 * ATTN_WIDTH], 2, axis=-1)
    kc = kc.reshape(b_, m, 2, DIFF_HEADS, hd)
    vc = vc.reshape(b_, m, DIFF_HEADS, 2 * hd)
    k_all = jnp.concatenate([kc, k], axis=1)
    v_all = jnp.concatenate([vc, v], axis=1)
    qb = q.reshape(b_, n // Q_BLOCK, Q_BLOCK, 2, DIFF_HEADS, hd).transpose(1, 0, 2, 3, 4, 5)
    o = lax.map(lambda qblk: diff_attn_core(qblk, k_all, v_all, lam), qb)
    o = o.transpose(1, 0, 2, 3, 4).reshape(b_, n, DIFF_HEADS, 2 * hd)
    o = rmsnorm(o, subln_g) * (1 - lam_init)
    y_lat = jnp.concatenate([o.reshape(b_, n, ATTN_WIDTH),
                             spatial_gating(u, gv, sgu_norm_g, sgu_w, sgu_b)], axis=-1) @ w_out
    if not ctx_out:
        return y_lat, None
    qc = qc.reshape(b_, m, 2, DIFF_HEADS, hd) * scale
    oc = rmsnorm(diff_attn_core(qc, kc, vc, lam), subln_g) * (1 - lam_init)
    y_ctx = jnp.concatenate([oc.reshape(b_, m, ATTN_WIDTH),
                             spatial_gating(uc, gvc, sgu_norm_g, sgu_w, sgu_b)], axis=-1) @ w_out
    return y_lat, y_ctx


def setup_inputs(seed: int = 0) -> dict:
    key = jax.random.key(seed)
    ks = jax.random.split(key, 24)
    f32 = jnp.float32
    d = D_MODEL

    def nrm(k, shape, s):
        return jax.random.normal(k, shape, f32) * s

    return {
        'x': nrm(ks[0], (BATCH, SEQ, d), 1.0),
        'c': nrm(ks[1], (BATCH, d), 1.0),
        'ctx': nrm(ks[2], (BATCH, CTX_LEN, d), 1.0),
        'c_ctx': nrm(ks[3], (d,), 1.0),
        'w_ada': nrm(ks[4], (DEPTH, d, 6 * d), 0.5 * d ** -0.5),
        'b_ada': nrm(ks[5], (DEPTH, 6 * d), 0.02),
        'norm1_g': 1.0 + nrm(ks[6], (DEPTH, d), 0.02),
        'norm2_g': 1.0 + nrm(ks[7], (DEPTH, d), 0.02),
        'w_in': nrm(ks[8], (DEPTH, d, PROJ_WIDTH), d ** -0.5),
        'w_out': nrm(ks[9], (DEPTH, MIX_WIDTH, d), MIX_WIDTH ** -0.5),
        'lambda_q1': nrm(ks[10], (DEPTH, DIFF_HEAD_DIM), 0.1),
        'lambda_k1': nrm(ks[11], (DEPTH, DIFF_HEAD_DIM), 0.1),
        'lambda_q2': nrm(ks[12], (DEPTH, DIFF_HEAD_DIM), 0.1),
        'lambda_k2': nrm(ks[13], (DEPTH, DIFF_HEAD_DIM), 0.1),
        'subln_g': 1.0 + nrm(ks[14], (DEPTH, 2 * DIFF_HEAD_DIM), 0.02),
        'sgu_norm_g': 1.0 + nrm(ks[15], (DEPTH, SGU_WIDTH), 0.02),
        'sgu_w': nrm(ks[16], (DEPTH, SGU_HEADS, CHUNK, CHUNK), CHUNK ** -0.5),
        'sgu_b': 1.0 + nrm(ks[17], (DEPTH, SGU_HEADS, CHUNK), 0.02),
        'w_router': nrm(ks[18], (DEPTH, d, N_EXPERTS), d ** -0.5),
        'w_gate': nrm(ks[19], (DEPTH, N_EXPERTS, d, EXPERT_FF), d ** -0.5),
        'w_up': nrm(ks[20], (DEPTH, N_EXPERTS, d, EXPERT_FF), d ** -0.5),
        'w_down': nrm(ks[21], (DEPTH, N_EXPERTS, EXPERT_FF, d), EXPERT_FF ** -0.5),
        'norm_f_g': 1.0 + nrm(ks[22], (d,), 0.02),
    }


def reference(x, c, ctx, c_ctx, w_ada, b_ada, norm1_g, norm2_g, w_in, w_out, lambda_q1, lambda_k1, lambda_q2, lambda_k2, subln_g, sgu_norm_g, sgu_w, sgu_b, w_router, w_gate, w_up, w_down, norm_f_g):
    n = x.shape[1]
    cos, sin = axial_rope_tables(n)
    silu_c = jax.nn.silu(c)
    silu_cc = jax.nn.silu(c_ctx)
    cx = ctx
    for l in range(DEPTH):
        last = l == DEPTH - 1
        lam_init = 0.8 - 0.6 * math.exp(-0.3 * l)
        mod = silu_c @ w_ada[l] + b_ada[l]
        sh1, sc1, g1, sh2, sc2, g2 = [t[:, None, :] for t in jnp.split(mod, 6, axis=-1)]
        cmod = silu_cc @ w_ada[l] + b_ada[l]
        csh1, csc1, cg1, csh2, csc2, cg2 = jnp.split(cmod, 6, axis=-1)
        h_lat = modulate(rmsnorm(x, norm1_g[l]), sh1, sc1)
        h_ctx = modulate(rmsnorm(cx, norm1_g[l]), csh1, csc1)
        y_lat, y_ctx = hybrid_mixer(h_lat, h_ctx, w_in[l], w_out[l], lambda_q1[l], lambda_k1[l],
                                    lambda_q2[l], lambda_k2[l], subln_g[l], sgu_norm_g[l], sgu_w[l],
                                    sgu_b[l], cos, sin, lam_init, not last)
        x = x + g1 * y_lat
        x = x + g2 * ec_moe(modulate(rmsnorm(x, norm2_g[l]), sh2, sc2),
                            w_router[l], w_gate[l], w_up[l], w_down[l])
        if not last:
            cx = cx + cg1 * y_ctx
            cx = cx + cg2 * ec_moe(modulate(rmsnorm(cx, norm2_g[l]), csh2, csc2),
                                   w_router[l], w_gate[l], w_up[l], w_down[l])
    return rmsnorm(x, norm_f_g)
```

```python
import functools
import math

import jax
import jax.numpy as jnp
from jax import lax
from jax.experimental import pallas as pl
from jax.experimental.pallas import tpu as pltpu

F32 = jnp.float32
MXU_DTYPE = jnp.bfloat16

EPS = 1e-6
GRID_W = 64
ROPE_THETA = 10000.0
HEAD_DIM = 64
SGU_HEADS = 8
CHUNK = 128
N_EXPERTS = 16
CAPACITY_FACTOR = 2
LANES = 128
VMEM_LIMIT = 56 * 1024 * 1024


def _cparams(*sem):
    return pltpu.CompilerParams(dimension_semantics=sem, vmem_limit_bytes=VMEM_LIMIT)


def _dot(a, b):
    return jnp.dot(a.astype(MXU_DTYPE), b.astype(MXU_DTYPE), preferred_element_type=F32)


def _dot_nt(a, b):
    return lax.dot_general(a.astype(MXU_DTYPE), b.astype(MXU_DTYPE), (((1,), (1,)), ((), ())),
                           preferred_element_type=F32)


def _dot_tn(a, b):
    return lax.dot_general(a.astype(MXU_DTYPE), b.astype(MXU_DTYPE), (((0,), (0,)), ((), ())),
                           preferred_element_type=F32)


def _split_hi_lo(x):
    hi = x.astype(MXU_DTYPE)
    lo = (x - hi.astype(F32)).astype(MXU_DTYPE)
    return hi, lo


def _silu(x):
    return x * jax.nn.sigmoid(x)


def _gelu(x):
    return 0.5 * x * (1.0 + lax.erf(x * (2.0 ** -0.5)))


def _tile(n, pref):
    t = pref
    while t > 8 and n % t:
        t //= 2
    return t if n % t == 0 else n


def _ada_kernel(c_ref, w_ref, b_ref, o_ref):
    o_ref[0] = _dot(_silu(c_ref[...]), w_ref[0]) + b_ref[0]


def _ada(cc, w_ada, b_ada):
    depth, d, n6 = w_ada.shape
    rows = cc.shape[0]
    tn = _tile(n6, 1536)
    return pl.pallas_call(
        _ada_kernel,
        grid=(depth, n6 // tn),
        in_specs=[pl.BlockSpec((rows, d), lambda l, j: (0, 0)),
                  pl.BlockSpec((1, d, tn), lambda l, j: (l, 0, j)),
                  pl.BlockSpec((1, 1, tn), lambda l, j: (l, 0, j))],
        out_specs=pl.BlockSpec((1, rows, tn), lambda l, j: (l, 0, j)),
        out_shape=jax.ShapeDtypeStruct((depth, rows, n6), F32),
        compiler_params=_cparams("parallel", "parallel"), name="ada",
    )(cc, w_ada, b_ada.reshape(depth, 1, n6))


def _proj_kernel(*refs, has_res, rope, aw, sw):
    if has_res:
        x_ref, moe_ref, g2_ref, *refs = refs
    else:
        x_ref, *refs = refs
    (ng_ref, sh_ref, sc_ref, w_ref, cos_ref, sin_ref, sng_ref, ws_ref, bs_ref, ones_ref, *outs) = refs
    if has_res:
        xo_ref, q_ref, k_ref, v_ref, sg_ref = outs
    else:
        q_ref, k_ref, v_ref, sg_ref = outs

    x = x_ref[0]
    if has_res:
        x = x + g2_ref[0] * moe_ref[0]
        xo_ref[0] = x
    h = x * lax.rsqrt(jnp.mean(x * x, axis=-1, keepdims=True) + EPS) * ng_ref[...]
    h = h * (1.0 + sc_ref[0]) + sh_ref[0]
    p = _dot(h, w_ref[...])

    q = p[:, :aw]
    k = p[:, aw:2 * aw]
    if rope:
        first = (lax.broadcasted_iota(jnp.int32, q.shape, 1) % 32) < 16
        cos = cos_ref[...]
        sin = sin_ref[...]

        def rot(t):
            return jnp.where(first, pltpu.roll(t, aw - 16, 1), pltpu.roll(t, 16, 1))

        q = q * cos + rot(q) * sin
        k = k * cos + rot(k) * sin
    q_ref[0] = (q * (HEAD_DIM ** -0.5)).astype(q_ref.dtype)
    k_ref[0] = k.astype(k_ref.dtype)
    v_ref[0] = p[:, 2 * aw:3 * aw].astype(v_ref.dtype)

    u = _gelu(p[:, 3 * aw:3 * aw + sw])
    gv = _gelu(p[:, 3 * aw + sw:])
    sq_hi, sq_lo = _split_hi_lo(gv * gv)
    ms = _dot(sq_hi, ones_ref[...]) + _dot(sq_lo, ones_ref[...])
    gvn = (gv * lax.rsqrt(ms + EPS) * sng_ref[...]).astype(MXU_DTYPE)
    head_of_lane = lax.broadcasted_iota(jnp.int32, (CHUNK, sw), 1) // (sw // SGU_HEADS)
    tm = x.shape[0]
    for ci in range(tm // CHUNK):
        rows = slice(ci * CHUNK, (ci + 1) * CHUNK)
        r = _dot(ws_ref[...], gvn[rows])
        s = bs_ref[...]
        for hh in range(SGU_HEADS):
            s = s + jnp.where(head_of_lane == hh, r[hh * CHUNK:(hh + 1) * CHUNK], 0.0)
        sg_ref[0, rows, :] = (u[rows] * s).astype(sg_ref.dtype)


def _proj(x, res, ng, sh, sc, w_in_p, cos, sin, sng, ws, bs, ones_blk, *, rope, aw, sw):
    b, n, d = x.shape
    pw = w_in_p.shape[1]
    tm = _tile(n, 512)
    has_res = res is not None
    tok = lambda bi, i: (bi, i, 0)
    per_b = lambda bi, i: (bi, 0, 0)
    const2 = lambda bi, i: (0, 0)
    in_specs = [pl.BlockSpec((1, tm, d), tok)]
    args = [x]
    if has_res:
        moe, g2 = res
        in_specs += [pl.BlockSpec((1, tm, d), tok), pl.BlockSpec((1, 1, d), per_b)]
        args += [moe, g2]
    in_specs += [pl.BlockSpec((1, d), const2), pl.BlockSpec((1, 1, d), per_b), pl.BlockSpec((1, 1, d), per_b),
                 pl.BlockSpec((d, pw), const2),
                 pl.BlockSpec((tm, aw), lambda bi, i: (i, 0)), pl.BlockSpec((tm, aw), lambda bi, i: (i, 0)),
                 pl.BlockSpec((1, sw), const2), pl.BlockSpec((SGU_HEADS * CHUNK, CHUNK), const2),
                 pl.BlockSpec((CHUNK, sw), const2), pl.BlockSpec((sw, sw), const2)]
    args += [ng, sh, sc, w_in_p, cos, sin, sng, ws, bs, ones_blk]
    out_specs = [pl.BlockSpec((1, tm, aw), tok)] * 3 + [pl.BlockSpec((1, tm, sw), tok)]
    out_shape = [jax.ShapeDtypeStruct((b, n, aw), MXU_DTYPE)] * 3 + [jax.ShapeDtypeStruct((b, n, sw), MXU_DTYPE)]
    if has_res:
        out_specs = [pl.BlockSpec((1, tm, d), tok)] + out_specs
        out_shape = [jax.ShapeDtypeStruct((b, n, d), F32)] + out_shape
    outs = pl.pallas_call(
        functools.partial(_proj_kernel, has_res=has_res, rope=rope, aw=aw, sw=sw),
        grid=(b, n // tm),
        in_specs=in_specs, out_specs=out_specs, out_shape=out_shape,
        compiler_params=_cparams("parallel", "parallel"), name="proj_lat" if rope else "proj_ctx",
    )(*args)
    if has_res:
        return outs[0], outs[1:]
    return x, outs


def _attn_kernel(lam_ref, q_ref, k_ref, v_ref, g_ref, o_ref, *, tk, lam_init):
    q = q_ref[0]
    tq, hw = q.shape
    lane = lax.broadcasted_iota(jnp.int32, q.shape, 1)
    zero = jnp.zeros_like(q)
    qs = (jnp.where(lane < HEAD_DIM, q, zero), jnp.where(lane >= HEAD_DIM, q, zero))
    nk = k_ref.shape[1] // tk

    def body(j, carry):
        start = pl.multiple_of(j * tk, tk)
        kt = k_ref[0, pl.ds(start, tk), :]
        vt = v_ref[0, pl.ds(start, tk), :]
        new = []
        for c in range(2):
            m, l, acc = carry[c]
            s = _dot_nt(qs[c], kt)
            m_new = jnp.maximum(m, jnp.max(s, axis=-1, keepdims=True))
            p = jnp.exp(s - m_new)
            alpha = jnp.exp(m - m_new)
            l = alpha * l + jnp.sum(p, axis=-1, keepdims=True)
            acc = alpha * acc + _dot(p, vt)
            new.append((m_new, l, acc))
        return tuple(new)

    init = tuple((jnp.full((tq, 1), -jnp.inf, F32), jnp.zeros((tq, 1), F32), jnp.zeros((tq, hw), F32))
                 for _ in range(2))
    (_, l0, a0), (_, l1, a1) = lax.fori_loop(0, nk, body, init)

    lp = lam_ref[...]
    lam = (jnp.exp(jnp.sum(lp[0:1] * lp[1:2], axis=-1, keepdims=True))
           - jnp.exp(jnp.sum(lp[2:3] * lp[3:4], axis=-1, keepdims=True)) + lam_init)
    o = a0 / l0 - lam * (a1 / l1)
    o = o * lax.rsqrt(jnp.mean(o * o, axis=-1, keepdims=True) + EPS) * g_ref[...] * (1.0 - lam_init)
    o_ref[0] = o.astype(o_ref.dtype)


def _attn(lam_params, q, k, v, subln_g, *, lam_init):
    b, n, aw = q.shape
    nkeys = k.shape[1]
    hw = 2 * HEAD_DIM
    heads = aw // hw
    tq = _tile(n, 512)
    tk = _tile(nkeys, 256)
    return pl.pallas_call(
        functools.partial(_attn_kernel, tk=tk, lam_init=lam_init),
        grid=(b, heads, n // tq),
        in_specs=[pl.BlockSpec(lam_params.shape, lambda bi, h, i: (0, 0)),
                  pl.BlockSpec((1, tq, hw), lambda bi, h, i: (bi, i, h)),
                  pl.BlockSpec((1, nkeys, hw), lambda bi, h, i: (bi, 0, h)),
                  pl.BlockSpec((1, nkeys, hw), lambda bi, h, i: (bi, 0, h)),
                  pl.BlockSpec((1, hw), lambda bi, h, i: (0, 0))],
        out_specs=pl.BlockSpec((1, tq, hw), lambda bi, h, i: (bi, i, h)),
        out_shape=jax.ShapeDtypeStruct((b, n, aw), MXU_DTYPE),
        compiler_params=_cparams("parallel", "parallel", "parallel"), name=f"attn_{n}",
    )(lam_params, q, k, v, subln_g)


def _out_kernel(o_ref, sg_ref, w_ref, x_ref, g1_ref, ng_ref, sh_ref, sc_ref, wr_ref, xo_ref, hm_ref, aff_ref, *, aw):
    y = _dot(o_ref[0], w_ref[:aw, :]) + _dot(sg_ref[0], w_ref[aw:, :])
    x = x_ref[0] + g1_ref[0] * y
    xo_ref[0] = x
    h = x * lax.rsqrt(jnp.mean(x * x, axis=-1, keepdims=True) + EPS) * ng_ref[...]
    h = h * (1.0 + sc_ref[0]) + sh_ref[0]
    hm_ref[0] = h.astype(hm_ref.dtype)
    logits = _dot_nt(wr_ref[...], h)
    e = jnp.exp(logits - jnp.max(logits, axis=0, keepdims=True))
    aff_ref[0] = e / jnp.sum(e, axis=0, keepdims=True)


def _outproj(o, sg, w_out, x, g1, ng, sh, sc, wr_t):
    b, n, d = x.shape
    aw = o.shape[2]
    sw = sg.shape[2]
    ne = wr_t.shape[0]
    tm = _tile(n, 512)
    tok = lambda bi, i: (bi, i, 0)
    per_b = lambda bi, i: (bi, 0, 0)
    const2 = lambda bi, i: (0, 0)
    return pl.pallas_call(
        functools.partial(_out_kernel, aw=aw),
        grid=(b, n // tm),
        in_specs=[pl.BlockSpec((1, tm, aw), tok), pl.BlockSpec((1, tm, sw), tok),
                  pl.BlockSpec((aw + sw, d), const2), pl.BlockSpec((1, tm, d), tok),
                  pl.BlockSpec((1, 1, d), per_b), pl.BlockSpec((1, d), const2),
                  pl.BlockSpec((1, 1, d), per_b), pl.BlockSpec((1, 1, d), per_b),
                  pl.BlockSpec((ne, d), const2)],
        out_specs=[pl.BlockSpec((1, tm, d), tok), pl.BlockSpec((1, tm, d), tok),
                   pl.BlockSpec((1, ne, tm), lambda bi, i: (bi, 0, i))],
        out_shape=[jax.ShapeDtypeStruct((b, n, d), F32), jax.ShapeDtypeStruct((b, n, d), MXU_DTYPE),
                   jax.ShapeDtypeStruct((b, ne, n), F32)],
        compiler_params=_cparams("parallel", "parallel"), name=f"outproj_{n}",
    )(o, sg, w_out, x, g1, ng, sh, sc, wr_t)


def _route_kernel(aff_ref, tri_ref, pos_ref, *, cap):
    a = aff_ref[0]
    ne, n = a.shape
    bits = lax.bitcast_convert_type(a, jnp.int32)
    t = jnp.zeros((ne, 1), jnp.int32)
    for bit in range(30, -1, -1):
        cand = t | (1 << bit)
        cnt = jnp.sum((bits >= cand).astype(jnp.int32), axis=1, keepdims=True)
        t = jnp.where(cnt >= cap, cand, t)
    gt = bits > t
    eq = bits == t
    need = cap - jnp.sum(gt.astype(jnp.int32), axis=1, keepdims=True)

    def exclusive_count(mask):
        out = []
        run = jnp.zeros((ne, 1), F32)
        for ci in range(n // LANES):
            m = mask[:, ci * LANES:(ci + 1) * LANES].astype(F32)
            incl = _dot(m, tri_ref[...])
            out.append(run + incl - m)
            run = run + incl[:, LANES - 1:LANES]
        return jnp.concatenate(out, axis=1)

    sel = gt | (eq & (exclusive_count(eq) < need.astype(F32)))
    pos_ref[0] = jnp.where(sel, exclusive_count(sel), -1.0)


def _route(aff_t, tri, *, cap):
    b, ne, n = aff_t.shape
    return pl.pallas_call(
        functools.partial(_route_kernel, cap=cap),
        grid=(b,),
        in_specs=[pl.BlockSpec((1, ne, n), lambda bi: (bi, 0, 0)), pl.BlockSpec((LANES, LANES), lambda bi: (0, 0))],
        out_specs=pl.BlockSpec((1, ne, n), lambda bi: (bi, 0, 0)),
        out_shape=jax.ShapeDtypeStruct((b, ne, n), F32),
        compiler_params=_cparams("parallel"), name=f"route_{n}",
    )(aff_t, tri)


def _gather_kernel(pos_ref, aff_ref, hm_ref, xs_ref, gs_ref, *, cap, tn):
    n = hm_ref.shape[1]
    d = hm_ref.shape[2]
    slot = lax.broadcasted_iota(jnp.int32, (cap, tn), 0).astype(F32)
    xs = jnp.zeros((cap, d), F32)
    gs = jnp.zeros((cap, 1), F32)
    for ci in range(n // tn):
        cols = slice(ci * tn, (ci + 1) * tn)
        hit = pos_ref[0, 0, :, cols] == slot
        xs = xs + _dot(hit.astype(MXU_DTYPE), hm_ref[0, cols, :])
        gs = gs + jnp.sum(jnp.where(hit, aff_ref[0, 0, :, cols], 0.0), axis=1, keepdims=True)
    xs_ref[0] = xs.astype(xs_ref.dtype)
    gs_ref[0] = jnp.broadcast_to(gs, (cap, LANES))


def _gather(pos, aff_t, hm, *, cap):
    b, ne, n = pos.shape
    d = hm.shape[2]
    tn = _tile(n, 512)
    pos4 = pos.reshape(b, ne, 1, n)
    aff4 = aff_t.reshape(b, ne, 1, n)
    return pl.pallas_call(
        functools.partial(_gather_kernel, cap=cap, tn=tn),
        grid=(b, ne),
        in_specs=[pl.BlockSpec((1, 1, 1, n), lambda bi, e: (bi, e, 0, 0)),
                  pl.BlockSpec((1, 1, 1, n), lambda bi, e: (bi, e, 0, 0)),
                  pl.BlockSpec((1, n, d), lambda bi, e: (bi, 0, 0))],
        out_specs=[pl.BlockSpec((1, cap, d), lambda bi, e: (e, bi, 0)),
                   pl.BlockSpec((1, cap, LANES), lambda bi, e: (e, bi, 0))],
        out_shape=[jax.ShapeDtypeStruct((ne, b * cap, d), MXU_DTYPE),
                   jax.ShapeDtypeStruct((ne, b * cap, LANES), F32)],
        compiler_params=_cparams("parallel", "parallel"), name=f"gather_{n}",
    )(pos4, aff4, hm)


def _ffn_kernel(x_ref, wg_ref, wu_ref, wd_ref, gs_ref, y_ref, acc_ref):
    f = pl.program_id(2)

    @pl.when(f == 0)
    def _():
        acc_ref[...] = jnp.zeros_like(acc_ref)

    x = x_ref[0]
    hid = _silu(_dot(x, wg_ref[0])) * _dot(x, wu_ref[0])
    acc_ref[...] += _dot(hid, wd_ref[0])

    @pl.when(f == pl.num_programs(2) - 1)
    def _():
        y_ref[0] = acc_ref[...] * gs_ref[0][:, :1]


def _ffn(xs, w_gate, w_up, w_down, gs):
    ne, r, d = xs.shape
    ff = w_gate.shape[2]
    tm = _tile(r, 1024)
    tf = _tile(ff, 512)
    return pl.pallas_call(
        _ffn_kernel,
        grid=(ne, r // tm, ff // tf),
        in_specs=[pl.BlockSpec((1, tm, d), lambda e, i, f: (e, i, 0)),
                  pl.BlockSpec((1, d, tf), lambda e, i, f: (e, 0, f)),
                  pl.BlockSpec((1, d, tf), lambda e, i, f: (e, 0, f)),
                  pl.BlockSpec((1, tf, d), lambda e, i, f: (e, f, 0)),
                  pl.BlockSpec((1, tm, LANES), lambda e, i, f: (e, i, 0))],
        out_specs=pl.BlockSpec((1, tm, d), lambda e, i, f: (e, i, 0)),
        out_shape=jax.ShapeDtypeStruct((ne, r, d), F32),
        scratch_shapes=[pltpu.VMEM((tm, d), F32)],
        compiler_params=_cparams("parallel", "parallel", "arbitrary"), name=f"ffn_{r}",
    )(xs, w_gate, w_up, w_down, gs)


def _combine_kernel(pos_ref, y_ref, o_ref, *, cap, tn):
    e = pl.program_id(1)

    @pl.when(e == 0)
    def _():
        o_ref[...] = jnp.zeros_like(o_ref)

    n = o_ref.shape[1]
    y_hi, y_lo = _split_hi_lo(y_ref[0])
    slot = lax.broadcasted_iota(jnp.int32, (cap, tn), 0).astype(F32)
    for ci in range(n // tn):
        cols = slice(ci * tn, (ci + 1) * tn)
        hit = (pos_ref[0, 0, :, cols] == slot).astype(MXU_DTYPE)
        o_ref[0, cols, :] += _dot_tn(hit, y_hi) + _dot_tn(hit, y_lo)


def _combine(pos, yg, *, cap):
    b, ne, n = pos.shape
    d = yg.shape[2]
    tn = _tile(n, 512)
    return pl.pallas_call(
        functools.partial(_combine_kernel, cap=cap, tn=tn),
        grid=(b, ne),
        in_specs=[pl.BlockSpec((1, 1, 1, n), lambda bi, e: (bi, e, 0, 0)),
                  pl.BlockSpec((1, cap, d), lambda bi, e: (e, bi, 0))],
        out_specs=pl.BlockSpec((1, n, d), lambda bi, e: (bi, 0, 0)),
        out_shape=jax.ShapeDtypeStruct((b, n, d), F32),
        compiler_params=_cparams("parallel", "arbitrary"), name=f"combine_{n}",
    )(pos.reshape(b, ne, 1, n), yg)


def _final_kernel(x_ref, moe_ref, g2_ref, ng_ref, o_ref):
    x = x_ref[0] + g2_ref[0] * moe_ref[0]
    o_ref[0] = x * lax.rsqrt(jnp.mean(x * x, axis=-1, keepdims=True) + EPS) * ng_ref[...]


def _final(x, moe, g2, ng):
    b, n, d = x.shape
    tm = _tile(n, 512)
    tok = lambda bi, i: (bi, i, 0)
    return pl.pallas_call(
        _final_kernel,
        grid=(b, n // tm),
        in_specs=[pl.BlockSpec((1, tm, d), tok), pl.BlockSpec((1, tm, d), tok),
                  pl.BlockSpec((1, 1, d), lambda bi, i: (bi, 0, 0)), pl.BlockSpec((1, d), lambda bi, i: (0, 0))],
        out_specs=pl.BlockSpec((1, tm, d), tok),
        out_shape=jax.ShapeDtypeStruct((b, n, d), F32),
        compiler_params=_cparams("parallel", "parallel"), name="final",
    )(x, moe, g2, ng)


def _rope_tables(n, heads):
    rows = n // GRID_W
    row = jnp.repeat(jnp.arange(rows), GRID_W).astype(F32)
    col = jnp.tile(jnp.arange(GRID_W), rows).astype(F32)
    n_freq = HEAD_DIM // 4
    freqs = ROPE_THETA ** (-jnp.arange(n_freq, dtype=F32) / n_freq)
    ang_r = row[:, None] * freqs
    ang_c = col[:, None] * freqs
    ang = jnp.concatenate([ang_r, ang_r, ang_c, ang_c], axis=-1)
    sign = jnp.where((jnp.arange(HEAD_DIM) % 32) < 16, -1.0, 1.0).astype(F32)
    return jnp.tile(jnp.cos(ang), (1, 2 * heads)), jnp.tile(jnp.sin(ang) * sign, (1, 2 * heads))


def _qk_column_order(aw):
    heads = aw // (2 * HEAD_DIM)
    new = jnp.arange(aw)
    h, c, dd = new // (2 * HEAD_DIM), (new // HEAD_DIM) % 2, new % HEAD_DIM
    return c * (heads * HEAD_DIM) + h * HEAD_DIM + dd


def _moe(hm, aff_t, tri, w_gate, w_up, w_down):
    n = hm.shape[1]
    cap = CAPACITY_FACTOR * n // N_EXPERTS
    pos = _route(aff_t, tri, cap=cap)
    xs, gs = _gather(pos, aff_t, hm, cap=cap)
    yg = _ffn(xs, w_gate, w_up, w_down, gs)
    return _combine(pos, yg, cap=cap)


def kernel(x, c, ctx, c_ctx, w_ada, b_ada, norm1_g, norm2_g, w_in, w_out, lambda_q1, lambda_k1, lambda_q2, lambda_k2, subln_g, sgu_norm_g, sgu_w, sgu_b, w_router, w_gate, w_up, w_down, norm_f_g):
    b, n, d = x.shape
    m = ctx.shape[1]
    depth = w_in.shape[0]
    sw = sgu_norm_g.shape[1]
    aw = (w_in.shape[2] - 2 * sw) // 3
    heads = aw // (2 * HEAD_DIM)
    bf = lambda t: t.astype(MXU_DTYPE)

    rows = -(-(b + 1) // 8) * 8
    cc = jnp.concatenate([c, c_ctx[None, :], jnp.zeros((rows - b - 1, d), F32)], axis=0)
    mod = _ada(cc, w_ada, b_ada)

    order = _qk_column_order(aw)
    col_perm = jnp.concatenate([order, aw + order, jnp.arange(2 * aw, w_in.shape[2])])
    cos, sin = _rope_tables(n, heads)
    cos_c = jnp.ones((m, aw), F32)
    sin_c = jnp.zeros((m, aw), F32)
    hw_s = sw // SGU_HEADS
    ones_blk = bf(jnp.kron(jnp.eye(SGU_HEADS, dtype=F32), jnp.full((hw_s, hw_s), 1.0 / hw_s, F32)))
    tri = bf(jnp.triu(jnp.ones((LANES, LANES), F32)))

    cx = ctx
    res_x = res_c = None
    for l in range(depth):
        last = l == depth - 1
        lam_init = 0.8 - 0.6 * math.exp(-0.3 * l)
        parts = [mod[l, :, i * d:(i + 1) * d] for i in range(6)]
        sh1, sc1, g1, sh2, sc2, g2 = [t[:b, None, :] for t in parts]
        csh1, csc1, cg1, csh2, csc2, cg2 = [jnp.broadcast_to(t[b][None, None, :], (b, 1, d)) for t in parts]

        w_in_p = bf(w_in[l][:, col_perm])
        w_out_l = bf(w_out[l])
        wr_t = bf(w_router[l].T)
        wg, wu, wd = bf(w_gate[l]), bf(w_up[l]), bf(w_down[l])
        ng1, ng2 = norm1_g[l][None, :], norm2_g[l][None, :]
        sng = sgu_norm_g[l][None, :]
        ws = bf(sgu_w[l].reshape(SGU_HEADS * CHUNK, CHUNK))
        bs = jnp.repeat(sgu_b[l].T, hw_s, axis=1)
        lam_params = jnp.zeros((8, LANES), F32).at[:4, :HEAD_DIM].set(
            jnp.stack([lambda_q1[l], lambda_k1[l], lambda_q2[l], lambda_k2[l]]))
        sub_g = subln_g[l][None, :]
        shared = dict(aw=aw, sw=sw)

        x, (q, k, v, sg) = _proj(x, res_x, ng1, sh1, sc1, w_in_p, cos, sin, sng, ws, bs, ones_blk, rope=True, **shared)
        cx, (qc, kc, vc, sgc) = _proj(cx, res_c, ng1, csh1, csc1, w_in_p, cos_c, sin_c, sng, ws, bs, ones_blk,
                                      rope=False, **shared)
        o = _attn(lam_params, q, jnp.concatenate([kc, k], axis=1), jnp.concatenate([vc, v], axis=1), sub_g,
                  lam_init=lam_init)
        x, hm, aff_t = _outproj(o, sg, w_out_l, x, g1, ng2, sh2, sc2, wr_t)
        res_x = (_moe(hm, aff_t, tri, wg, wu, wd), g2)
        if not last:
            oc = _attn(lam_params, qc, kc, vc, sub_g, lam_init=lam_init)
            cx, hmc, aff_c = _outproj(oc, sgc, w_out_l, cx, cg1, ng2, csh2, csc2, wr_t)
            res_c = (_moe(hmc, aff_c, tri, wg, wu, wd), cg2)
    return _final(x, res_x[0], res_x[1], norm_f_g[None, :])
```

```python
import functools
import math

import jax
import jax.numpy as jnp
from jax import lax
from jax.experimental import pallas as pl
from jax.experimental.pallas import tpu as pltpu

F32 = jnp.float32
MXU_DTYPE = jnp.bfloat16

EPS = 1e-6
GRID_W = 64
ROPE_THETA = 10000.0
HEAD_DIM = 64
SGU_HEADS = 8
CHUNK = 128
N_EXPERTS = 16
CAPACITY_FACTOR = 2
LANES = 128
VMEM_LIMIT = 56 * 1024 * 1024


def _cparams(*sem):
    return pltpu.CompilerParams(dimension_semantics=sem, vmem_limit_bytes=VMEM_LIMIT)


def _dot(a, b):
    return jnp.dot(a.astype(MXU_DTYPE), b.astype(MXU_DTYPE), preferred_element_type=F32)


def _dot_nt(a, b):
    return lax.dot_general(a.astype(MXU_DTYPE), b.astype(MXU_DTYPE), (((1,), (1,)), ((), ())),
                           preferred_element_type=F32)


def _dot_tn(a, b):
    return lax.dot_general(a.astype(MXU_DTYPE), b.astype(MXU_DTYPE), (((0,), (0,)), ((), ())),
                           preferred_element_type=F32)


def _split_hi_lo(x):
    hi = x.astype(MXU_DTYPE)
    lo = (x - hi.astype(F32)).astype(MXU_DTYPE)
    return hi, lo


def _silu(x):
    return x * jax.nn.sigmoid(x)


def _gelu(x):
    return 0.5 * x * (1.0 + lax.erf(x * (2.0 ** -0.5)))


def _tile(n, pref):
    t = pref
    while t > 8 and n % t:
        t //= 2
    return t if n % t == 0 else n


def _ada_kernel(c_ref, w_ref, b_ref, o_ref):
    o_ref[0] = _dot(_silu(c_ref[...]), w_ref[0]) + b_ref[0]


def _ada(cc, w_ada, b_ada):
    depth, d, n6 = w_ada.shape
    rows = cc.shape[0]
    tn = _tile(n6, 1536)
    return pl.pallas_call(
        _ada_kernel,
        grid=(depth, n6 // tn),
        in_specs=[pl.BlockSpec((rows, d), lambda l, j: (0, 0)),
                  pl.BlockSpec((1, d, tn), lambda l, j: (l, 0, j)),
                  pl.BlockSpec((1, 1, tn), lambda l, j: (l, 0, j))],
        out_specs=pl.BlockSpec((1, rows, tn), lambda l, j: (l, 0, j)),
        out_shape=jax.ShapeDtypeStruct((depth, rows, n6), F32),
        compiler_params=_cparams("parallel", "parallel"), name="ada",
    )(cc, w_ada, b_ada.reshape(depth, 1, n6))


def _proj_kernel(*refs, has_res, rope, aw, sw):
    if has_res:
        x_ref, moe_ref, g2_ref, *refs = refs
    else:
        x_ref, *refs = refs
    (ng_ref, sh_ref, sc_ref, w_ref, cos_ref, sin_ref, sng_ref, ws_ref, bs_ref, ones_ref, *outs) = refs
    if has_res:
        xo_ref, q_ref, k_ref, v_ref, sg_ref = outs
    else:
        q_ref, k_ref, v_ref, sg_ref = outs

    x = x_ref[0]
    if has_res:
        x = x + g2_ref[0] * moe_ref[0]
        xo_ref[0] = x
    h = x * lax.rsqrt(jnp.mean(x * x, axis=-1, keepdims=True) + EPS) * ng_ref[...]
    h = h * (1.0 + sc_ref[0]) + sh_ref[0]
    p = _dot(h, w_ref[...])

    q = p[:, :aw]
    k = p[:, aw:2 * aw]
    if rope:
        first = (lax.broadcasted_iota(jnp.int32, q.shape, 1) % 32) < 16
        cos = cos_ref[...]
        sin = sin_ref[...]

        def rot(t):
            return jnp.where(first, pltpu.roll(t, aw - 16, 1), pltpu.roll(t, 16, 1))

        q = q * cos + rot(q) * sin
        k = k * cos + rot(k) * sin
    q_ref[0] = (q * (HEAD_DIM ** -0.5 * math.log2(math.e))).astype(q_ref.dtype)
    k_ref[0] = k.astype(k_ref.dtype)
    v_ref[0] = p[:, 2 * aw:3 * aw].astype(v_ref.dtype)

    u = _gelu(p[:, 3 * aw:3 * aw + sw])
    gv = _gelu(p[:, 3 * aw + sw:])
    sq_hi, sq_lo = _split_hi_lo(gv * gv)
    ms = _dot(sq_hi, ones_ref[...]) + _dot(sq_lo, ones_ref[...])
    gvn = (gv * lax.rsqrt(ms + EPS) * sng_ref[...]).astype(MXU_DTYPE)
    head_of_lane = lax.broadcasted_iota(jnp.int32, (CHUNK, sw), 1) // (sw // SGU_HEADS)
    tm = x.shape[0]
    for ci in range(tm // CHUNK):
        rows = slice(ci * CHUNK, (ci + 1) * CHUNK)
        r = _dot(ws_ref[...], gvn[rows])
        s = bs_ref[...]
        for hh in range(SGU_HEADS):
            s = s + jnp.where(head_of_lane == hh, r[hh * CHUNK:(hh + 1) * CHUNK], 0.0)
        sg_ref[0, rows, :] = (u[rows] * s).astype(sg_ref.dtype)


def _proj(x, res, ng, sh, sc, w_in_p, cos, sin, sng, ws, bs, ones_blk, *, rope, aw, sw):
    b, n, d = x.shape
    pw = w_in_p.shape[1]
    tm = _tile(n, 512)
    has_res = res is not None
    tok = lambda bi, i: (bi, i, 0)
    per_b = lambda bi, i: (bi, 0, 0)
    const2 = lambda bi, i: (0, 0)
    in_specs = [pl.BlockSpec((1, tm, d), tok)]
    args = [x]
    if has_res:
        moe, g2 = res
        in_specs += [pl.BlockSpec((1, tm, d), tok), pl.BlockSpec((1, 1, d), per_b)]
        args += [moe, g2]
    in_specs += [pl.BlockSpec((1, d), const2), pl.BlockSpec((1, 1, d), per_b), pl.BlockSpec((1, 1, d), per_b),
                 pl.BlockSpec((d, pw), const2),
                 pl.BlockSpec((tm, aw), lambda bi, i: (i, 0)), pl.BlockSpec((tm, aw), lambda bi, i: (i, 0)),
                 pl.BlockSpec((1, sw), const2), pl.BlockSpec((SGU_HEADS * CHUNK, CHUNK), const2),
                 pl.BlockSpec((CHUNK, sw), const2), pl.BlockSpec((sw, sw), const2)]
    args += [ng, sh, sc, w_in_p, cos, sin, sng, ws, bs, ones_blk]
    out_specs = [pl.BlockSpec((1, tm, aw), tok)] * 3 + [pl.BlockSpec((1, tm, sw), tok)]
    out_shape = [jax.ShapeDtypeStruct((b, n, aw), MXU_DTYPE)] * 3 + [jax.ShapeDtypeStruct((b, n, sw), MXU_DTYPE)]
    if has_res:
        out_specs = [pl.BlockSpec((1, tm, d), tok)] + out_specs
        out_shape = [jax.ShapeDtypeStruct((b, n, d), F32)] + out_shape
    outs = pl.pallas_call(
        functools.partial(_proj_kernel, has_res=has_res, rope=rope, aw=aw, sw=sw),
        grid=(b, n // tm),
        in_specs=in_specs, out_specs=out_specs, out_shape=out_shape,
        compiler_params=_cparams("parallel", "parallel"), name="proj_lat" if rope else "proj_ctx",
    )(*args)
    if has_res:
        return outs[0], outs[1:]
    return x, outs


def _attn_kernel(lam_ref, q_ref, k_ref, v_ref, g_ref, o_ref, s0_ref, s1_ref, p0_ref, p1_ref, *, tq, lam_init):
    s_ref = (s0_ref, s1_ref)
    p_ref = (p0_ref, p1_ref)
    hw = q_ref.shape[2]
    nt = q_ref.shape[1] // tq
    lane = lax.broadcasted_iota(jnp.int32, (tq, hw), 1)
    lp = lam_ref[...]
    lam = (jnp.exp(jnp.sum(lp[0:1] * lp[1:2], axis=-1, keepdims=True))
           - jnp.exp(jnp.sum(lp[2:3] * lp[3:4], axis=-1, keepdims=True)) + lam_init)

    def rows(t):
        return pl.ds(pl.multiple_of(t * tq, tq), tq)

    def scores(t, slot):
        q = q_ref[0, rows(t), :]
        for c in range(2):
            qc = jnp.where((lane >= HEAD_DIM) == (c == 1), q, jnp.zeros_like(q))
            s_ref[slot][c] = _dot_nt(qc, k_ref[0])

    def softmax(slot):
        for c in range(2):
            s = s_ref[slot][c]
            p_ref[slot][c] = jnp.exp2(s - jnp.max(s, axis=-1, keepdims=True)).astype(MXU_DTYPE)

    def values(t, slot):
        vals = v_ref[0]
        vals1 = jnp.concatenate([vals, jnp.ones_like(vals)], axis=1)
        comps = []
        for c in range(2):
            pv = _dot(p_ref[slot][c], vals1)
            comps.append(pv[:, :hw] / pv[:, hw:])
        o = comps[0] - lam * comps[1]
        o = o * lax.rsqrt(jnp.mean(o * o, axis=-1, keepdims=True) + EPS) * g_ref[...] * (1.0 - lam_init)
        o_ref[0, rows(t), :] = o.astype(o_ref.dtype)

    if nt % 2:
        for t in range(nt):
            scores(t, 0)
            softmax(0)
            values(t, 0)
        return

    scores(0, 0)
    scores(1, 1)
    softmax(0)

    def phase(t, carry):
        for slot in range(2):
            @pl.when(t % 2 == slot)
            def _():
                scores(t + 1, 1 - slot)
                softmax(slot)
                values(t - 1, 1 - slot)
        return carry

    lax.fori_loop(1, nt - 1, phase, 0)
    softmax(1)
    values(nt - 2, 0)
    values(nt - 1, 1)


def _attn(lam_params, q, k, v, subln_g, *, lam_init):
    b, n, aw = q.shape
    nkeys = k.shape[1]
    hw = 2 * HEAD_DIM
    heads = aw // hw
    tq = _tile(n, 256)
    return pl.pallas_call(
        functools.partial(_attn_kernel, tq=tq, lam_init=lam_init),
        grid=(b, heads),
        in_specs=[pl.BlockSpec(lam_params.shape, lambda bi, h: (0, 0)),
                  pl.BlockSpec((1, n, hw), lambda bi, h: (bi, 0, h)),
                  pl.BlockSpec((1, nkeys, hw), lambda bi, h: (bi, 0, h)),
                  pl.BlockSpec((1, nkeys, hw), lambda bi, h: (bi, 0, h)),
                  pl.BlockSpec((1, hw), lambda bi, h: (0, 0))],
        out_specs=pl.BlockSpec((1, n, hw), lambda bi, h: (bi, 0, h)),
        out_shape=jax.ShapeDtypeStruct((b, n, aw), MXU_DTYPE),
        scratch_shapes=[pltpu.VMEM((2, tq, nkeys), F32)] * 2 + [pltpu.VMEM((2, tq, nkeys), MXU_DTYPE)] * 2,
        compiler_params=_cparams("parallel", "parallel"), name=f"attn_{n}",
    )(lam_params, q, k, v, subln_g)


def _out_kernel(o_ref, sg_ref, w_ref, x_ref, g1_ref, ng_ref, sh_ref, sc_ref, wr_ref, xo_ref, hm_ref, aff_ref, *, aw):
    y = _dot(o_ref[0], w_ref[:aw, :]) + _dot(sg_ref[0], w_ref[aw:, :])
    x = x_ref[0] + g1_ref[0] * y
    xo_ref[0] = x
    h = x * lax.rsqrt(jnp.mean(x * x, axis=-1, keepdims=True) + EPS) * ng_ref[...]
    h = h * (1.0 + sc_ref[0]) + sh_ref[0]
    hm_ref[0] = h.astype(hm_ref.dtype)
    logits = _dot_nt(wr_ref[...], h)
    e = jnp.exp(logits - jnp.max(logits, axis=0, keepdims=True))
    aff_ref[0] = e / jnp.sum(e, axis=0, keepdims=True)


def _outproj(o, sg, w_out, x, g1, ng, sh, sc, wr_t):
    b, n, d = x.shape
    aw = o.shape[2]
    sw = sg.shape[2]
    ne = wr_t.shape[0]
    tm = _tile(n, 512)
    tok = lambda bi, i: (bi, i, 0)
    per_b = lambda bi, i: (bi, 0, 0)
    const2 = lambda bi, i: (0, 0)
    return pl.pallas_call(
        functools.partial(_out_kernel, aw=aw),
        grid=(b, n // tm),
        in_specs=[pl.BlockSpec((1, tm, aw), tok), pl.BlockSpec((1, tm, sw), tok),
                  pl.BlockSpec((aw + sw, d), const2), pl.BlockSpec((1, tm, d), tok),
                  pl.BlockSpec((1, 1, d), per_b), pl.BlockSpec((1, d), const2),
                  pl.BlockSpec((1, 1, d), per_b), pl.BlockSpec((1, 1, d), per_b),
                  pl.BlockSpec((ne, d), const2)],
        out_specs=[pl.BlockSpec((1, tm, d), tok), pl.BlockSpec((1, tm, d), tok),
                   pl.BlockSpec((1, ne, tm), lambda bi, i: (bi, 0, i))],
        out_shape=[jax.ShapeDtypeStruct((b, n, d), F32), jax.ShapeDtypeStruct((b, n, d), MXU_DTYPE),
                   jax.ShapeDtypeStruct((b, ne, n), F32)],
        compiler_params=_cparams("parallel", "parallel"), name=f"outproj_{n}",
    )(o, sg, w_out, x, g1, ng, sh, sc, wr_t)


def _route_kernel(aff_ref, tri_ref, pos_ref, *, cap):
    a = aff_ref[0]
    ne, n = a.shape
    bits = lax.bitcast_convert_type(a, jnp.int32)
    t = jnp.zeros((ne, 1), jnp.int32)
    for bit in range(30, -1, -1):
        cand = t | (1 << bit)
        cnt = jnp.sum((bits >= cand).astype(jnp.int32), axis=1, keepdims=True)
        t = jnp.where(cnt >= cap, cand, t)
    gt = bits > t
    eq = bits == t
    need = cap - jnp.sum(gt.astype(jnp.int32), axis=1, keepdims=True)

    def exclusive_count(mask):
        out = []
        run = jnp.zeros((ne, 1), F32)
        for ci in range(n // LANES):
            m = mask[:, ci * LANES:(ci + 1) * LANES].astype(F32)
            incl = _dot(m, tri_ref[...])
            out.append(run + incl - m)
            run = run + incl[:, LANES - 1:LANES]
        return jnp.concatenate(out, axis=1)

    sel = gt | (eq & (exclusive_count(eq) < need.astype(F32)))
    pos_ref[0] = jnp.where(sel, exclusive_count(sel), -1.0)


def _route(aff_t, tri, *, cap):
    b, ne, n = aff_t.shape
    return pl.pallas_call(
        functools.partial(_route_kernel, cap=cap),
        grid=(b,),
        in_specs=[pl.BlockSpec((1, ne, n), lambda bi: (bi, 0, 0)), pl.BlockSpec((LANES, LANES), lambda bi: (0, 0))],
        out_specs=pl.BlockSpec((1, ne, n), lambda bi: (bi, 0, 0)),
        out_shape=jax.ShapeDtypeStruct((b, ne, n), F32),
        compiler_params=_cparams("parallel"), name=f"route_{n}",
    )(aff_t, tri)


def _gather_kernel(pos_ref, aff_ref, hm_ref, xs_ref, gs_ref, *, cap, tn):
    n = hm_ref.shape[1]
    d = hm_ref.shape[2]
    slot = lax.broadcasted_iota(jnp.int32, (cap, tn), 0).astype(F32)
    xs = jnp.zeros((cap, d), F32)
    gs = jnp.zeros((cap, 1), F32)
    for ci in range(n // tn):
        cols = slice(ci * tn, (ci + 1) * tn)
        hit = pos_ref[0, 0, :, cols] == slot
        xs = xs + _dot(hit.astype(MXU_DTYPE), hm_ref[0, cols, :])
        gs = gs + jnp.sum(jnp.where(hit, aff_ref[0, 0, :, cols], 0.0), axis=1, keepdims=True)
    xs_ref[0] = xs.astype(xs_ref.dtype)
    gs_ref[0] = jnp.broadcast_to(gs, (cap, LANES))


def _gather(pos, aff_t, hm, *, cap):
    b, ne, n = pos.shape
    d = hm.shape[2]
    tn = _tile(n, 512)
    pos4 = pos.reshape(b, ne, 1, n)
    aff4 = aff_t.reshape(b, ne, 1, n)
    return pl.pallas_call(
        functools.partial(_gather_kernel, cap=cap, tn=tn),
        grid=(b, ne),
        in_specs=[pl.BlockSpec((1, 1, 1, n), lambda bi, e: (bi, e, 0, 0)),
                  pl.BlockSpec((1, 1, 1, n), lambda bi, e: (bi, e, 0, 0)),
                  pl.BlockSpec((1, n, d), lambda bi, e: (bi, 0, 0))],
        out_specs=[pl.BlockSpec((1, cap, d), lambda bi, e: (e, bi, 0)),
                   pl.BlockSpec((1, cap, LANES), lambda bi, e: (e, bi, 0))],
        out_shape=[jax.ShapeDtypeStruct((ne, b * cap, d), MXU_DTYPE),
                   jax.ShapeDtypeStruct((ne, b * cap, LANES), F32)],
        compiler_params=_cparams("parallel", "parallel"), name=f"gather_{n}",
    )(pos4, aff4, hm)


def _ffn_kernel(x_ref, wg_ref, wu_ref, wd_ref, gs_ref, y_ref, acc_ref):
    f = pl.program_id(2)

    @pl.when(f == 0)
    def _():
        acc_ref[...] = jnp.zeros_like(acc_ref)

    x = x_ref[0]
    hid = _silu(_dot(x, wg_ref[0])) * _dot(x, wu_ref[0])
    acc_ref[...] += _dot(hid, wd_ref[0])

    @pl.when(f == pl.num_programs(2) - 1)
    def _():
        y_ref[0] = acc_ref[...] * gs_ref[0][:, :1]


def _ffn(xs, w_gate, w_up, w_down, gs):
    ne, r, d = xs.shape
    ff = w_gate.shape[2]
    tm = _tile(r, 1024)
    tf = _tile(ff, 512)
    return pl.pallas_call(
        _ffn_kernel,
        grid=(ne, r // tm, ff // tf),
        in_specs=[pl.BlockSpec((1, tm, d), lambda e, i, f: (e, i, 0)),
                  pl.BlockSpec((1, d, tf), lambda e, i, f: (e, 0, f)),
                  pl.BlockSpec((1, d, tf), lambda e, i, f: (e, 0, f)),
                  pl.BlockSpec((1, tf, d), lambda e, i, f: (e, f, 0)),
                  pl.BlockSpec((1, tm, LANES), lambda e, i, f: (e, i, 0))],
        out_specs=pl.BlockSpec((1, tm, d), lambda e, i, f: (e, i, 0)),
        out_shape=jax.ShapeDtypeStruct((ne, r, d), F32),
        scratch_shapes=[pltpu.VMEM((tm, d), F32)],
        compiler_params=_cparams("parallel", "parallel", "arbitrary"), name=f"ffn_{r}",
    )(xs, w_gate, w_up, w_down, gs)


def _combine_kernel(pos_ref, y_ref, o_ref, *, cap, tn):
    e = pl.program_id(1)

    @pl.when(e == 0)
    def _():
        o_ref[...] = jnp.zeros_like(o_ref)

    n = o_ref.shape[1]
    y_hi, y_lo = _split_hi_lo(y_ref[0])
    slot = lax.broadcasted_iota(jnp.int32, (cap, tn), 0).astype(F32)
    for ci in range(n // tn):
        cols = slice(ci * tn, (ci + 1) * tn)
        hit = (pos_ref[0, 0, :, cols] == slot).astype(MXU_DTYPE)
        o_ref[0, cols, :] += _dot_tn(hit, y_hi) + _dot_tn(hit, y_lo)


def _combine(pos, yg, *, cap):
    b, ne, n = pos.shape
    d = yg.shape[2]
    tn = _tile(n, 512)
    return pl.pallas_call(
        functools.partial(_combine_kernel, cap=cap, tn=tn),
        grid=(b, ne),
        in_specs=[pl.BlockSpec((1, 1, 1, n), lambda bi, e: (bi, e, 0, 0)),
                  pl.BlockSpec((1, cap, d), lambda bi, e: (e, bi, 0))],
        out_specs=pl.BlockSpec((1, n, d), lambda bi, e: (bi, 0, 0)),
        out_shape=jax.ShapeDtypeStruct((b, n, d), F32),
        compiler_params=_cparams("parallel", "arbitrary"), name=f"combine_{n}",
    )(pos.reshape(b, ne, 1, n), yg)


def _final_kernel(x_ref, moe_ref, g2_ref, ng_ref, o_ref):
    x = x_ref[0] + g2_ref[0] * moe_ref[0]
    o_ref[0] = x * lax.rsqrt(jnp.mean(x * x, axis=-1, keepdims=True) + EPS) * ng_ref[...]


def _final(x, moe, g2, ng):
    b, n, d = x.shape
    tm = _tile(n, 512)
    tok = lambda bi, i: (bi, i, 0)
    return pl.pallas_call(
        _final_kernel,
        grid=(b, n // tm),
        in_specs=[pl.BlockSpec((1, tm, d), tok), pl.BlockSpec((1, tm, d), tok),
                  pl.BlockSpec((1, 1, d), lambda bi, i: (bi, 0, 0)), pl.BlockSpec((1, d), lambda bi, i: (0, 0))],
        out_specs=pl.BlockSpec((1, tm, d), tok),
        out_shape=jax.ShapeDtypeStruct((b, n, d), F32),
        compiler_params=_cparams("parallel", "parallel"), name="final",
    )(x, moe, g2, ng)


def _rope_tables(n, heads):
    rows = n // GRID_W
    row = jnp.repeat(jnp.arange(rows), GRID_W).astype(F32)
    col = jnp.tile(jnp.arange(GRID_W), rows).astype(F32)
    n_freq = HEAD_DIM // 4
    freqs = ROPE_THETA ** (-jnp.arange(n_freq, dtype=F32) / n_freq)
    ang_r = row[:, None] * freqs
    ang_c = col[:, None] * freqs
    ang = jnp.concatenate([ang_r, ang_r, ang_c, ang_c], axis=-1)
    sign = jnp.where((jnp.arange(HEAD_DIM) % 32) < 16, -1.0, 1.0).astype(F32)
    return jnp.tile(jnp.cos(ang), (1, 2 * heads)), jnp.tile(jnp.sin(ang) * sign, (1, 2 * heads))


def _head_major(w):
    depth, d, aw = w.shape
    heads = aw // (2 * HEAD_DIM)
    return w.reshape(depth, d, 2, heads, HEAD_DIM).transpose(0, 1, 3, 2, 4).reshape(depth, d, aw)


def _moe(hm, aff_t, tri, w_gate, w_up, w_down):
    n = hm.shape[1]
    cap = CAPACITY_FACTOR * n // N_EXPERTS
    pos = _route(aff_t, tri, cap=cap)
    xs, gs = _gather(pos, aff_t, hm, cap=cap)
    yg = _ffn(xs, w_gate, w_up, w_down, gs)
    return _combine(pos, yg, cap=cap)


def kernel(x, c, ctx, c_ctx, w_ada, b_ada, norm1_g, norm2_g, w_in, w_out, lambda_q1, lambda_k1, lambda_q2, lambda_k2, subln_g, sgu_norm_g, sgu_w, sgu_b, w_router, w_gate, w_up, w_down, norm_f_g):
    b, n, d = x.shape
    m = ctx.shape[1]
    depth = w_in.shape[0]
    sw = sgu_norm_g.shape[1]
    aw = (w_in.shape[2] - 2 * sw) // 3
    heads = aw // (2 * HEAD_DIM)
    bf = lambda t: t.astype(MXU_DTYPE)

    rows = -(-(b + 1) // 8) * 8
    cc = jnp.concatenate([c, c_ctx[None, :], jnp.zeros((rows - b - 1, d), F32)], axis=0)
    mod = _ada(cc, w_ada, b_ada)

    w_in_all = bf(jnp.concatenate([_head_major(w_in[..., :aw]), _head_major(w_in[..., aw:2 * aw]),
                                   w_in[..., 2 * aw:]], axis=-1))
    cos, sin = _rope_tables(n, heads)
    cos_c = jnp.ones((m, aw), F32)
    sin_c = jnp.zeros((m, aw), F32)
    hw_s = sw // SGU_HEADS
    ones_blk = bf(jnp.kron(jnp.eye(SGU_HEADS, dtype=F32), jnp.full((hw_s, hw_s), 1.0 / hw_s, F32)))
    tri = bf(jnp.triu(jnp.ones((LANES, LANES), F32)))

    cx = ctx
    res_x = res_c = None
    for l in range(depth):
        last = l == depth - 1
        lam_init = 0.8 - 0.6 * math.exp(-0.3 * l)
        parts = [mod[l, :, i * d:(i + 1) * d] for i in range(6)]
        sh1, sc1, g1, sh2, sc2, g2 = [t[:b, None, :] for t in parts]
        csh1, csc1, cg1, csh2, csc2, cg2 = [jnp.broadcast_to(t[b][None, None, :], (b, 1, d)) for t in parts]

        w_in_p = w_in_all[l]
        w_out_l = bf(w_out[l])
        wr_t = bf(w_router[l].T)
        wg, wu, wd = bf(w_gate[l]), bf(w_up[l]), bf(w_down[l])
        ng1, ng2 = norm1_g[l][None, :], norm2_g[l][None, :]
        sng = sgu_norm_g[l][None, :]
        ws = bf(sgu_w[l].reshape(SGU_HEADS * CHUNK, CHUNK))
        bs = jnp.repeat(sgu_b[l].T, hw_s, axis=1)
        lam_params = jnp.zeros((8, LANES), F32).at[:4, :HEAD_DIM].set(
            jnp.stack([lambda_q1[l], lambda_k1[l], lambda_q2[l], lambda_k2[l]]))
        sub_g = subln_g[l][None, :]
        shared = dict(aw=aw, sw=sw)

        x, (q, k, v, sg) = _proj(x, res_x, ng1, sh1, sc1, w_in_p, cos, sin, sng, ws, bs, ones_blk, rope=True, **shared)
        cx, (qc, kc, vc, sgc) = _proj(cx, res_c, ng1, csh1, csc1, w_in_p, cos_c, sin_c, sng, ws, bs, ones_blk,
                                      rope=False, **shared)
        o = _attn(lam_params, q, jnp.concatenate([kc, k], axis=1), jnp.concatenate([vc, v], axis=1), sub_g,
                  lam_init=lam_init)
        x, hm, aff_t = _outproj(o, sg, w_out_l, x, g1, ng2, sh2, sc2, wr_t)
        res_x = (_moe(hm, aff_t, tri, wg, wu, wd), g2)
        if not last:
            oc = _attn(lam_params, qc, kc, vc, sub_g, lam_init=lam_init)
            cx, hmc, aff_c = _outproj(oc, sgc, w_out_l, cx, cg1, ng2, csh2, csc2, wr_t)
            res_c = (_moe(hmc, aff_c, tri, wg, wu, wd), cg2)
    return _final(x, res_x[0], res_x[1], norm_f_g[None, :])
```

```python
import functools
import math

import jax
import jax.numpy as jnp
from jax import lax
from jax.experimental import pallas as pl
from jax.experimental.pallas import tpu as pltpu

F32 = jnp.float32
MXU_DTYPE = jnp.bfloat16

EPS = 1e-6
GRID_W = 64
ROPE_THETA = 10000.0
HEAD_DIM = 64
SGU_HEADS = 8
CHUNK = 128
N_EXPERTS = 16
CAPACITY_FACTOR = 2
LANES = 128
SLOT_WINDOW = 128
VMEM_LIMIT = 56 * 1024 * 1024


def _cparams(*sem):
    return pltpu.CompilerParams(dimension_semantics=sem, vmem_limit_bytes=VMEM_LIMIT)


def _dot(a, b):
    return jnp.dot(a.astype(MXU_DTYPE), b.astype(MXU_DTYPE), preferred_element_type=F32)


def _dot_nt(a, b):
    return lax.dot_general(a.astype(MXU_DTYPE), b.astype(MXU_DTYPE), (((1,), (1,)), ((), ())),
                           preferred_element_type=F32)


def _dot_tn(a, b):
    return lax.dot_general(a.astype(MXU_DTYPE), b.astype(MXU_DTYPE), (((0,), (0,)), ((), ())),
                           preferred_element_type=F32)


def _split_hi_lo(x):
    hi = x.astype(MXU_DTYPE)
    lo = (x - hi.astype(F32)).astype(MXU_DTYPE)
    return hi, lo


def _silu(x):
    return x * jax.nn.sigmoid(x)


def _gelu(x):
    return 0.5 * x * (1.0 + lax.erf(x * (2.0 ** -0.5)))


def _tile(n, pref):
    t = pref
    while t > 8 and n % t:
        t //= 2
    return t if n % t == 0 else n


def _ada_kernel(c_ref, w_ref, b_ref, o_ref):
    o_ref[0] = _dot(_silu(c_ref[...]), w_ref[0]) + b_ref[0]


def _ada(cc, w_ada, b_ada):
    depth, d, n6 = w_ada.shape
    rows = cc.shape[0]
    tn = _tile(n6, 1536)
    return pl.pallas_call(
        _ada_kernel,
        grid=(depth, n6 // tn),
        in_specs=[pl.BlockSpec((rows, d), lambda l, j: (0, 0)),
                  pl.BlockSpec((1, d, tn), lambda l, j: (l, 0, j)),
                  pl.BlockSpec((1, 1, tn), lambda l, j: (l, 0, j))],
        out_specs=pl.BlockSpec((1, rows, tn), lambda l, j: (l, 0, j)),
        out_shape=jax.ShapeDtypeStruct((depth, rows, n6), F32),
        compiler_params=_cparams("parallel", "parallel"), name="ada",
    )(cc, w_ada, b_ada.reshape(depth, 1, n6))


def _proj_kernel(*refs, has_res, rope, aw, sw):
    if has_res:
        x_ref, moe_ref, g2_ref, *refs = refs
    else:
        x_ref, *refs = refs
    (ng_ref, sh_ref, sc_ref, w_ref, cos_ref, sin_ref, sng_ref, ws_ref, bs_ref, ones_ref, *outs) = refs
    if has_res:
        xo_ref, q_ref, k_ref, v_ref, sg_ref = outs
    else:
        q_ref, k_ref, v_ref, sg_ref = outs

    x = x_ref[0]
    if has_res:
        x = x + g2_ref[0] * moe_ref[0]
        xo_ref[0] = x
    h = x * lax.rsqrt(jnp.mean(x * x, axis=-1, keepdims=True) + EPS) * ng_ref[...]
    h = h * (1.0 + sc_ref[0]) + sh_ref[0]
    p = _dot(h, w_ref[...])

    q = p[:, :aw]
    k = p[:, aw:2 * aw]
    if rope:
        first = (lax.broadcasted_iota(jnp.int32, q.shape, 1) % 32) < 16
        cos = cos_ref[...]
        sin = sin_ref[...]

        def rot(t):
            return jnp.where(first, pltpu.roll(t, aw - 16, 1), pltpu.roll(t, 16, 1))

        q = q * cos + rot(q) * sin
        k = k * cos + rot(k) * sin
    q_ref[0] = (q * (HEAD_DIM ** -0.5 * math.log2(math.e))).astype(q_ref.dtype)
    k_ref[0] = k.astype(k_ref.dtype)
    v_ref[0] = p[:, 2 * aw:3 * aw].astype(v_ref.dtype)

    u = _gelu(p[:, 3 * aw:3 * aw + sw])
    gv = _gelu(p[:, 3 * aw + sw:])
    sq_hi, sq_lo = _split_hi_lo(gv * gv)
    ms = _dot(sq_hi, ones_ref[...]) + _dot(sq_lo, ones_ref[...])
    gvn = (gv * lax.rsqrt(ms + EPS) * sng_ref[...]).astype(MXU_DTYPE)
    head_of_lane = lax.broadcasted_iota(jnp.int32, (CHUNK, sw), 1) // (sw // SGU_HEADS)
    tm = x.shape[0]
    for ci in range(tm // CHUNK):
        rows = slice(ci * CHUNK, (ci + 1) * CHUNK)
        r = _dot(ws_ref[...], gvn[rows])
        s = bs_ref[...]
        for hh in range(SGU_HEADS):
            s = s + jnp.where(head_of_lane == hh, r[hh * CHUNK:(hh + 1) * CHUNK], 0.0)
        sg_ref[0, rows, :] = (u[rows] * s).astype(sg_ref.dtype)


def _proj(x, res, ng, sh, sc, w_in_p, cos, sin, sng, ws, bs, ones_blk, *, rope, aw, sw):
    b, n, d = x.shape
    pw = w_in_p.shape[1]
    tm = _tile(n, 512)
    has_res = res is not None
    tok = lambda bi, i: (bi, i, 0)
    per_b = lambda bi, i: (bi, 0, 0)
    const2 = lambda bi, i: (0, 0)
    in_specs = [pl.BlockSpec((1, tm, d), tok)]
    args = [x]
    if has_res:
        moe, g2 = res
        in_specs += [pl.BlockSpec((1, tm, d), tok), pl.BlockSpec((1, 1, d), per_b)]
        args += [moe, g2]
    in_specs += [pl.BlockSpec((1, d), const2), pl.BlockSpec((1, 1, d), per_b), pl.BlockSpec((1, 1, d), per_b),
                 pl.BlockSpec((d, pw), const2),
                 pl.BlockSpec((tm, aw), lambda bi, i: (i, 0)), pl.BlockSpec((tm, aw), lambda bi, i: (i, 0)),
                 pl.BlockSpec((1, sw), const2), pl.BlockSpec((SGU_HEADS * CHUNK, CHUNK), const2),
                 pl.BlockSpec((CHUNK, sw), const2), pl.BlockSpec((sw, sw), const2)]
    args += [ng, sh, sc, w_in_p, cos, sin, sng, ws, bs, ones_blk]
    out_specs = [pl.BlockSpec((1, tm, aw), tok)] * 3 + [pl.BlockSpec((1, tm, sw), tok)]
    out_shape = [jax.ShapeDtypeStruct((b, n, aw), MXU_DTYPE)] * 3 + [jax.ShapeDtypeStruct((b, n, sw), MXU_DTYPE)]
    if has_res:
        out_specs = [pl.BlockSpec((1, tm, d), tok)] + out_specs
        out_shape = [jax.ShapeDtypeStruct((b, n, d), F32)] + out_shape
    outs = pl.pallas_call(
        functools.partial(_proj_kernel, has_res=has_res, rope=rope, aw=aw, sw=sw),
        grid=(b, n // tm),
        in_specs=in_specs, out_specs=out_specs, out_shape=out_shape,
        compiler_params=_cparams("parallel", "parallel"), name="proj_lat" if rope else "proj_ctx",
    )(*args)
    if has_res:
        return outs[0], outs[1:]
    return x, outs


def _attn_kernel(lam_ref, q_ref, k_ref, v_ref, g_ref, o_ref, s0_ref, s1_ref, p0_ref, p1_ref, *, tq, lam_init):
    s_ref = (s0_ref, s1_ref)
    p_ref = (p0_ref, p1_ref)
    hw = q_ref.shape[2]
    nt = q_ref.shape[1] // tq
    lane = lax.broadcasted_iota(jnp.int32, (tq, hw), 1)
    lp = lam_ref[...]
    lam = (jnp.exp(jnp.sum(lp[0:1] * lp[1:2], axis=-1, keepdims=True))
           - jnp.exp(jnp.sum(lp[2:3] * lp[3:4], axis=-1, keepdims=True)) + lam_init)

    def rows(t):
        return pl.ds(pl.multiple_of(t * tq, tq), tq)

    def scores(t, slot):
        q = q_ref[0, rows(t), :]
        for c in range(2):
            qc = jnp.where((lane >= HEAD_DIM) == (c == 1), q, jnp.zeros_like(q))
            s_ref[slot][c] = _dot_nt(qc, k_ref[0])

    def softmax(slot):
        for c in range(2):
            s = s_ref[slot][c]
            p_ref[slot][c] = jnp.exp2(s - jnp.max(s, axis=-1, keepdims=True)).astype(MXU_DTYPE)

    def values(t, slot):
        vals = v_ref[0]
        vals1 = jnp.concatenate([vals, jnp.ones_like(vals)], axis=1)
        comps = []
        for c in range(2):
            pv = _dot(p_ref[slot][c], vals1)
            comps.append(pv[:, :hw] / pv[:, hw:])
        o = comps[0] - lam * comps[1]
        o = o * lax.rsqrt(jnp.mean(o * o, axis=-1, keepdims=True) + EPS) * g_ref[...] * (1.0 - lam_init)
        o_ref[0, rows(t), :] = o.astype(o_ref.dtype)

    if nt % 2:
        for t in range(nt):
            scores(t, 0)
            softmax(0)
            values(t, 0)
        return

    scores(0, 0)
    scores(1, 1)
    softmax(0)

    def phase(t, carry):
        for slot in range(2):
            @pl.when(t % 2 == slot)
            def _():
                scores(t + 1, 1 - slot)
                softmax(slot)
                values(t - 1, 1 - slot)
        return carry

    lax.fori_loop(1, nt - 1, phase, 0)
    softmax(1)
    values(nt - 2, 0)
    values(nt - 1, 1)


def _attn(lam_params, q, k, v, subln_g, *, lam_init):
    b, n, aw = q.shape
    nkeys = k.shape[1]
    hw = 2 * HEAD_DIM
    heads = aw // hw
    tq = _tile(n, 256)
    return pl.pallas_call(
        functools.partial(_attn_kernel, tq=tq, lam_init=lam_init),
        grid=(b, heads),
        in_specs=[pl.BlockSpec(lam_params.shape, lambda bi, h: (0, 0)),
                  pl.BlockSpec((1, n, hw), lambda bi, h: (bi, 0, h)),
                  pl.BlockSpec((1, nkeys, hw), lambda bi, h: (bi, 0, h)),
                  pl.BlockSpec((1, nkeys, hw), lambda bi, h: (bi, 0, h)),
                  pl.BlockSpec((1, hw), lambda bi, h: (0, 0))],
        out_specs=pl.BlockSpec((1, n, hw), lambda bi, h: (bi, 0, h)),
        out_shape=jax.ShapeDtypeStruct((b, n, aw), MXU_DTYPE),
        scratch_shapes=[pltpu.VMEM((2, tq, nkeys), F32)] * 2 + [pltpu.VMEM((2, tq, nkeys), MXU_DTYPE)] * 2,
        compiler_params=_cparams("parallel", "parallel"), name=f"attn_{n}",
    )(lam_params, q, k, v, subln_g)


def _out_kernel(o_ref, sg_ref, w_ref, x_ref, g1_ref, ng_ref, sh_ref, sc_ref, wr_ref, xo_ref, hm_ref, aff_ref, *, aw):
    y = _dot(o_ref[0], w_ref[:aw, :]) + _dot(sg_ref[0], w_ref[aw:, :])
    x = x_ref[0] + g1_ref[0] * y
    xo_ref[0] = x
    h = x * lax.rsqrt(jnp.mean(x * x, axis=-1, keepdims=True) + EPS) * ng_ref[...]
    h = h * (1.0 + sc_ref[0]) + sh_ref[0]
    hm_ref[0] = h.astype(hm_ref.dtype)
    logits = _dot_nt(wr_ref[...], h)
    e = jnp.exp(logits - jnp.max(logits, axis=0, keepdims=True))
    aff_ref[0] = e / jnp.sum(e, axis=0, keepdims=True)


def _outproj(o, sg, w_out, x, g1, ng, sh, sc, wr_t):
    b, n, d = x.shape
    aw = o.shape[2]
    sw = sg.shape[2]
    ne = wr_t.shape[0]
    tm = _tile(n, 512)
    tok = lambda bi, i: (bi, i, 0)
    per_b = lambda bi, i: (bi, 0, 0)
    const2 = lambda bi, i: (0, 0)
    return pl.pallas_call(
        functools.partial(_out_kernel, aw=aw),
        grid=(b, n // tm),
        in_specs=[pl.BlockSpec((1, tm, aw), tok), pl.BlockSpec((1, tm, sw), tok),
                  pl.BlockSpec((aw + sw, d), const2), pl.BlockSpec((1, tm, d), tok),
                  pl.BlockSpec((1, 1, d), per_b), pl.BlockSpec((1, d), const2),
                  pl.BlockSpec((1, 1, d), per_b), pl.BlockSpec((1, 1, d), per_b),
                  pl.BlockSpec((ne, d), const2)],
        out_specs=[pl.BlockSpec((1, tm, d), tok), pl.BlockSpec((1, tm, d), tok),
                   pl.BlockSpec((1, ne, tm), lambda bi, i: (bi, 0, i))],
        out_shape=[jax.ShapeDtypeStruct((b, n, d), F32), jax.ShapeDtypeStruct((b, n, d), MXU_DTYPE),
                   jax.ShapeDtypeStruct((b, ne, n), F32)],
        compiler_params=_cparams("parallel", "parallel"), name=f"outproj_{n}",
    )(o, sg, w_out, x, g1, ng, sh, sc, wr_t)


def _route_kernel(aff_ref, tri_ref, pos_ref, start_ref, *, cap, tt):
    a = aff_ref[0]
    ne, n = a.shape
    bits = lax.bitcast_convert_type(a, jnp.int32)
    t = jnp.zeros((ne, 1), jnp.int32)
    for bit in range(30, -1, -1):
        cand = t | (1 << bit)
        cnt = jnp.sum((bits >= cand).astype(jnp.int32), axis=1, keepdims=True)
        t = jnp.where(cnt >= cap, cand, t)
    gt = bits > t
    eq = bits == t
    need = cap - jnp.sum(gt.astype(jnp.int32), axis=1, keepdims=True)

    def exclusive_count(mask):
        out = []
        runs = []
        run = jnp.zeros((ne, 1), F32)
        for ci in range(n // LANES):
            runs.append(run)
            m = mask[:, ci * LANES:(ci + 1) * LANES].astype(F32)
            incl = _dot(m, tri_ref[...])
            out.append(run + incl - m)
            run = run + incl[:, LANES - 1:LANES]
        return jnp.concatenate(out, axis=1), runs

    sel = gt | (eq & (exclusive_count(eq)[0] < need.astype(F32)))
    pos, runs = exclusive_count(sel)
    pos_ref[0] = jnp.where(sel, pos, -1.0)
    lane = lax.broadcasted_iota(jnp.int32, (ne, LANES), 1)
    first = jnp.full((ne, LANES), float(cap), F32)
    for j in range(n // tt):
        first = jnp.where(lane == j, runs[j * tt // LANES], first)
    start_ref[0] = first


def _route(aff_t, tri, *, cap, tt):
    b, ne, n = aff_t.shape
    return pl.pallas_call(
        functools.partial(_route_kernel, cap=cap, tt=tt),
        grid=(b,),
        in_specs=[pl.BlockSpec((1, ne, n), lambda bi: (bi, 0, 0)), pl.BlockSpec((LANES, LANES), lambda bi: (0, 0))],
        out_specs=[pl.BlockSpec((1, ne, n), lambda bi: (bi, 0, 0)), pl.BlockSpec((1, ne, LANES), lambda bi: (bi, 0, 0))],
        out_shape=[jax.ShapeDtypeStruct((b, ne, n), F32), jax.ShapeDtypeStruct((b, ne, LANES), F32)],
        compiler_params=_cparams("parallel"), name=f"route_{n}",
    )(aff_t, tri)


def _window_hits(a_ref, pos_ref, *, win):
    bi, t, nt = pl.program_id(0), pl.program_id(1), pl.num_programs(1)
    ne, tt = pos_ref.shape[1], pos_ref.shape[2]
    base = lax.broadcasted_iota(jnp.int32, (win, tt), 0)
    offs, hits = [], []
    for e in range(ne):
        a_e = a_ref[(bi * ne + e) * nt + t]
        offs.append(a_e)
        hits.append(pos_ref[0, e:e + 1, :] == (base + a_e).astype(F32))
    return offs, hits


def _gather_win_kernel(a_ref, pos_ref, aff_ref, hm_ref, xs_ref, gs_ref, *, cap, win):
    @pl.when(pl.program_id(1) == 0)
    def _():
        xs_ref[...] = jnp.zeros_like(xs_ref)
        gs_ref[...] = jnp.zeros_like(gs_ref)

    ne = pos_ref.shape[1]
    offs, hits = _window_hits(a_ref, pos_ref, win=win)
    group = 8
    for e0 in range(0, ne, group):
        onehot = jnp.concatenate([h.astype(MXU_DTYPE) for h in hits[e0:e0 + group]], axis=0)
        picked = _dot(onehot, hm_ref[0]).astype(xs_ref.dtype)
        for e in range(e0, min(e0 + group, ne)):
            rows = pl.ds(pl.multiple_of(e * cap + offs[e], 16), win)
            gate = jnp.sum(jnp.where(hits[e], aff_ref[0, e:e + 1, :], 0.0), axis=1, keepdims=True)
            xs_ref[0, rows, :] += picked[(e - e0) * win:(e - e0 + 1) * win]
            gs_ref[0, rows, :] += jnp.broadcast_to(gate, (win, LANES))


def _gather_win(a_flat, pos, aff_t, hm, *, cap, win, tt):
    b, ne, n = pos.shape
    d = hm.shape[2]
    tile = lambda bi, t, a: (bi, 0, t)
    whole = lambda bi, t, a: (bi, 0, 0)
    return pl.pallas_call(
        functools.partial(_gather_win_kernel, cap=cap, win=win),
        grid_spec=pltpu.PrefetchScalarGridSpec(
            num_scalar_prefetch=1, grid=(b, n // tt),
            in_specs=[pl.BlockSpec((1, ne, tt), tile), pl.BlockSpec((1, ne, tt), tile),
                      pl.BlockSpec((1, tt, d), lambda bi, t, a: (bi, t, 0))],
            out_specs=[pl.BlockSpec((1, ne * cap, d), whole), pl.BlockSpec((1, ne * cap, LANES), whole)]),
        out_shape=[jax.ShapeDtypeStruct((b, ne * cap, d), MXU_DTYPE),
                   jax.ShapeDtypeStruct((b, ne * cap, LANES), F32)],
        compiler_params=_cparams("parallel", "arbitrary"), name=f"gather_win_{n}",
    )(a_flat, pos, aff_t, hm)


def _gather_kernel(pos_ref, aff_ref, hm_ref, xs_ref, gs_ref, *, cap, tn):
    n = hm_ref.shape[1]
    d = hm_ref.shape[2]
    slot = lax.broadcasted_iota(jnp.int32, (cap, tn), 0).astype(F32)
    xs = jnp.zeros((cap, d), F32)
    gs = jnp.zeros((cap, 1), F32)
    for ci in range(n // tn):
        cols = slice(ci * tn, (ci + 1) * tn)
        hit = pos_ref[0, 0, :, cols] == slot
        xs = xs + _dot(hit.astype(MXU_DTYPE), hm_ref[0, cols, :])
        gs = gs + jnp.sum(jnp.where(hit, aff_ref[0, 0, :, cols], 0.0), axis=1, keepdims=True)
    xs_ref[0] = xs.astype(xs_ref.dtype)
    gs_ref[0] = jnp.broadcast_to(gs, (cap, LANES))


def _gather(pos, aff_t, hm, *, cap):
    b, ne, n = pos.shape
    d = hm.shape[2]
    tn = _tile(n, 512)
    pos4 = pos.reshape(b, ne, 1, n)
    aff4 = aff_t.reshape(b, ne, 1, n)
    return pl.pallas_call(
        functools.partial(_gather_kernel, cap=cap, tn=tn),
        grid=(b, ne),
        in_specs=[pl.BlockSpec((1, 1, 1, n), lambda bi, e: (bi, e, 0, 0)),
                  pl.BlockSpec((1, 1, 1, n), lambda bi, e: (bi, e, 0, 0)),
                  pl.BlockSpec((1, n, d), lambda bi, e: (bi, 0, 0))],
        out_specs=[pl.BlockSpec((1, cap, d), lambda bi, e: (bi, e, 0)),
                   pl.BlockSpec((1, cap, LANES), lambda bi, e: (bi, e, 0))],
        out_shape=[jax.ShapeDtypeStruct((b, ne * cap, d), MXU_DTYPE),
                   jax.ShapeDtypeStruct((b, ne * cap, LANES), F32)],
        compiler_params=_cparams("parallel", "parallel"), name=f"gather_{n}",
    )(pos4, aff4, hm)


def _ffn_kernel(x_ref, wg_ref, wu_ref, wd_ref, gs_ref, y_ref, acc_ref):
    f = pl.program_id(2)

    @pl.when(f == 0)
    def _():
        acc_ref[...] = jnp.zeros_like(acc_ref)

    tm, d = acc_ref.shape
    x = x_ref[...].reshape(tm, d)
    hid = _silu(_dot(x, wg_ref[0])) * _dot(x, wu_ref[0])
    acc_ref[...] += _dot(hid, wd_ref[0])

    @pl.when(f == pl.num_programs(2) - 1)
    def _():
        y_ref[0] = acc_ref[...] * gs_ref[...].reshape(tm, LANES)[:, :1]


def _ffn(xs, w_gate, w_up, w_down, gs, *, cap):
    b, rows, d = xs.shape
    ne, _, ff = w_gate.shape
    bt = b
    while bt * cap > 1024 and bt % 2 == 0:
        bt //= 2
    tm = bt * cap
    tf = _tile(ff, 512)
    return pl.pallas_call(
        _ffn_kernel,
        grid=(ne, b // bt, ff // tf),
        in_specs=[pl.BlockSpec((bt, 1, cap, d), lambda e, i, f: (i, e, 0, 0)),
                  pl.BlockSpec((1, d, tf), lambda e, i, f: (e, 0, f)),
                  pl.BlockSpec((1, d, tf), lambda e, i, f: (e, 0, f)),
                  pl.BlockSpec((1, tf, d), lambda e, i, f: (e, f, 0)),
                  pl.BlockSpec((bt, 1, cap, LANES), lambda e, i, f: (i, e, 0, 0))],
        out_specs=pl.BlockSpec((1, tm, d), lambda e, i, f: (e, i, 0)),
        out_shape=jax.ShapeDtypeStruct((ne, b * cap, d), F32),
        scratch_shapes=[pltpu.VMEM((tm, d), F32)],
        compiler_params=_cparams("parallel", "parallel", "arbitrary"), name=f"ffn_{b * cap}",
    )(xs.reshape(b, ne, cap, d), w_gate, w_up, w_down, gs.reshape(b, ne, cap, LANES))


def _combine_win_kernel(a_ref, pos_ref, *refs, win):
    y_refs, o_ref = refs[:-1], refs[-1]
    d = o_ref.shape[2]
    _, hits = _window_hits(a_ref, pos_ref, win=win)
    onehot = jnp.concatenate([h.astype(MXU_DTYPE) for h in hits], axis=0)
    parts = [_split_hi_lo(y_ref[0]) for y_ref in y_refs]
    y_all = jnp.concatenate([jnp.concatenate([p[0] for p in parts], axis=0),
                             jnp.concatenate([p[1] for p in parts], axis=0)], axis=1)
    r = _dot_tn(onehot, y_all)
    o_ref[0] = r[:, :d] + r[:, d:]


def _combine_win(a_flat, pos, yg, *, cap, win, tt):
    b, ne, n = pos.shape
    d = yg.shape[2]
    nt = n // tt

    def window(e):
        return pl.BlockSpec((pl.Element(1), pl.Element(win), pl.Element(d)),
                            lambda bi, t, a: (e, pl.multiple_of(bi * cap + a[(bi * ne + e) * nt + t], 16), 0))

    return pl.pallas_call(
        functools.partial(_combine_win_kernel, win=win),
        grid_spec=pltpu.PrefetchScalarGridSpec(
            num_scalar_prefetch=1, grid=(b, nt),
            in_specs=[pl.BlockSpec((1, ne, tt), lambda bi, t, a: (bi, 0, t))] + [window(e) for e in range(ne)],
            out_specs=pl.BlockSpec((1, tt, d), lambda bi, t, a: (bi, t, 0))),
        out_shape=jax.ShapeDtypeStruct((b, n, d), F32),
        compiler_params=_cparams("parallel", "parallel"), name=f"combine_win_{n}",
    )(a_flat, pos, *([yg] * ne))


def _combine_kernel(pos_ref, y_ref, o_ref, *, cap, tn):
    e = pl.program_id(1)

    @pl.when(e == 0)
    def _():
        o_ref[...] = jnp.zeros_like(o_ref)

    n = o_ref.shape[1]
    y_hi, y_lo = _split_hi_lo(y_ref[0])
    slot = lax.broadcasted_iota(jnp.int32, (cap, tn), 0).astype(F32)
    for ci in range(n // tn):
        cols = slice(ci * tn, (ci + 1) * tn)
        hit = (pos_ref[0, 0, :, cols] == slot).astype(MXU_DTYPE)
        o_ref[0, cols, :] += _dot_tn(hit, y_hi) + _dot_tn(hit, y_lo)


def _combine(pos, yg, *, cap):
    b, ne, n = pos.shape
    d = yg.shape[2]
    tn = _tile(n, 512)
    return pl.pallas_call(
        functools.partial(_combine_kernel, cap=cap, tn=tn),
        grid=(b, ne),
        in_specs=[pl.BlockSpec((1, 1, 1, n), lambda bi, e: (bi, e, 0, 0)),
                  pl.BlockSpec((1, cap, d), lambda bi, e: (e, bi, 0))],
        out_specs=pl.BlockSpec((1, n, d), lambda bi, e: (bi, 0, 0)),
        out_shape=jax.ShapeDtypeStruct((b, n, d), F32),
        compiler_params=_cparams("parallel", "arbitrary"), name=f"combine_{n}",
    )(pos.reshape(b, ne, 1, n), yg)


def _final_kernel(x_ref, moe_ref, g2_ref, ng_ref, o_ref):
    x = x_ref[0] + g2_ref[0] * moe_ref[0]
    o_ref[0] = x * lax.rsqrt(jnp.mean(x * x, axis=-1, keepdims=True) + EPS) * ng_ref[...]


def _final(x, moe, g2, ng):
    b, n, d = x.shape
    tm = _tile(n, 512)
    tok = lambda bi, i: (bi, i, 0)
    return pl.pallas_call(
        _final_kernel,
        grid=(b, n // tm),
        in_specs=[pl.BlockSpec((1, tm, d), tok), pl.BlockSpec((1, tm, d), tok),
                  pl.BlockSpec((1, 1, d), lambda bi, i: (bi, 0, 0)), pl.BlockSpec((1, d), lambda bi, i: (0, 0))],
        out_specs=pl.BlockSpec((1, tm, d), tok),
        out_shape=jax.ShapeDtypeStruct((b, n, d), F32),
        compiler_params=_cparams("parallel", "parallel"), name="final",
    )(x, moe, g2, ng)


def _rope_tables(n, heads):
    rows = n // GRID_W
    row = jnp.repeat(jnp.arange(rows), GRID_W).astype(F32)
    col = jnp.tile(jnp.arange(GRID_W), rows).astype(F32)
    n_freq = HEAD_DIM // 4
    freqs = ROPE_THETA ** (-jnp.arange(n_freq, dtype=F32) / n_freq)
    ang_r = row[:, None] * freqs
    ang_c = col[:, None] * freqs
    ang = jnp.concatenate([ang_r, ang_r, ang_c, ang_c], axis=-1)
    sign = jnp.where((jnp.arange(HEAD_DIM) % 32) < 16, -1.0, 1.0).astype(F32)
    return jnp.tile(jnp.cos(ang), (1, 2 * heads)), jnp.tile(jnp.sin(ang) * sign, (1, 2 * heads))


def _head_major(w):
    depth, d, aw = w.shape
    heads = aw // (2 * HEAD_DIM)
    return w.reshape(depth, d, 2, heads, HEAD_DIM).transpose(0, 1, 3, 2, 4).reshape(depth, d, aw)


def _moe(hm, aff_t, tri, w_gate, w_up, w_down):
    b, n, _ = hm.shape
    ne = aff_t.shape[1]
    cap = CAPACITY_FACTOR * n // ne
    tt = _tile(n, 512)
    nt = n // tt
    win = min(SLOT_WINDOW, cap)
    pos, first = _route(aff_t, tri, cap=cap, tt=tt)
    first = first[:, :, :nt + 1].astype(jnp.int32)
    offs = jnp.minimum(first[:, :, :nt] // 16 * 16, cap - win)
    fits = jnp.all(first[:, :, 1:] <= offs + win)
    a_flat = offs.reshape(-1)
    xs, gs = lax.cond(fits,
                      lambda: _gather_win(a_flat, pos, aff_t, hm, cap=cap, win=win, tt=tt),
                      lambda: _gather(pos, aff_t, hm, cap=cap))
    yg = _ffn(xs, w_gate, w_up, w_down, gs, cap=cap)
    return lax.cond(fits,
                    lambda: _combine_win(a_flat, pos, yg, cap=cap, win=win, tt=tt),
                    lambda: _combine(pos, yg, cap=cap))


def kernel(x, c, ctx, c_ctx, w_ada, b_ada, norm1_g, norm2_g, w_in, w_out, lambda_q1, lambda_k1, lambda_q2, lambda_k2, subln_g, sgu_norm_g, sgu_w, sgu_b, w_router, w_gate, w_up, w_down, norm_f_g):
    b, n, d = x.shape
    m = ctx.shape[1]
    depth = w_in.shape[0]
    sw = sgu_norm_g.shape[1]
    aw = (w_in.shape[2] - 2 * sw) // 3
    heads = aw // (2 * HEAD_DIM)
    bf = lambda t: t.astype(MXU_DTYPE)

    rows = -(-(b + 1) // 8) * 8
    cc = jnp.concatenate([c, c_ctx[None, :], jnp.zeros((rows - b - 1, d), F32)], axis=0)
    mod = _ada(cc, w_ada, b_ada)

    w_in_all = bf(jnp.concatenate([_head_major(w_in[..., :aw]), _head_major(w_in[..., aw:2 * aw]),
                                   w_in[..., 2 * aw:]], axis=-1))
    cos, sin = _rope_tables(n, heads)
    cos_c = jnp.ones((m, aw), F32)
    sin_c = jnp.zeros((m, aw), F32)
    hw_s = sw // SGU_HEADS
    ones_blk = bf(jnp.kron(jnp.eye(SGU_HEADS, dtype=F32), jnp.full((hw_s, hw_s), 1.0 / hw_s, F32)))
    tri = bf(jnp.triu(jnp.ones((LANES, LANES), F32)))

    cx = ctx
    res_x = res_c = None
    for l in range(depth):
        last = l == depth - 1
        lam_init = 0.8 - 0.6 * math.exp(-0.3 * l)
        parts = [mod[l, :, i * d:(i + 1) * d] for i in range(6)]
        sh1, sc1, g1, sh2, sc2, g2 = [t[:b, None, :] for t in parts]
        csh1, csc1, cg1, csh2, csc2, cg2 = [jnp.broadcast_to(t[b][None, None, :], (b, 1, d)) for t in parts]

        w_in_p = w_in_all[l]
        w_out_l = bf(w_out[l])
        wr_t = bf(w_router[l].T)
        wg, wu, wd = bf(w_gate[l]), bf(w_up[l]), bf(w_down[l])
        ng1, ng2 = norm1_g[l][None, :], norm2_g[l][None, :]
        sng = sgu_norm_g[l][None, :]
        ws = bf(sgu_w[l].reshape(SGU_HEADS * CHUNK, CHUNK))
        bs = jnp.repeat(sgu_b[l].T, hw_s, axis=1)
        lam_params = jnp.zeros((8, LANES), F32).at[:4, :HEAD_DIM].set(
            jnp.stack([lambda_q1[l], lambda_k1[l], lambda_q2[l], lambda_k2[l]]))
        sub_g = subln_g[l][None, :]
        shared = dict(aw=aw, sw=sw)

        x, (q, k, v, sg) = _proj(x, res_x, ng1, sh1, sc1, w_in_p, cos, sin, sng, ws, bs, ones_blk, rope=True, **shared)
        cx, (qc, kc, vc, sgc) = _proj(cx, res_c, ng1, csh1, csc1, w_in_p, cos_c, sin_c, sng, ws, bs, ones_blk,
                                      rope=False, **shared)
        o = _attn(lam_params, q, jnp.concatenate([kc, k], axis=1), jnp.concatenate([vc, v], axis=1), sub_g,
                  lam_init=lam_init)
        x, hm, aff_t = _outproj(o, sg, w_out_l, x, g1, ng2, sh2, sc2, wr_t)
        res_x = (_moe(hm, aff_t, tri, wg, wu, wd), g2)
        if not last:
            oc = _attn(lam_params, qc, kc, vc, sub_g, lam_init=lam_init)
            cx, hmc, aff_c = _outproj(oc, sgc, w_out_l, cx, cg1, ng2, csh2, csc2, wr_t)
            res_c = (_moe(hmc, aff_c, tri, wg, wu, wd), cg2)
    return _final(x, res_x[0], res_x[1], norm_f_g[None, :])
```

```python
import functools
import math

import jax
import jax.numpy as jnp
from jax import lax
from jax.experimental import pallas as pl
from jax.experimental.pallas import tpu as pltpu

F32 = jnp.float32
MXU_DTYPE = jnp.bfloat16

EPS = 1e-6
GRID_W = 64
ROPE_THETA = 10000.0
HEAD_DIM = 64
SGU_HEADS = 8
CHUNK = 128
N_EXPERTS = 16
CAPACITY_FACTOR = 2
LANES = 128
SLOT_WINDOW = 128
VMEM_LIMIT = 56 * 1024 * 1024


def _cparams(*sem):
    return pltpu.CompilerParams(dimension_semantics=sem, vmem_limit_bytes=VMEM_LIMIT)


def _dot(a, b):
    return jnp.dot(a.astype(MXU_DTYPE), b.astype(MXU_DTYPE), preferred_element_type=F32)


def _dot_nt(a, b):
    return lax.dot_general(a.astype(MXU_DTYPE), b.astype(MXU_DTYPE), (((1,), (1,)), ((), ())),
                           preferred_element_type=F32)


def _dot_tn(a, b):
    return lax.dot_general(a.astype(MXU_DTYPE), b.astype(MXU_DTYPE), (((0,), (0,)), ((), ())),
                           preferred_element_type=F32)


def _split_hi_lo(x):
    hi = x.astype(MXU_DTYPE)
    lo = (x - hi.astype(F32)).astype(MXU_DTYPE)
    return hi, lo


def _silu(x):
    return x * jax.nn.sigmoid(x)


def _gelu(x):
    return 0.5 * x * (1.0 + lax.erf(x * (2.0 ** -0.5)))


def _tile(n, pref):
    t = pref
    while t > 8 and n % t:
        t //= 2
    return t if n % t == 0 else n


def _ada_kernel(c_ref, w_ref, b_ref, o_ref):
    o_ref[0] = _dot(_silu(c_ref[...]), w_ref[0]) + b_ref[0]


def _ada(cc, w_ada, b_ada):
    depth, d, n6 = w_ada.shape
    rows = cc.shape[0]
    tn = _tile(n6, 1536)
    return pl.pallas_call(
        _ada_kernel,
        grid=(depth, n6 // tn),
        in_specs=[pl.BlockSpec((rows, d), lambda l, j: (0, 0)),
                  pl.BlockSpec((1, d, tn), lambda l, j: (l, 0, j)),
                  pl.BlockSpec((1, 1, tn), lambda l, j: (l, 0, j))],
        out_specs=pl.BlockSpec((1, rows, tn), lambda l, j: (l, 0, j)),
        out_shape=jax.ShapeDtypeStruct((depth, rows, n6), F32),
        compiler_params=_cparams("parallel", "parallel"), name="ada",
    )(cc, w_ada, b_ada.reshape(depth, 1, n6))


def _proj_kernel(*refs, has_res, rope, aw, sw):
    if has_res:
        x_ref, moe_ref, g2_ref, *refs = refs
    else:
        x_ref, *refs = refs
    (ng_ref, sh_ref, sc_ref, w_ref, cos_ref, sin_ref, sng_ref, ws_ref, bs_ref, ones_ref, *outs) = refs
    if has_res:
        xo_ref, q_ref, k_ref, v_ref, sg_ref = outs
    else:
        q_ref, k_ref, v_ref, sg_ref = outs

    x = x_ref[0]
    if has_res:
        x = x + g2_ref[0] * moe_ref[0]
        xo_ref[0] = x
    h = x * lax.rsqrt(jnp.mean(x * x, axis=-1, keepdims=True) + EPS) * ng_ref[...]
    h = h * (1.0 + sc_ref[0]) + sh_ref[0]
    p = _dot(h, w_ref[...])

    q = p[:, :aw]
    k = p[:, aw:2 * aw]
    if rope:
        first = (lax.broadcasted_iota(jnp.int32, q.shape, 1) % 32) < 16
        cos = cos_ref[...]
        sin = sin_ref[...]

        def rot(t):
            return jnp.where(first, pltpu.roll(t, aw - 16, 1), pltpu.roll(t, 16, 1))

        q = q * cos + rot(q) * sin
        k = k * cos + rot(k) * sin
    q_ref[0] = (q * (HEAD_DIM ** -0.5 * math.log2(math.e))).astype(q_ref.dtype)
    k_ref[0] = k.astype(k_ref.dtype)
    v_ref[0] = p[:, 2 * aw:3 * aw].astype(v_ref.dtype)

    u = _gelu(p[:, 3 * aw:3 * aw + sw])
    gv = _gelu(p[:, 3 * aw + sw:])
    sq_hi, sq_lo = _split_hi_lo(gv * gv)
    ms = _dot(sq_hi, ones_ref[...]) + _dot(sq_lo, ones_ref[...])
    gvn = (gv * lax.rsqrt(ms + EPS) * sng_ref[...]).astype(MXU_DTYPE)
    head_of_lane = lax.broadcasted_iota(jnp.int32, (CHUNK, sw), 1) // (sw // SGU_HEADS)
    tm = x.shape[0]
    for ci in range(tm // CHUNK):
        rows = slice(ci * CHUNK, (ci + 1) * CHUNK)
        r = _dot(ws_ref[...], gvn[rows])
        s = bs_ref[...]
        for hh in range(SGU_HEADS):
            s = s + jnp.where(head_of_lane == hh, r[hh * CHUNK:(hh + 1) * CHUNK], 0.0)
        sg_ref[0, rows, :] = (u[rows] * s).astype(sg_ref.dtype)


def _proj(x, res, ng, sh, sc, w_in_p, cos, sin, sng, ws, bs, ones_blk, *, rope, aw, sw):
    b, n, d = x.shape
    pw = w_in_p.shape[1]
    tm = _tile(n, 512)
    has_res = res is not None
    tok = lambda bi, i: (bi, i, 0)
    per_b = lambda bi, i: (bi, 0, 0)
    const2 = lambda bi, i: (0, 0)
    in_specs = [pl.BlockSpec((1, tm, d), tok)]
    args = [x]
    if has_res:
        moe, g2 = res
        in_specs += [pl.BlockSpec((1, tm, d), tok), pl.BlockSpec((1, 1, d), per_b)]
        args += [moe, g2]
    in_specs += [pl.BlockSpec((1, d), const2), pl.BlockSpec((1, 1, d), per_b), pl.BlockSpec((1, 1, d), per_b),
                 pl.BlockSpec((d, pw), const2),
                 pl.BlockSpec((tm, aw), lambda bi, i: (i, 0)), pl.BlockSpec((tm, aw), lambda bi, i: (i, 0)),
                 pl.BlockSpec((1, sw), const2), pl.BlockSpec((SGU_HEADS * CHUNK, CHUNK), const2),
                 pl.BlockSpec((CHUNK, sw), const2), pl.BlockSpec((sw, sw), const2)]
    args += [ng, sh, sc, w_in_p, cos, sin, sng, ws, bs, ones_blk]
    out_specs = [pl.BlockSpec((1, tm, aw), tok)] * 3 + [pl.BlockSpec((1, tm, sw), tok)]
    out_shape = [jax.ShapeDtypeStruct((b, n, aw), MXU_DTYPE)] * 3 + [jax.ShapeDtypeStruct((b, n, sw), MXU_DTYPE)]
    if has_res:
        out_specs = [pl.BlockSpec((1, tm, d), tok)] + out_specs
        out_shape = [jax.ShapeDtypeStruct((b, n, d), F32)] + out_shape
    outs = pl.pallas_call(
        functools.partial(_proj_kernel, has_res=has_res, rope=rope, aw=aw, sw=sw),
        grid=(b, n // tm),
        in_specs=in_specs, out_specs=out_specs, out_shape=out_shape,
        compiler_params=_cparams("parallel", "parallel"), name="proj_lat" if rope else "proj_ctx",
    )(*args)
    if has_res:
        return outs[0], outs[1:]
    return x, outs


def _attn_kernel(lam_ref, q0_ref, q1_ref, k0_ref, k1_ref, v_ref, g_ref, o_ref,
                 s0_ref, s1_ref, m0_ref, m1_ref, *, tq, tkc, lam_init):
    q_ref, k_ref = (q0_ref, q1_ref), (k0_ref, k1_ref)
    s_ref, m_ref = (s0_ref, s1_ref), (m0_ref, m1_ref)
    nkeys, hw = v_ref.shape[1], v_ref.shape[2]
    nt = q0_ref.shape[1] // tq
    lane = lax.broadcasted_iota(jnp.int32, (tq, hw), 1)
    mine = (lane >= HEAD_DIM) == (pl.program_id(1) % 2 == 1)
    lp = lam_ref[...]
    lam = (jnp.exp(jnp.sum(lp[0:1] * lp[1:2], axis=-1, keepdims=True))
           - jnp.exp(jnp.sum(lp[2:3] * lp[3:4], axis=-1, keepdims=True)) + lam_init)

    def rows(t):
        return pl.ds(t * tq if isinstance(t, int) else pl.multiple_of(t * tq, tq), tq)

    def phase(t, slot, *, do_scores=True, do_values=True):
        other = 1 - slot
        if do_scores:
            qc = []
            for c in range(2):
                q = q_ref[c][0, rows(t + 1), :]
                qc.append(jnp.where(mine, q, jnp.zeros_like(q)))
            run = [None, None]
        if do_values:
            top = [jnp.max(m_ref[slot][c], axis=-1, keepdims=True) for c in range(2)]
            acc = [jnp.zeros((tq, 2 * hw), F32), jnp.zeros((tq, 2 * hw), F32)]
        for kc in range(nkeys // tkc):
            cols = slice(kc * tkc, (kc + 1) * tkc)
            if do_values:
                vals = v_ref[0, cols, :]
                vals1 = jnp.concatenate([vals, jnp.ones_like(vals)], axis=1)
            for c in range(2):
                if do_scores:
                    s = _dot_nt(qc[c], k_ref[c][0, cols, :])
                    s_ref[other][c, :, cols] = s
                    for j in range(tkc // LANES):
                        part = s[:, j * LANES:(j + 1) * LANES]
                        run[c] = part if run[c] is None else jnp.maximum(run[c], part)
                if do_values:
                    p = jnp.exp2(s_ref[slot][c, :, cols] - top[c])
                    acc[c] = acc[c] + _dot(p, vals1)
        if do_scores:
            for c in range(2):
                m_ref[other][c] = run[c]
        if do_values:
            o = acc[0][:, :hw] / acc[0][:, hw:] - lam * (acc[1][:, :hw] / acc[1][:, hw:])
            o = o * lax.rsqrt(jnp.mean(o * o, axis=-1, keepdims=True) + EPS) * g_ref[...] * (1.0 - lam_init)
            o_ref[0, rows(t), :] = o.astype(o_ref.dtype)

    phase(-1, 1, do_values=False)

    def step(t, carry):
        for slot in range(2):
            @pl.when(t % 2 == slot)
            def _():
                phase(t, slot)
        return carry

    lax.fori_loop(0, nt - 1, step, 0)
    phase(nt - 1, (nt - 1) % 2, do_scores=False)


def _attn(lam_params, q, k, v, subln_g, *, lam_init):
    b, n, aw = q.shape
    nkeys = k.shape[1]
    hw = 2 * HEAD_DIM
    heads = aw // hw
    pairs = heads // 2
    tq = _tile(n, 256)
    tkc = _tile(nkeys, 256)
    comp0 = lambda bi, h: (bi, 0, h // 2)
    comp1 = lambda bi, h: (bi, 0, pairs + h // 2)
    return pl.pallas_call(
        functools.partial(_attn_kernel, tq=tq, tkc=tkc, lam_init=lam_init),
        grid=(b, heads),
        in_specs=[pl.BlockSpec(lam_params.shape, lambda bi, h: (0, 0)),
                  pl.BlockSpec((1, n, hw), comp0), pl.BlockSpec((1, n, hw), comp1),
                  pl.BlockSpec((1, nkeys, hw), comp0), pl.BlockSpec((1, nkeys, hw), comp1),
                  pl.BlockSpec((1, nkeys, hw), lambda bi, h: (bi, 0, h)),
                  pl.BlockSpec((1, hw), lambda bi, h: (0, 0))],
        out_specs=pl.BlockSpec((1, n, hw), lambda bi, h: (bi, 0, h)),
        out_shape=jax.ShapeDtypeStruct((b, n, aw), MXU_DTYPE),
        scratch_shapes=[pltpu.VMEM((2, tq, nkeys), F32)] * 2 + [pltpu.VMEM((2, tq, LANES), F32)] * 2,
        compiler_params=_cparams("parallel", "parallel"), name=f"attn_{n}",
    )(lam_params, q, q, k, k, v, subln_g)


def _out_kernel(o_ref, sg_ref, w_ref, x_ref, g1_ref, ng_ref, sh_ref, sc_ref, wr_ref, xo_ref, hm_ref, aff_ref, *, aw):
    y = _dot(o_ref[0], w_ref[:aw, :]) + _dot(sg_ref[0], w_ref[aw:, :])
    x = x_ref[0] + g1_ref[0] * y
    xo_ref[0] = x
    h = x * lax.rsqrt(jnp.mean(x * x, axis=-1, keepdims=True) + EPS) * ng_ref[...]
    h = h * (1.0 + sc_ref[0]) + sh_ref[0]
    hm_ref[0] = h.astype(hm_ref.dtype)
    logits = _dot_nt(wr_ref[...], h)
    e = jnp.exp(logits - jnp.max(logits, axis=0, keepdims=True))
    aff_ref[0] = e / jnp.sum(e, axis=0, keepdims=True)


def _outproj(o, sg, w_out, x, g1, ng, sh, sc, wr_t):
    b, n, d = x.shape
    aw = o.shape[2]
    sw = sg.shape[2]
    ne = wr_t.shape[0]
    tm = _tile(n, 512)
    tok = lambda bi, i: (bi, i, 0)
    per_b = lambda bi, i: (bi, 0, 0)
    const2 = lambda bi, i: (0, 0)
    return pl.pallas_call(
        functools.partial(_out_kernel, aw=aw),
        grid=(b, n // tm),
        in_specs=[pl.BlockSpec((1, tm, aw), tok), pl.BlockSpec((1, tm, sw), tok),
                  pl.BlockSpec((aw + sw, d), const2), pl.BlockSpec((1, tm, d), tok),
                  pl.BlockSpec((1, 1, d), per_b), pl.BlockSpec((1, d), const2),
                  pl.BlockSpec((1, 1, d), per_b), pl.BlockSpec((1, 1, d), per_b),
                  pl.BlockSpec((ne, d), const2)],
        out_specs=[pl.BlockSpec((1, tm, d), tok), pl.BlockSpec((1, tm, d), tok),
                   pl.BlockSpec((1, ne, tm), lambda bi, i: (bi, 0, i))],
        out_shape=[jax.ShapeDtypeStruct((b, n, d), F32), jax.ShapeDtypeStruct((b, n, d), MXU_DTYPE),
                   jax.ShapeDtypeStruct((b, ne, n), F32)],
        compiler_params=_cparams("parallel", "parallel"), name=f"outproj_{n}",
    )(o, sg, w_out, x, g1, ng, sh, sc, wr_t)


def _route_kernel(aff_ref, tri_ref, pos_ref, start_ref, *, cap, tt):
    a = aff_ref[0]
    ne, n = a.shape
    bits = lax.bitcast_convert_type(a, jnp.int32)
    t = jnp.zeros((ne, 1), jnp.int32)
    for bit in range(30, -1, -1):
        cand = t | (1 << bit)
        cnt = jnp.sum((bits >= cand).astype(jnp.int32), axis=1, keepdims=True)
        t = jnp.where(cnt >= cap, cand, t)
    gt = bits > t
    eq = bits == t
    need = cap - jnp.sum(gt.astype(jnp.int32), axis=1, keepdims=True)

    def exclusive_count(mask):
        out = []
        runs = []
        run = jnp.zeros((ne, 1), F32)
        for ci in range(n // LANES):
            runs.append(run)
            m = mask[:, ci * LANES:(ci + 1) * LANES].astype(F32)
            incl = _dot(m, tri_ref[...])
            out.append(run + incl - m)
            run = run + incl[:, LANES - 1:LANES]
        return jnp.concatenate(out, axis=1), runs

    sel = gt | (eq & (exclusive_count(eq)[0] < need.astype(F32)))
    pos, runs = exclusive_count(sel)
    pos_ref[0] = jnp.where(sel, pos, -1.0)
    lane = lax.broadcasted_iota(jnp.int32, (ne, LANES), 1)
    first = jnp.full((ne, LANES), float(cap), F32)
    for j in range(n // tt):
        first = jnp.where(lane == j, runs[j * tt // LANES], first)
    start_ref[0] = first


def _route(aff_t, tri, *, cap, tt):
    b, ne, n = aff_t.shape
    return pl.pallas_call(
        functools.partial(_route_kernel, cap=cap, tt=tt),
        grid=(b,),
        in_specs=[pl.BlockSpec((1, ne, n), lambda bi: (bi, 0, 0)), pl.BlockSpec((LANES, LANES), lambda bi: (0, 0))],
        out_specs=[pl.BlockSpec((1, ne, n), lambda bi: (bi, 0, 0)), pl.BlockSpec((1, ne, LANES), lambda bi: (bi, 0, 0))],
        out_shape=[jax.ShapeDtypeStruct((b, ne, n), F32), jax.ShapeDtypeStruct((b, ne, LANES), F32)],
        compiler_params=_cparams("parallel"), name=f"route_{n}",
    )(aff_t, tri)


def _window_hits(a_ref, pos_ref, *, win):
    bi, t, nt = pl.program_id(0), pl.program_id(1), pl.num_programs(1)
    ne, tt = pos_ref.shape[1], pos_ref.shape[2]
    base = lax.broadcasted_iota(jnp.int32, (win, tt), 0)
    offs, hits = [], []
    for e in range(ne):
        a_e = a_ref[(bi * ne + e) * nt + t]
        offs.append(a_e)
        hits.append(pos_ref[0, e:e + 1, :] == (base + a_e).astype(F32))
    return offs, hits


def _gather_win_kernel(a_ref, pos_ref, aff_ref, hm_ref, xs_ref, gs_ref, *, cap, win):
    @pl.when(pl.program_id(1) == 0)
    def _():
        xs_ref[...] = jnp.zeros_like(xs_ref)
        gs_ref[...] = jnp.zeros_like(gs_ref)

    ne = pos_ref.shape[1]
    offs, hits = _window_hits(a_ref, pos_ref, win=win)
    group = 8
    for e0 in range(0, ne, group):
        onehot = jnp.concatenate([h.astype(MXU_DTYPE) for h in hits[e0:e0 + group]], axis=0)
        picked = _dot(onehot, hm_ref[0]).astype(xs_ref.dtype)
        for e in range(e0, min(e0 + group, ne)):
            rows = pl.ds(pl.multiple_of(e * cap + offs[e], 16), win)
            gate = jnp.sum(jnp.where(hits[e], aff_ref[0, e:e + 1, :], 0.0), axis=1, keepdims=True)
            xs_ref[0, rows, :] += picked[(e - e0) * win:(e - e0 + 1) * win]
            gs_ref[0, rows, :] += jnp.broadcast_to(gate, (win, LANES))


def _gather_win(a_flat, pos, aff_t, hm, *, cap, win, tt):
    b, ne, n = pos.shape
    d = hm.shape[2]
    tile = lambda bi, t, a: (bi, 0, t)
    whole = lambda bi, t, a: (bi, 0, 0)
    return pl.pallas_call(
        functools.partial(_gather_win_kernel, cap=cap, win=win),
        grid_spec=pltpu.PrefetchScalarGridSpec(
            num_scalar_prefetch=1, grid=(b, n // tt),
            in_specs=[pl.BlockSpec((1, ne, tt), tile), pl.BlockSpec((1, ne, tt), tile),
                      pl.BlockSpec((1, tt, d), lambda bi, t, a: (bi, t, 0))],
            out_specs=[pl.BlockSpec((1, ne * cap, d), whole), pl.BlockSpec((1, ne * cap, LANES), whole)]),
        out_shape=[jax.ShapeDtypeStruct((b, ne * cap, d), MXU_DTYPE),
                   jax.ShapeDtypeStruct((b, ne * cap, LANES), F32)],
        compiler_params=_cparams("parallel", "arbitrary"), name=f"gather_win_{n}",
    )(a_flat, pos, aff_t, hm)


def _gather_kernel(pos_ref, aff_ref, hm_ref, xs_ref, gs_ref, *, cap, tn):
    n = hm_ref.shape[1]
    d = hm_ref.shape[2]
    slot = lax.broadcasted_iota(jnp.int32, (cap, tn), 0).astype(F32)
    xs = jnp.zeros((cap, d), F32)
    gs = jnp.zeros((cap, 1), F32)
    for ci in range(n // tn):
        cols = slice(ci * tn, (ci + 1) * tn)
        hit = pos_ref[0, 0, :, cols] == slot
        xs = xs + _dot(hit.astype(MXU_DTYPE), hm_ref[0, cols, :])
        gs = gs + jnp.sum(jnp.where(hit, aff_ref[0, 0, :, cols], 0.0), axis=1, keepdims=True)
    xs_ref[0] = xs.astype(xs_ref.dtype)
    gs_ref[0] = jnp.broadcast_to(gs, (cap, LANES))


def _gather(pos, aff_t, hm, *, cap):
    b, ne, n = pos.shape
    d = hm.shape[2]
    tn = _tile(n, 512)
    pos4 = pos.reshape(b, ne, 1, n)
    aff4 = aff_t.reshape(b, ne, 1, n)
    return pl.pallas_call(
        functools.partial(_gather_kernel, cap=cap, tn=tn),
        grid=(b, ne),
        in_specs=[pl.BlockSpec((1, 1, 1, n), lambda bi, e: (bi, e, 0, 0)),
                  pl.BlockSpec((1, 1, 1, n), lambda bi, e: (bi, e, 0, 0)),
                  pl.BlockSpec((1, n, d), lambda bi, e: (bi, 0, 0))],
        out_specs=[pl.BlockSpec((1, cap, d), lambda bi, e: (bi, e, 0)),
                   pl.BlockSpec((1, cap, LANES), lambda bi, e: (bi, e, 0))],
        out_shape=[jax.ShapeDtypeStruct((b, ne * cap, d), MXU_DTYPE),
                   jax.ShapeDtypeStruct((b, ne * cap, LANES), F32)],
        compiler_params=_cparams("parallel", "parallel"), name=f"gather_{n}",
    )(pos4, aff4, hm)


def _ffn_kernel(x_ref, wg_ref, wu_ref, wd_ref, gs_ref, y_ref, acc_ref):
    f = pl.program_id(2)

    @pl.when(f == 0)
    def _():
        acc_ref[...] = jnp.zeros_like(acc_ref)

    tm, d = acc_ref.shape
    x = x_ref[...].reshape(tm, d)
    hid = _silu(_dot(x, wg_ref[0])) * _dot(x, wu_ref[0])
    acc_ref[...] += _dot(hid, wd_ref[0])

    @pl.when(f == pl.num_programs(2) - 1)
    def _():
        y_ref[0] = acc_ref[...] * gs_ref[...].reshape(tm, LANES)[:, :1]


def _ffn(xs, w_gate, w_up, w_down, gs, *, cap):
    b, rows, d = xs.shape
    ne, _, ff = w_gate.shape
    bt = b
    while bt * cap > 1024 and bt % 2 == 0:
        bt //= 2
    tm = bt * cap
    tf = _tile(ff, 512)
    return pl.pallas_call(
        _ffn_kernel,
        grid=(ne, b // bt, ff // tf),
        in_specs=[pl.BlockSpec((bt, 1, cap, d), lambda e, i, f: (i, e, 0, 0)),
                  pl.BlockSpec((1, d, tf), lambda e, i, f: (e, 0, f)),
                  pl.BlockSpec((1, d, tf), lambda e, i, f: (e, 0, f)),
                  pl.BlockSpec((1, tf, d), lambda e, i, f: (e, f, 0)),
                  pl.BlockSpec((bt, 1, cap, LANES), lambda e, i, f: (i, e, 0, 0))],
        out_specs=pl.BlockSpec((1, tm, d), lambda e, i, f: (e, i, 0)),
        out_shape=jax.ShapeDtypeStruct((ne, b * cap, d), F32),
        scratch_shapes=[pltpu.VMEM((tm, d), F32)],
        compiler_params=_cparams("parallel", "parallel", "arbitrary"), name=f"ffn_{b * cap}",
    )(xs.reshape(b, ne, cap, d), w_gate, w_up, w_down, gs.reshape(b, ne, cap, LANES))


def _combine_win_kernel(a_ref, pos_ref, *refs, win):
    y_refs, o_ref = refs[:-1], refs[-1]
    d = o_ref.shape[2]
    _, hits = _window_hits(a_ref, pos_ref, win=win)
    onehot = jnp.concatenate([h.astype(MXU_DTYPE) for h in hits], axis=0)
    parts = [_split_hi_lo(y_ref[0]) for y_ref in y_refs]
    y_all = jnp.concatenate([jnp.concatenate([p[0] for p in parts], axis=0),
                             jnp.concatenate([p[1] for p in parts], axis=0)], axis=1)
    r = _dot_tn(onehot, y_all)
    o_ref[0] = r[:, :d] + r[:, d:]


def _combine_win(a_flat, pos, yg, *, cap, win, tt):
    b, ne, n = pos.shape
    d = yg.shape[2]
    nt = n // tt

    def window(e):
        return pl.BlockSpec((pl.Element(1), pl.Element(win), pl.Element(d)),
                            lambda bi, t, a: (e, pl.multiple_of(bi * cap + a[(bi * ne + e) * nt + t], 16), 0))

    return pl.pallas_call(
        functools.partial(_combine_win_kernel, win=win),
        grid_spec=pltpu.PrefetchScalarGridSpec(
            num_scalar_prefetch=1, grid=(b, nt),
            in_specs=[pl.BlockSpec((1, ne, tt), lambda bi, t, a: (bi, 0, t))] + [window(e) for e in range(ne)],
            out_specs=pl.BlockSpec((1, tt, d), lambda bi, t, a: (bi, t, 0))),
        out_shape=jax.ShapeDtypeStruct((b, n, d), F32),
        compiler_params=_cparams("parallel", "parallel"), name=f"combine_win_{n}",
    )(a_flat, pos, *([yg] * ne))


def _combine_kernel(pos_ref, y_ref, o_ref, *, cap, tn):
    e = pl.program_id(1)

    @pl.when(e == 0)
    def _():
        o_ref[...] = jnp.zeros_like(o_ref)

    n = o_ref.shape[1]
    y_hi, y_lo = _split_hi_lo(y_ref[0])
    slot = lax.broadcasted_iota(jnp.int32, (cap, tn), 0).astype(F32)
    for ci in range(n // tn):
        cols = slice(ci * tn, (ci + 1) * tn)
        hit = (pos_ref[0, 0, :, cols] == slot).astype(MXU_DTYPE)
        o_ref[0, cols, :] += _dot_tn(hit, y_hi) + _dot_tn(hit, y_lo)


def _combine(pos, yg, *, cap):
    b, ne, n = pos.shape
    d = yg.shape[2]
    tn = _tile(n, 512)
    return pl.pallas_call(
        functools.partial(_combine_kernel, cap=cap, tn=tn),
        grid=(b, ne),
        in_specs=[pl.BlockSpec((1, 1, 1, n), lambda bi, e: (bi, e, 0, 0)),
                  pl.BlockSpec((1, cap, d), lambda bi, e: (e, bi, 0))],
        out_specs=pl.BlockSpec((1, n, d), lambda bi, e: (bi, 0, 0)),
        out_shape=jax.ShapeDtypeStruct((b, n, d), F32),
        compiler_params=_cparams("parallel", "arbitrary"), name=f"combine_{n}",
    )(pos.reshape(b, ne, 1, n), yg)


def _final_kernel(x_ref, moe_ref, g2_ref, ng_ref, o_ref):
    x = x_ref[0] + g2_ref[0] * moe_ref[0]
    o_ref[0] = x * lax.rsqrt(jnp.mean(x * x, axis=-1, keepdims=True) + EPS) * ng_ref[...]


def _final(x, moe, g2, ng):
    b, n, d = x.shape
    tm = _tile(n, 512)
    tok = lambda bi, i: (bi, i, 0)
    return pl.pallas_call(
        _final_kernel,
        grid=(b, n // tm),
        in_specs=[pl.BlockSpec((1, tm, d), tok), pl.BlockSpec((1, tm, d), tok),
                  pl.BlockSpec((1, 1, d), lambda bi, i: (bi, 0, 0)), pl.BlockSpec((1, d), lambda bi, i: (0, 0))],
        out_specs=pl.BlockSpec((1, tm, d), tok),
        out_shape=jax.ShapeDtypeStruct((b, n, d), F32),
        compiler_params=_cparams("parallel", "parallel"), name="final",
    )(x, moe, g2, ng)


def _rope_tables(n, heads):
    rows = n // GRID_W
    row = jnp.repeat(jnp.arange(rows), GRID_W).astype(F32)
    col = jnp.tile(jnp.arange(GRID_W), rows).astype(F32)
    n_freq = HEAD_DIM // 4
    freqs = ROPE_THETA ** (-jnp.arange(n_freq, dtype=F32) / n_freq)
    ang_r = row[:, None] * freqs
    ang_c = col[:, None] * freqs
    ang = jnp.concatenate([ang_r, ang_r, ang_c, ang_c], axis=-1)
    sign = jnp.where((jnp.arange(HEAD_DIM) % 32) < 16, -1.0, 1.0).astype(F32)
    return jnp.tile(jnp.cos(ang), (1, 2 * heads)), jnp.tile(jnp.sin(ang) * sign, (1, 2 * heads))


def _moe(hm, aff_t, tri, w_gate, w_up, w_down):
    b, n, _ = hm.shape
    ne = aff_t.shape[1]
    cap = CAPACITY_FACTOR * n // ne
    tt = _tile(n, 512)
    nt = n // tt
    win = min(SLOT_WINDOW, cap)
    pos, first = _route(aff_t, tri, cap=cap, tt=tt)
    first = first[:, :, :nt + 1].astype(jnp.int32)
    offs = jnp.minimum(first[:, :, :nt] // 16 * 16, cap - win)
    fits = jnp.all(first[:, :, 1:] <= offs + win)
    a_flat = offs.reshape(-1)
    xs, gs = lax.cond(fits,
                      lambda: _gather_win(a_flat, pos, aff_t, hm, cap=cap, win=win, tt=tt),
                      lambda: _gather(pos, aff_t, hm, cap=cap))
    yg = _ffn(xs, w_gate, w_up, w_down, gs, cap=cap)
    return lax.cond(fits,
                    lambda: _combine_win(a_flat, pos, yg, cap=cap, win=win, tt=tt),
                    lambda: _combine(pos, yg, cap=cap))


def kernel(x, c, ctx, c_ctx, w_ada, b_ada, norm1_g, norm2_g, w_in, w_out, lambda_q1, lambda_k1, lambda_q2, lambda_k2, subln_g, sgu_norm_g, sgu_w, sgu_b, w_router, w_gate, w_up, w_down, norm_f_g):
    b, n, d = x.shape
    m = ctx.shape[1]
    depth = w_in.shape[0]
    sw = sgu_norm_g.shape[1]
    aw = (w_in.shape[2] - 2 * sw) // 3
    heads = aw // (2 * HEAD_DIM)
    bf = lambda t: t.astype(MXU_DTYPE)

    rows = -(-(b + 1) // 8) * 8
    cc = jnp.concatenate([c, c_ctx[None, :], jnp.zeros((rows - b - 1, d), F32)], axis=0)
    mod = _ada(cc, w_ada, b_ada)

    w_in_all = bf(w_in)
    cos, sin = _rope_tables(n, heads)
    cos_c = jnp.ones((m, aw), F32)
    sin_c = jnp.zeros((m, aw), F32)
    hw_s = sw // SGU_HEADS
    ones_blk = bf(jnp.kron(jnp.eye(SGU_HEADS, dtype=F32), jnp.full((hw_s, hw_s), 1.0 / hw_s, F32)))
    tri = bf(jnp.triu(jnp.ones((LANES, LANES), F32)))

    cx = ctx
    res_x = res_c = None
    for l in range(depth):
        last = l == depth - 1
        lam_init = 0.8 - 0.6 * math.exp(-0.3 * l)
        parts = [mod[l, :, i * d:(i + 1) * d] for i in range(6)]
        sh1, sc1, g1, sh2, sc2, g2 = [t[:b, None, :] for t in parts]
        csh1, csc1, cg1, csh2, csc2, cg2 = [jnp.broadcast_to(t[b][None, None, :], (b, 1, d)) for t in parts]

        w_in_p = w_in_all[l]
        w_out_l = bf(w_out[l])
        wr_t = bf(w_router[l].T)
        wg, wu, wd = bf(w_gate[l]), bf(w_up[l]), bf(w_down[l])
        ng1, ng2 = norm1_g[l][None, :], norm2_g[l][None, :]
        sng = sgu_norm_g[l][None, :]
        ws = bf(sgu_w[l].reshape(SGU_HEADS * CHUNK, CHUNK))
        bs = jnp.repeat(sgu_b[l].T, hw_s, axis=1)
        lam_params = jnp.zeros((8, LANES), F32).at[:4, :HEAD_DIM].set(
            jnp.stack([lambda_q1[l], lambda_k1[l], lambda_q2[l], lambda_k2[l]]))
        sub_g = subln_g[l][None, :]
        shared = dict(aw=aw, sw=sw)

        x, (q, k, v, sg) = _proj(x, res_x, ng1, sh1, sc1, w_in_p, cos, sin, sng, ws, bs, ones_blk, rope=True, **shared)
        cx, (qc, kc, vc, sgc) = _proj(cx, res_c, ng1, csh1, csc1, w_in_p, cos_c, sin_c, sng, ws, bs, ones_blk,
                                      rope=False, **shared)
        o = _attn(lam_params, q, jnp.concatenate([kc, k], axis=1), jnp.concatenate([vc, v], axis=1), sub_g,
                  lam_init=lam_init)
        x, hm, aff_t = _outproj(o, sg, w_out_l, x, g1, ng2, sh2, sc2, wr_t)
        res_x = (_moe(hm, aff_t, tri, wg, wu, wd), g2)
        if not last:
            oc = _attn(lam_params, qc, kc, vc, sub_g, lam_init=lam_init)
            cx, hmc, aff_c = _outproj(oc, sgc, w_out_l, cx, cg1, ng2, csh2, csc2, wr_t)
            res_c = (_moe(hmc, aff_c, tri, wg, wu, wd), cg2)
    return _final(x, res_x[0], res_x[1], norm_f_g[None, :])
```

```python
import functools
import math

import jax
import jax.numpy as jnp
from jax import lax
from jax.experimental import pallas as pl
from jax.experimental.pallas import tpu as pltpu

F32 = jnp.float32
MXU_DTYPE = jnp.bfloat16

EPS = 1e-6
GRID_W = 64
ROPE_THETA = 10000.0
HEAD_DIM = 64
SGU_HEADS = 8
CHUNK = 128
N_EXPERTS = 16
CAPACITY_FACTOR = 2
LANES = 128
SLOT_WINDOW = 128
VMEM_LIMIT = 56 * 1024 * 1024


def _cparams(*sem):
    return pltpu.CompilerParams(dimension_semantics=sem, vmem_limit_bytes=VMEM_LIMIT)


def _dot(a, b):
    return jnp.dot(a.astype(MXU_DTYPE), b.astype(MXU_DTYPE), preferred_element_type=F32)


def _dot_nt(a, b):
    return lax.dot_general(a.astype(MXU_DTYPE), b.astype(MXU_DTYPE), (((1,), (1,)), ((), ())),
                           preferred_element_type=F32)


def _dot_tn(a, b):
    return lax.dot_general(a.astype(MXU_DTYPE), b.astype(MXU_DTYPE), (((0,), (0,)), ((), ())),
                           preferred_element_type=F32)


def _split_hi_lo(x):
    hi = x.astype(MXU_DTYPE)
    lo = (x - hi.astype(F32)).astype(MXU_DTYPE)
    return hi, lo


def _silu(x):
    return x * jax.nn.sigmoid(x)


def _gelu(x):
    return 0.5 * x * (1.0 + lax.erf(x * (2.0 ** -0.5)))


def _tile(n, pref):
    t = pref
    while t > 8 and n % t:
        t //= 2
    return t if n % t == 0 else n


def _ada_kernel(c_ref, w_ref, b_ref, o_ref):
    o_ref[0] = _dot(_silu(c_ref[...]), w_ref[0]) + b_ref[0]


def _ada(cc, w_ada, b_ada):
    depth, d, n6 = w_ada.shape
    rows = cc.shape[0]
    tn = _tile(n6, 1536)
    return pl.pallas_call(
        _ada_kernel,
        grid=(depth, n6 // tn),
        in_specs=[pl.BlockSpec((rows, d), lambda l, j: (0, 0)),
                  pl.BlockSpec((1, d, tn), lambda l, j: (l, 0, j)),
                  pl.BlockSpec((1, 1, tn), lambda l, j: (l, 0, j))],
        out_specs=pl.BlockSpec((1, rows, tn), lambda l, j: (l, 0, j)),
        out_shape=jax.ShapeDtypeStruct((depth, rows, n6), F32),
        compiler_params=_cparams("parallel", "parallel"), name="ada",
    )(cc, w_ada, b_ada.reshape(depth, 1, n6))


def _proj_kernel(*refs, has_res, rope, aw, sw):
    if has_res:
        x_ref, moe_ref, g2_ref, *refs = refs
    else:
        x_ref, *refs = refs
    (ng_ref, sh_ref, sc_ref, w_ref, cos_ref, sin_ref, sng_ref, ws_ref, bs_ref, ones_ref, *outs) = refs
    if has_res:
        xo_ref, q_ref, k_ref, v_ref, sg_ref = outs
    else:
        q_ref, k_ref, v_ref, sg_ref = outs

    x = x_ref[0]
    if has_res:
        x = x + g2_ref[0] * moe_ref[0]
        xo_ref[0] = x
    h = x * lax.rsqrt(jnp.mean(x * x, axis=-1, keepdims=True) + EPS) * ng_ref[...]
    h = h * (1.0 + sc_ref[0]) + sh_ref[0]
    p = _dot(h, w_ref[...])

    q = p[:, :aw]
    k = p[:, aw:2 * aw]
    if rope:
        first = (lax.broadcasted_iota(jnp.int32, q.shape, 1) % 32) < 16
        cos = cos_ref[...]
        sin = sin_ref[...]

        def rot(t):
            return jnp.where(first, pltpu.roll(t, aw - 16, 1), pltpu.roll(t, 16, 1))

        q = q * cos + rot(q) * sin
        k = k * cos + rot(k) * sin
    q_ref[0] = (q * (HEAD_DIM ** -0.5 * math.log2(math.e))).astype(q_ref.dtype)
    k_ref[0] = k.astype(k_ref.dtype)
    v_ref[0] = p[:, 2 * aw:3 * aw].astype(v_ref.dtype)

    u = _gelu(p[:, 3 * aw:3 * aw + sw])
    gv = _gelu(p[:, 3 * aw + sw:])
    sq_hi, sq_lo = _split_hi_lo(gv * gv)
    ms = _dot(sq_hi, ones_ref[...]) + _dot(sq_lo, ones_ref[...])
    gvn = (gv * lax.rsqrt(ms + EPS) * sng_ref[...]).astype(MXU_DTYPE)
    head_of_lane = lax.broadcasted_iota(jnp.int32, (CHUNK, sw), 1) // (sw // SGU_HEADS)
    tm = x.shape[0]
    for ci in range(tm // CHUNK):
        rows = slice(ci * CHUNK, (ci + 1) * CHUNK)
        r = _dot(ws_ref[...], gvn[rows])
        s = bs_ref[...]
        for hh in range(SGU_HEADS):
            s = s + jnp.where(head_of_lane == hh, r[hh * CHUNK:(hh + 1) * CHUNK], 0.0)
        sg_ref[0, rows, :] = (u[rows] * s).astype(sg_ref.dtype)


def _proj(x, res, ng, sh, sc, w_in_p, cos, sin, sng, ws, bs, ones_blk, *, rope, aw, sw):
    b, n, d = x.shape
    pw = w_in_p.shape[1]
    tm = _tile(n, 512)
    has_res = res is not None
    tok = lambda bi, i: (bi, i, 0)
    per_b = lambda bi, i: (bi, 0, 0)
    const2 = lambda bi, i: (0, 0)
    in_specs = [pl.BlockSpec((1, tm, d), tok)]
    args = [x]
    if has_res:
        moe, g2 = res
        in_specs += [pl.BlockSpec((1, tm, d), tok), pl.BlockSpec((1, 1, d), per_b)]
        args += [moe, g2]
    in_specs += [pl.BlockSpec((1, d), const2), pl.BlockSpec((1, 1, d), per_b), pl.BlockSpec((1, 1, d), per_b),
                 pl.BlockSpec((d, pw), const2),
                 pl.BlockSpec((tm, aw), lambda bi, i: (i, 0)), pl.BlockSpec((tm, aw), lambda bi, i: (i, 0)),
                 pl.BlockSpec((1, sw), const2), pl.BlockSpec((SGU_HEADS * CHUNK, CHUNK), const2),
                 pl.BlockSpec((CHUNK, sw), const2), pl.BlockSpec((sw, sw), const2)]
    args += [ng, sh, sc, w_in_p, cos, sin, sng, ws, bs, ones_blk]
    out_specs = [pl.BlockSpec((1, tm, aw), tok)] * 3 + [pl.BlockSpec((1, tm, sw), tok)]
    out_shape = [jax.ShapeDtypeStruct((b, n, aw), MXU_DTYPE)] * 3 + [jax.ShapeDtypeStruct((b, n, sw), MXU_DTYPE)]
    if has_res:
        out_specs = [pl.BlockSpec((1, tm, d), tok)] + out_specs
        out_shape = [jax.ShapeDtypeStruct((b, n, d), F32)] + out_shape
    outs = pl.pallas_call(
        functools.partial(_proj_kernel, has_res=has_res, rope=rope, aw=aw, sw=sw),
        grid=(b, n // tm),
        in_specs=in_specs, out_specs=out_specs, out_shape=out_shape,
        compiler_params=_cparams("parallel", "parallel"), name="proj_lat" if rope else "proj_ctx",
    )(*args)
    if has_res:
        return outs[0], outs[1:]
    return x, outs


def _attn_kernel(lam_ref, q0_ref, q1_ref, k0_ref, k1_ref, v_ref, g_ref, o_ref,
                 s0_ref, s1_ref, m0_ref, m1_ref, *, tq, tkc, lam_init):
    q_ref, k_ref = (q0_ref, q1_ref), (k0_ref, k1_ref)
    s_ref, m_ref = (s0_ref, s1_ref), (m0_ref, m1_ref)
    nkeys, hw = v_ref.shape[1], v_ref.shape[2]
    nt = q0_ref.shape[1] // tq
    lane = lax.broadcasted_iota(jnp.int32, (tq, hw), 1)
    mine = (lane >= HEAD_DIM) == (pl.program_id(1) % 2 == 1)
    lp = lam_ref[...]
    lam = (jnp.exp(jnp.sum(lp[0:1] * lp[1:2], axis=-1, keepdims=True))
           - jnp.exp(jnp.sum(lp[2:3] * lp[3:4], axis=-1, keepdims=True)) + lam_init)

    def rows(t):
        return pl.ds(t * tq if isinstance(t, int) else pl.multiple_of(t * tq, tq), tq)

    def phase(t, slot, *, do_scores=True, do_values=True):
        other = 1 - slot
        if do_scores:
            qc = []
            for c in range(2):
                q = q_ref[c][0, rows(t + 1), :]
                qc.append(jnp.where(mine, q, jnp.zeros_like(q)))
            run = [None, None]
        if do_values:
            top = [jnp.max(m_ref[slot][c], axis=-1, keepdims=True) for c in range(2)]
            acc = [jnp.zeros((tq, 2 * hw), F32), jnp.zeros((tq, 2 * hw), F32)]
        for kc in range(nkeys // tkc):
            cols = slice(kc * tkc, (kc + 1) * tkc)
            if do_values:
                vals = v_ref[0, cols, :]
                vals1 = jnp.concatenate([vals, jnp.ones_like(vals)], axis=1)
            for c in range(2):
                if do_scores:
                    s = _dot_nt(qc[c], k_ref[c][0, cols, :])
                    s_ref[other][c, :, cols] = s
                    for j in range(tkc // LANES):
                        part = s[:, j * LANES:(j + 1) * LANES]
                        run[c] = part if run[c] is None else jnp.maximum(run[c], part)
                if do_values:
                    p = jnp.exp2(s_ref[slot][c, :, cols] - top[c])
                    acc[c] = acc[c] + _dot(p, vals1)
        if do_scores:
            for c in range(2):
                m_ref[other][c] = run[c]
        if do_values:
            o = acc[0][:, :hw] / acc[0][:, hw:] - lam * (acc[1][:, :hw] / acc[1][:, hw:])
            o = o * lax.rsqrt(jnp.mean(o * o, axis=-1, keepdims=True) + EPS) * g_ref[...] * (1.0 - lam_init)
            o_ref[0, rows(t), :] = o.astype(o_ref.dtype)

    phase(-1, 1, do_values=False)

    def step(t, carry):
        for slot in range(2):
            @pl.when(t % 2 == slot)
            def _():
                phase(t, slot)
        return carry

    lax.fori_loop(0, nt - 1, step, 0)
    phase(nt - 1, (nt - 1) % 2, do_scores=False)


def _attn(lam_params, q, k, v, subln_g, *, lam_init):
    b, n, aw = q.shape
    nkeys = k.shape[1]
    hw = 2 * HEAD_DIM
    heads = aw // hw
    pairs = heads // 2
    tq = _tile(n, 256)
    tkc = _tile(nkeys, 256)
    comp0 = lambda bi, h: (bi, 0, h // 2)
    comp1 = lambda bi, h: (bi, 0, pairs + h // 2)
    return pl.pallas_call(
        functools.partial(_attn_kernel, tq=tq, tkc=tkc, lam_init=lam_init),
        grid=(b, heads),
        in_specs=[pl.BlockSpec(lam_params.shape, lambda bi, h: (0, 0)),
                  pl.BlockSpec((1, n, hw), comp0), pl.BlockSpec((1, n, hw), comp1),
                  pl.BlockSpec((1, nkeys, hw), comp0), pl.BlockSpec((1, nkeys, hw), comp1),
                  pl.BlockSpec((1, nkeys, hw), lambda bi, h: (bi, 0, h)),
                  pl.BlockSpec((1, hw), lambda bi, h: (0, 0))],
        out_specs=pl.BlockSpec((1, n, hw), lambda bi, h: (bi, 0, h)),
        out_shape=jax.ShapeDtypeStruct((b, n, aw), MXU_DTYPE),
        scratch_shapes=[pltpu.VMEM((2, tq, nkeys), F32)] * 2 + [pltpu.VMEM((2, tq, LANES), F32)] * 2,
        compiler_params=_cparams("parallel", "parallel"), name=f"attn_{n}",
    )(lam_params, q, q, k, k, v, subln_g)


def _out_kernel(o_ref, sg_ref, w_ref, x_ref, g1_ref, ng_ref, sh_ref, sc_ref, wr_ref, xo_ref, hm_ref, aff_ref, *, aw):
    y = _dot(o_ref[0], w_ref[:aw, :]) + _dot(sg_ref[0], w_ref[aw:, :])
    x = x_ref[0] + g1_ref[0] * y
    xo_ref[0] = x
    h = x * lax.rsqrt(jnp.mean(x * x, axis=-1, keepdims=True) + EPS) * ng_ref[...]
    h = h * (1.0 + sc_ref[0]) + sh_ref[0]
    hm_ref[0] = h.astype(hm_ref.dtype)
    logits = _dot_nt(wr_ref[...], h)
    e = jnp.exp(logits - jnp.max(logits, axis=0, keepdims=True))
    aff_ref[0] = e / jnp.sum(e, axis=0, keepdims=True)


def _outproj(o, sg, w_out, x, g1, ng, sh, sc, wr_t):
    b, n, d = x.shape
    aw = o.shape[2]
    sw = sg.shape[2]
    ne = wr_t.shape[0]
    tm = _tile(n, 512)
    tok = lambda bi, i: (bi, i, 0)
    per_b = lambda bi, i: (bi, 0, 0)
    const2 = lambda bi, i: (0, 0)
    return pl.pallas_call(
        functools.partial(_out_kernel, aw=aw),
        grid=(b, n // tm),
        in_specs=[pl.BlockSpec((1, tm, aw), tok), pl.BlockSpec((1, tm, sw), tok),
                  pl.BlockSpec((aw + sw, d), const2), pl.BlockSpec((1, tm, d), tok),
                  pl.BlockSpec((1, 1, d), per_b), pl.BlockSpec((1, d), const2),
                  pl.BlockSpec((1, 1, d), per_b), pl.BlockSpec((1, 1, d), per_b),
                  pl.BlockSpec((ne, d), const2)],
        out_specs=[pl.BlockSpec((1, tm, d), tok), pl.BlockSpec((1, tm, d), tok),
                   pl.BlockSpec((1, ne, tm), lambda bi, i: (bi, 0, i))],
        out_shape=[jax.ShapeDtypeStruct((b, n, d), F32), jax.ShapeDtypeStruct((b, n, d), MXU_DTYPE),
                   jax.ShapeDtypeStruct((b, ne, n), F32)],
        compiler_params=_cparams("parallel", "parallel"), name=f"outproj_{n}",
    )(o, sg, w_out, x, g1, ng, sh, sc, wr_t)


def _route_kernel(aff_ref, tri_ref, pos_ref, start_ref, *, cap, tt):
    a = aff_ref[0]
    ne, n = a.shape
    bits = lax.bitcast_convert_type(a, jnp.int32)
    t = jnp.zeros((ne, 1), jnp.int32)
    for bit in range(30, -1, -1):
        cand = t | (1 << bit)
        cnt = jnp.sum((bits >= cand).astype(jnp.int32), axis=1, keepdims=True)
        t = jnp.where(cnt >= cap, cand, t)
    gt = bits > t
    eq = bits == t
    need = cap - jnp.sum(gt.astype(jnp.int32), axis=1, keepdims=True)

    def exclusive_count(mask):
        out = []
        runs = []
        run = jnp.zeros((ne, 1), F32)
        for ci in range(n // LANES):
            runs.append(run)
            m = mask[:, ci * LANES:(ci + 1) * LANES].astype(F32)
            incl = _dot(m, tri_ref[...])
            out.append(run + incl - m)
            run = run + incl[:, LANES - 1:LANES]
        return jnp.concatenate(out, axis=1), runs

    sel = gt | (eq & (exclusive_count(eq)[0] < need.astype(F32)))
    pos, runs = exclusive_count(sel)
    pos_ref[0] = jnp.where(sel, pos, -1.0)
    lane = lax.broadcasted_iota(jnp.int32, (ne, LANES), 1)
    first = jnp.full((ne, LANES), float(cap), F32)
    for j in range(n // tt):
        first = jnp.where(lane == j, runs[j * tt // LANES], first)
    start_ref[0] = first


def _route(aff_t, tri, *, cap, tt):
    b, ne, n = aff_t.shape
    return pl.pallas_call(
        functools.partial(_route_kernel, cap=cap, tt=tt),
        grid=(b,),
        in_specs=[pl.BlockSpec((1, ne, n), lambda bi: (bi, 0, 0)), pl.BlockSpec((LANES, LANES), lambda bi: (0, 0))],
        out_specs=[pl.BlockSpec((1, ne, n), lambda bi: (bi, 0, 0)), pl.BlockSpec((1, ne, LANES), lambda bi: (bi, 0, 0))],
        out_shape=[jax.ShapeDtypeStruct((b, ne, n), F32), jax.ShapeDtypeStruct((b, ne, LANES), F32)],
        compiler_params=_cparams("parallel"), name=f"route_{n}",
    )(aff_t, tri)


def _window_hits(a_ref, pos_ref, *, win):
    bi, t, nt = pl.program_id(0), pl.program_id(1), pl.num_programs(1)
    ne, tt = pos_ref.shape[1], pos_ref.shape[2]
    base = lax.broadcasted_iota(jnp.int32, (win, tt), 0)
    offs, hits = [], []
    for e in range(ne):
        a_e = a_ref[(bi * ne + e) * nt + t]
        offs.append(a_e)
        hits.append(pos_ref[0, e:e + 1, :] == (base + a_e).astype(F32))
    return offs, hits


def _gather_win_kernel(a_ref, pos_ref, aff_ref, hm_ref, xs_ref, gs_ref, *, cap, win):
    @pl.when(pl.program_id(1) == 0)
    def _():
        xs_ref[...] = jnp.zeros_like(xs_ref)
        gs_ref[...] = jnp.zeros_like(gs_ref)

    ne = pos_ref.shape[1]
    offs, hits = _window_hits(a_ref, pos_ref, win=win)
    group = 8
    for e0 in range(0, ne, group):
        onehot = jnp.concatenate([h.astype(MXU_DTYPE) for h in hits[e0:e0 + group]], axis=0)
        picked = _dot(onehot, hm_ref[0]).astype(xs_ref.dtype)
        for e in range(e0, min(e0 + group, ne)):
            rows = pl.ds(pl.multiple_of(e * cap + offs[e], 16), win)
            gate = jnp.sum(jnp.where(hits[e], aff_ref[0, e:e + 1, :], 0.0), axis=1, keepdims=True)
            xs_ref[0, rows, :] += picked[(e - e0) * win:(e - e0 + 1) * win]
            gs_ref[0, rows, :] += jnp.broadcast_to(gate, (win, LANES))


def _gather_win(a_flat, pos, aff_t, hm, *, cap, win, tt):
    b, ne, n = pos.shape
    d = hm.shape[2]
    tile = lambda bi, t, a: (bi, 0, t)
    whole = lambda bi, t, a: (bi, 0, 0)
    return pl.pallas_call(
        functools.partial(_gather_win_kernel, cap=cap, win=win),
        grid_spec=pltpu.PrefetchScalarGridSpec(
            num_scalar_prefetch=1, grid=(b, n // tt),
            in_specs=[pl.BlockSpec((1, ne, tt), tile), pl.BlockSpec((1, ne, tt), tile),
                      pl.BlockSpec((1, tt, d), lambda bi, t, a: (bi, t, 0))],
            out_specs=[pl.BlockSpec((1, ne * cap, d), whole), pl.BlockSpec((1, ne * cap, LANES), whole)]),
        out_shape=[jax.ShapeDtypeStruct((b, ne * cap, d), MXU_DTYPE),
                   jax.ShapeDtypeStruct((b, ne * cap, LANES), F32)],
        compiler_params=_cparams("parallel", "arbitrary"), name=f"gather_win_{n}",
    )(a_flat, pos, aff_t, hm)


def _gather_kernel(pos_ref, aff_ref, hm_ref, xs_ref, gs_ref, *, cap, tn):
    n = hm_ref.shape[1]
    d = hm_ref.shape[2]
    slot = lax.broadcasted_iota(jnp.int32, (cap, tn), 0).astype(F32)
    xs = jnp.zeros((cap, d), F32)
    gs = jnp.zeros((cap, 1), F32)
    for ci in range(n // tn):
        cols = slice(ci * tn, (ci + 1) * tn)
        hit = pos_ref[0, 0, :, cols] == slot
        xs = xs + _dot(hit.astype(MXU_DTYPE), hm_ref[0, cols, :])
        gs = gs + jnp.sum(jnp.where(hit, aff_ref[0, 0, :, cols], 0.0), axis=1, keepdims=True)
    xs_ref[0] = xs.astype(xs_ref.dtype)
    gs_ref[0] = jnp.broadcast_to(gs, (cap, LANES))


def _gather(pos, aff_t, hm, *, cap):
    b, ne, n = pos.shape
    d = hm.shape[2]
    tn = _tile(n, 512)
    pos4 = pos.reshape(b, ne, 1, n)
    aff4 = aff_t.reshape(b, ne, 1, n)
    return pl.pallas_call(
        functools.partial(_gather_kernel, cap=cap, tn=tn),
        grid=(b, ne),
        in_specs=[pl.BlockSpec((1, 1, 1, n), lambda bi, e: (bi, e, 0, 0)),
                  pl.BlockSpec((1, 1, 1, n), lambda bi, e: (bi, e, 0, 0)),
                  pl.BlockSpec((1, n, d), lambda bi, e: (bi, 0, 0))],
        out_specs=[pl.BlockSpec((1, cap, d), lambda bi, e: (bi, e, 0)),
                   pl.BlockSpec((1, cap, LANES), lambda bi, e: (bi, e, 0))],
        out_shape=[jax.ShapeDtypeStruct((b, ne * cap, d), MXU_DTYPE),
                   jax.ShapeDtypeStruct((b, ne * cap, LANES), F32)],
        compiler_params=_cparams("parallel", "parallel"), name=f"gather_{n}",
    )(pos4, aff4, hm)


def _ffn_kernel(x_ref, wg_ref, wu_ref, wd_ref, gs_ref, y_ref, acc_ref):
    f = pl.program_id(2)

    @pl.when(f == 0)
    def _():
        acc_ref[...] = jnp.zeros_like(acc_ref)

    tm, d = acc_ref.shape
    x = x_ref[...].reshape(tm, d)
    hid = _silu(_dot(x, wg_ref[0, 0])) * _dot(x, wu_ref[0, 0])
    acc_ref[...] += _dot(hid, wd_ref[0, 0])

    @pl.when(f == pl.num_programs(2) - 1)
    def _():
        y_ref[0] = acc_ref[...] * gs_ref[...].reshape(tm, LANES)[:, :1]


def _ffn(xs, w_gate, w_up, w_down, gs, *, layer, cap):
    b, rows, d = xs.shape
    _, ne, _, ff = w_gate.shape
    bt = b
    while bt * cap > 1024 and bt % 2 == 0:
        bt //= 2
    tm = bt * cap
    tf = _tile(ff, 512)
    return pl.pallas_call(
        _ffn_kernel,
        grid=(ne, b // bt, ff // tf),
        in_specs=[pl.BlockSpec((bt, 1, cap, d), lambda e, i, f: (i, e, 0, 0)),
                  pl.BlockSpec((1, 1, d, tf), lambda e, i, f: (layer, e, 0, f)),
                  pl.BlockSpec((1, 1, d, tf), lambda e, i, f: (layer, e, 0, f)),
                  pl.BlockSpec((1, 1, tf, d), lambda e, i, f: (layer, e, f, 0)),
                  pl.BlockSpec((bt, 1, cap, LANES), lambda e, i, f: (i, e, 0, 0))],
        out_specs=pl.BlockSpec((1, tm, d), lambda e, i, f: (e, i, 0)),
        out_shape=jax.ShapeDtypeStruct((ne, b * cap, d), F32),
        scratch_shapes=[pltpu.VMEM((tm, d), F32)],
        compiler_params=_cparams("parallel", "parallel", "arbitrary"), name=f"ffn_{b * cap}",
    )(xs.reshape(b, ne, cap, d), w_gate, w_up, w_down, gs.reshape(b, ne, cap, LANES))


def _combine_win_kernel(a_ref, pos_ref, *refs, win):
    y_refs, o_ref = refs[:-1], refs[-1]
    d = o_ref.shape[2]
    _, hits = _window_hits(a_ref, pos_ref, win=win)
    onehot = jnp.concatenate([h.astype(MXU_DTYPE) for h in hits], axis=0)
    parts = [_split_hi_lo(y_ref[0]) for y_ref in y_refs]
    y_all = jnp.concatenate([jnp.concatenate([p[0] for p in parts], axis=0),
                             jnp.concatenate([p[1] for p in parts], axis=0)], axis=1)
    r = _dot_tn(onehot, y_all)
    o_ref[0] = r[:, :d] + r[:, d:]


def _combine_win(a_flat, pos, yg, *, cap, win, tt):
    b, ne, n = pos.shape
    d = yg.shape[2]
    nt = n // tt

    def window(e):
        return pl.BlockSpec((pl.Element(1), pl.Element(win), pl.Element(d)),
                            lambda bi, t, a: (e, pl.multiple_of(bi * cap + a[(bi * ne + e) * nt + t], 16), 0))

    return pl.pallas_call(
        functools.partial(_combine_win_kernel, win=win),
        grid_spec=pltpu.PrefetchScalarGridSpec(
            num_scalar_prefetch=1, grid=(b, nt),
            in_specs=[pl.BlockSpec((1, ne, tt), lambda bi, t, a: (bi, 0, t))] + [window(e) for e in range(ne)],
            out_specs=pl.BlockSpec((1, tt, d), lambda bi, t, a: (bi, t, 0))),
        out_shape=jax.ShapeDtypeStruct((b, n, d), F32),
        compiler_params=_cparams("parallel", "parallel"), name=f"combine_win_{n}",
    )(a_flat, pos, *([yg] * ne))


def _combine_kernel(pos_ref, y_ref, o_ref, *, cap, tn):
    e = pl.program_id(1)

    @pl.when(e == 0)
    def _():
        o_ref[...] = jnp.zeros_like(o_ref)

    n = o_ref.shape[1]
    y_hi, y_lo = _split_hi_lo(y_ref[0])
    slot = lax.broadcasted_iota(jnp.int32, (cap, tn), 0).astype(F32)
    for ci in range(n // tn):
        cols = slice(ci * tn, (ci + 1) * tn)
        hit = (pos_ref[0, 0, :, cols] == slot).astype(MXU_DTYPE)
        o_ref[0, cols, :] += _dot_tn(hit, y_hi) + _dot_tn(hit, y_lo)


def _combine(pos, yg, *, cap):
    b, ne, n = pos.shape
    d = yg.shape[2]
    tn = _tile(n, 512)
    return pl.pallas_call(
        functools.partial(_combine_kernel, cap=cap, tn=tn),
        grid=(b, ne),
        in_specs=[pl.BlockSpec((1, 1, 1, n), lambda bi, e: (bi, e, 0, 0)),
                  pl.BlockSpec((1, cap, d), lambda bi, e: (e, bi, 0))],
        out_specs=pl.BlockSpec((1, n, d), lambda bi, e: (bi, 0, 0)),
        out_shape=jax.ShapeDtypeStruct((b, n, d), F32),
        compiler_params=_cparams("parallel", "arbitrary"), name=f"combine_{n}",
    )(pos.reshape(b, ne, 1, n), yg)


def _final_kernel(x_ref, moe_ref, g2_ref, ng_ref, o_ref):
    x = x_ref[0] + g2_ref[0] * moe_ref[0]
    o_ref[0] = x * lax.rsqrt(jnp.mean(x * x, axis=-1, keepdims=True) + EPS) * ng_ref[...]


def _final(x, moe, g2, ng):
    b, n, d = x.shape
    tm = _tile(n, 512)
    tok = lambda bi, i: (bi, i, 0)
    return pl.pallas_call(
        _final_kernel,
        grid=(b, n // tm),
        in_specs=[pl.BlockSpec((1, tm, d), tok), pl.BlockSpec((1, tm, d), tok),
                  pl.BlockSpec((1, 1, d), lambda bi, i: (bi, 0, 0)), pl.BlockSpec((1, d), lambda bi, i: (0, 0))],
        out_specs=pl.BlockSpec((1, tm, d), tok),
        out_shape=jax.ShapeDtypeStruct((b, n, d), F32),
        compiler_params=_cparams("parallel", "parallel"), name="final",
    )(x, moe, g2, ng)


def _rope_tables(n, heads):
    rows = n // GRID_W
    row = jnp.repeat(jnp.arange(rows), GRID_W).astype(F32)
    col = jnp.tile(jnp.arange(GRID_W), rows).astype(F32)
    n_freq = HEAD_DIM // 4
    freqs = ROPE_THETA ** (-jnp.arange(n_freq, dtype=F32) / n_freq)
    ang_r = row[:, None] * freqs
    ang_c = col[:, None] * freqs
    ang = jnp.concatenate([ang_r, ang_r, ang_c, ang_c], axis=-1)
    sign = jnp.where((jnp.arange(HEAD_DIM) % 32) < 16, -1.0, 1.0).astype(F32)
    return jnp.tile(jnp.cos(ang), (1, 2 * heads)), jnp.tile(jnp.sin(ang) * sign, (1, 2 * heads))


def _moe(hm, aff_t, tri, w_gate, w_up, w_down, layer):
    b, n, _ = hm.shape
    ne = aff_t.shape[1]
    cap = CAPACITY_FACTOR * n // ne
    tt = _tile(n, 512)
    nt = n // tt
    win = min(SLOT_WINDOW, cap)
    pos, first = _route(aff_t, tri, cap=cap, tt=tt)
    first = first[:, :, :nt + 1].astype(jnp.int32)
    offs = jnp.minimum(first[:, :, :nt] // 16 * 16, cap - win)
    fits = jnp.all(first[:, :, 1:] <= offs + win)
    a_flat = offs.reshape(-1)
    xs, gs = lax.cond(fits,
                      lambda: _gather_win(a_flat, pos, aff_t, hm, cap=cap, win=win, tt=tt),
                      lambda: _gather(pos, aff_t, hm, cap=cap))
    yg = _ffn(xs, w_gate, w_up, w_down, gs, layer=layer, cap=cap)
    return lax.cond(fits,
                    lambda: _combine_win(a_flat, pos, yg, cap=cap, win=win, tt=tt),
                    lambda: _combine(pos, yg, cap=cap))


def kernel(x, c, ctx, c_ctx, w_ada, b_ada, norm1_g, norm2_g, w_in, w_out, lambda_q1, lambda_k1, lambda_q2, lambda_k2, subln_g, sgu_norm_g, sgu_w, sgu_b, w_router, w_gate, w_up, w_down, norm_f_g):
    b, n, d = x.shape
    m = ctx.shape[1]
    depth = w_in.shape[0]
    sw = sgu_norm_g.shape[1]
    aw = (w_in.shape[2] - 2 * sw) // 3
    heads = aw // (2 * HEAD_DIM)
    bf = lambda t: t.astype(MXU_DTYPE)

    rows = -(-(b + 1) // 8) * 8
    cc = jnp.concatenate([c, c_ctx[None, :], jnp.zeros((rows - b - 1, d), F32)], axis=0)
    mod = _ada(cc, w_ada, b_ada)

    w_in_all = bf(w_in)
    cos, sin = _rope_tables(n, heads)
    cos_c = jnp.ones((m, aw), F32)
    sin_c = jnp.zeros((m, aw), F32)
    hw_s = sw // SGU_HEADS
    ones_blk = bf(jnp.kron(jnp.eye(SGU_HEADS, dtype=F32), jnp.full((hw_s, hw_s), 1.0 / hw_s, F32)))
    tri = bf(jnp.triu(jnp.ones((LANES, LANES), F32)))

    cx = ctx
    res_x = res_c = None
    for l in range(depth):
        last = l == depth - 1
        lam_init = 0.8 - 0.6 * math.exp(-0.3 * l)
        parts = [mod[l, :, i * d:(i + 1) * d] for i in range(6)]
        sh1, sc1, g1, sh2, sc2, g2 = [t[:b, None, :] for t in parts]
        csh1, csc1, cg1, csh2, csc2, cg2 = [jnp.broadcast_to(t[b][None, None, :], (b, 1, d)) for t in parts]

        w_in_p = w_in_all[l]
        w_out_l = bf(w_out[l])
        wr_t = bf(w_router[l].T)
        ng1, ng2 = norm1_g[l][None, :], norm2_g[l][None, :]
        sng = sgu_norm_g[l][None, :]
        ws = bf(sgu_w[l].reshape(SGU_HEADS * CHUNK, CHUNK))
        bs = jnp.repeat(sgu_b[l].T, hw_s, axis=1)
        lam_params = jnp.zeros((8, LANES), F32).at[:4, :HEAD_DIM].set(
            jnp.stack([lambda_q1[l], lambda_k1[l], lambda_q2[l], lambda_k2[l]]))
        sub_g = subln_g[l][None, :]
        shared = dict(aw=aw, sw=sw)

        x, (q, k, v, sg) = _proj(x, res_x, ng1, sh1, sc1, w_in_p, cos, sin, sng, ws, bs, ones_blk, rope=True, **shared)
        cx, (qc, kc, vc, sgc) = _proj(cx, res_c, ng1, csh1, csc1, w_in_p, cos_c, sin_c, sng, ws, bs, ones_blk,
                                      rope=False, **shared)
        o = _attn(lam_params, q, jnp.concatenate([kc, k], axis=1), jnp.concatenate([vc, v], axis=1), sub_g,
                  lam_init=lam_init)
        x, hm, aff_t = _outproj(o, sg, w_out_l, x, g1, ng2, sh2, sc2, wr_t)
        res_x = (_moe(hm, aff_t, tri, w_gate, w_up, w_down, l), g2)
        if not last:
            oc = _attn(lam_params, qc, kc, vc, sub_g, lam_init=lam_init)
            cx, hmc, aff_c = _outproj(oc, sgc, w_out_l, cx, cg1, ng2, csh2, csc2, wr_t)
            res_c = (_moe(hmc, aff_c, tri, w_gate, w_up, w_down, l), cg2)
    return _final(x, res_x[0], res_x[1], norm_f_g[None, :])
```

```python
import functools
import math

import jax
import jax.numpy as jnp
from jax import lax
from jax.experimental import pallas as pl
from jax.experimental.pallas import tpu as pltpu

F32 = jnp.float32
MXU_DTYPE = jnp.bfloat16

EPS = 1e-6
GRID_W = 64
ROPE_THETA = 10000.0
HEAD_DIM = 64
SGU_HEADS = 8
CHUNK = 128
N_EXPERTS = 16
CAPACITY_FACTOR = 2
LANES = 128
SLOT_WINDOW = 128
VMEM_LIMIT = 56 * 1024 * 1024


def _cparams(*sem):
    return pltpu.CompilerParams(dimension_semantics=sem, vmem_limit_bytes=VMEM_LIMIT)


def _dot(a, b):
    return jnp.dot(a.astype(MXU_DTYPE), b.astype(MXU_DTYPE), preferred_element_type=F32)


def _dot_nt(a, b):
    return lax.dot_general(a.astype(MXU_DTYPE), b.astype(MXU_DTYPE), (((1,), (1,)), ((), ())),
                           preferred_element_type=F32)


def _dot_tn(a, b):
    return lax.dot_general(a.astype(MXU_DTYPE), b.astype(MXU_DTYPE), (((0,), (0,)), ((), ())),
                           preferred_element_type=F32)


def _split_hi_lo(x):
    hi = x.astype(MXU_DTYPE)
    lo = (x - hi.astype(F32)).astype(MXU_DTYPE)
    return hi, lo


def _silu(x):
    return x * jax.nn.sigmoid(x)


def _gelu(x):
    return 0.5 * x * (1.0 + lax.erf(x * (2.0 ** -0.5)))


def _tile(n, pref):
    t = pref
    while t > 8 and n % t:
        t //= 2
    return t if n % t == 0 else n


def _ada_kernel(c_ref, w_ref, b_ref, o_ref):
    o_ref[0] = _dot(_silu(c_ref[...]), w_ref[0]) + b_ref[0]


def _ada(cc, w_ada, b_ada):
    depth, d, n6 = w_ada.shape
    rows = cc.shape[0]
    tn = _tile(n6, 1536)
    return pl.pallas_call(
        _ada_kernel,
        grid=(depth, n6 // tn),
        in_specs=[pl.BlockSpec((rows, d), lambda l, j: (0, 0)),
                  pl.BlockSpec((1, d, tn), lambda l, j: (l, 0, j)),
                  pl.BlockSpec((1, 1, tn), lambda l, j: (l, 0, j))],
        out_specs=pl.BlockSpec((1, rows, tn), lambda l, j: (l, 0, j)),
        out_shape=jax.ShapeDtypeStruct((depth, rows, n6), F32),
        compiler_params=_cparams("parallel", "parallel"), name="ada",
    )(cc, w_ada, b_ada.reshape(depth, 1, n6))


def _proj_kernel(*refs, has_res, rope, aw, sw):
    if has_res:
        x_ref, moe_ref, g2_ref, *refs = refs
    else:
        x_ref, *refs = refs
    (ng_ref, sh_ref, sc_ref, w_ref, cos_ref, sin_ref, sng_ref, ws_ref, bs_ref, ones_ref, *outs) = refs
    if has_res:
        xo_ref, q_ref, k_ref, v_ref, sg_ref = outs
    else:
        q_ref, k_ref, v_ref, sg_ref = outs

    x = x_ref[0]
    if has_res:
        x = x + g2_ref[0] * moe_ref[0]
        xo_ref[0] = x
    h = x * lax.rsqrt(jnp.mean(x * x, axis=-1, keepdims=True) + EPS) * ng_ref[...]
    h = h * (1.0 + sc_ref[0]) + sh_ref[0]
    p = _dot(h, w_ref[...])

    q = p[:, :aw]
    k = p[:, aw:2 * aw]
    if rope:
        first = (lax.broadcasted_iota(jnp.int32, q.shape, 1) % 32) < 16
        cos = cos_ref[...]
        sin = sin_ref[...]

        def rot(t):
            return jnp.where(first, pltpu.roll(t, aw - 16, 1), pltpu.roll(t, 16, 1))

        q = q * cos + rot(q) * sin
        k = k * cos + rot(k) * sin
    q_ref[0] = (q * (HEAD_DIM ** -0.5 * math.log2(math.e))).astype(q_ref.dtype)
    k_ref[0] = k.astype(k_ref.dtype)
    v_ref[0] = p[:, 2 * aw:3 * aw].astype(v_ref.dtype)

    u = _gelu(p[:, 3 * aw:3 * aw + sw])
    gv = _gelu(p[:, 3 * aw + sw:])
    sq_hi, sq_lo = _split_hi_lo(gv * gv)
    ms = _dot(sq_hi, ones_ref[...]) + _dot(sq_lo, ones_ref[...])
    gvn = (gv * lax.rsqrt(ms + EPS) * sng_ref[...]).astype(MXU_DTYPE)
    head_of_lane = lax.broadcasted_iota(jnp.int32, (CHUNK, sw), 1) // (sw // SGU_HEADS)
    tm = x.shape[0]
    for ci in range(tm // CHUNK):
        rows = slice(ci * CHUNK, (ci + 1) * CHUNK)
        r = _dot(ws_ref[...], gvn[rows])
        s = bs_ref[...]
        for hh in range(SGU_HEADS):
            s = s + jnp.where(head_of_lane == hh, r[hh * CHUNK:(hh + 1) * CHUNK], 0.0)
        sg_ref[0, rows, :] = (u[rows] * s).astype(sg_ref.dtype)


def _proj(x, res, ng, sh, sc, w_in_p, cos, sin, sng, ws, bs, ones_blk, *, rope, aw, sw):
    b, n, d = x.shape
    pw = w_in_p.shape[1]
    tm = _tile(n, 512)
    has_res = res is not None
    tok = lambda bi, i: (bi, i, 0)
    per_b = lambda bi, i: (bi, 0, 0)
    const2 = lambda bi, i: (0, 0)
    in_specs = [pl.BlockSpec((1, tm, d), tok)]
    args = [x]
    if has_res:
        moe, g2 = res
        in_specs += [pl.BlockSpec((1, tm, d), tok), pl.BlockSpec((1, 1, d), per_b)]
        args += [moe, g2]
    in_specs += [pl.BlockSpec((1, d), const2), pl.BlockSpec((1, 1, d), per_b), pl.BlockSpec((1, 1, d), per_b),
                 pl.BlockSpec((d, pw), const2),
                 pl.BlockSpec((tm, aw), lambda bi, i: (i, 0)), pl.BlockSpec((tm, aw), lambda bi, i: (i, 0)),
                 pl.BlockSpec((1, sw), const2), pl.BlockSpec((SGU_HEADS * CHUNK, CHUNK), const2),
                 pl.BlockSpec((CHUNK, sw), const2), pl.BlockSpec((sw, sw), const2)]
    args += [ng, sh, sc, w_in_p, cos, sin, sng, ws, bs, ones_blk]
    out_specs = [pl.BlockSpec((1, tm, aw), tok)] * 3 + [pl.BlockSpec((1, tm, sw), tok)]
    out_shape = [jax.ShapeDtypeStruct((b, n, aw), MXU_DTYPE)] * 3 + [jax.ShapeDtypeStruct((b, n, sw), MXU_DTYPE)]
    if has_res:
        out_specs = [pl.BlockSpec((1, tm, d), tok)] + out_specs
        out_shape = [jax.ShapeDtypeStruct((b, n, d), F32)] + out_shape
    outs = pl.pallas_call(
        functools.partial(_proj_kernel, has_res=has_res, rope=rope, aw=aw, sw=sw),
        grid=(b, n // tm),
        in_specs=in_specs, out_specs=out_specs, out_shape=out_shape,
        compiler_params=_cparams("parallel", "parallel"), name="proj_lat" if rope else "proj_ctx",
    )(*args)
    if has_res:
        return outs[0], outs[1:]
    return x, outs


def _attn_kernel(lam_ref, q0_ref, q1_ref, *refs, tq, tkc, lam_init):
    *key_refs, g_ref, o_ref, s0_ref, s1_ref, m0_ref, m1_ref = refs
    sources = [key_refs[i:i + 3] for i in range(0, len(key_refs), 3)]
    chunks, col = [], 0
    for k0_ref, k1_ref, v_ref in sources:
        for r0 in range(0, v_ref.shape[1], tkc):
            chunks.append(((k0_ref, k1_ref), v_ref, r0, col))
            col += tkc
    q_ref = (q0_ref, q1_ref)
    s_ref, m_ref = (s0_ref, s1_ref), (m0_ref, m1_ref)
    hw = q0_ref.shape[2]
    nt = q0_ref.shape[1] // tq
    lane = lax.broadcasted_iota(jnp.int32, (tq, hw), 1)
    mine = (lane >= HEAD_DIM) == (pl.program_id(1) % 2 == 1)
    lp = lam_ref[...]
    lam = (jnp.exp(jnp.sum(lp[0:1] * lp[1:2], axis=-1, keepdims=True))
           - jnp.exp(jnp.sum(lp[2:3] * lp[3:4], axis=-1, keepdims=True)) + lam_init)

    def rows(t):
        return pl.ds(t * tq if isinstance(t, int) else pl.multiple_of(t * tq, tq), tq)

    def phase(t, slot, *, do_scores=True, do_values=True):
        other = 1 - slot
        if do_scores:
            qc = []
            for c in range(2):
                q = q_ref[c][0, rows(t + 1), :]
                qc.append(jnp.where(mine, q, jnp.zeros_like(q)))
            run = [None, None]
        if do_values:
            top = [jnp.max(m_ref[slot][c], axis=-1, keepdims=True) for c in range(2)]
            acc = [jnp.zeros((tq, 2 * hw), F32), jnp.zeros((tq, 2 * hw), F32)]
        for k_ref, v_ref, r0, col in chunks:
            cols = slice(col, col + tkc)
            if do_values:
                vals = v_ref[0, r0:r0 + tkc, :]
                vals1 = jnp.concatenate([vals, jnp.ones_like(vals)], axis=1)
            for c in range(2):
                if do_scores:
                    s = _dot_nt(qc[c], k_ref[c][0, r0:r0 + tkc, :])
                    s_ref[other][c, :, cols] = s
                    for j in range(tkc // LANES):
                        part = s[:, j * LANES:(j + 1) * LANES]
                        run[c] = part if run[c] is None else jnp.maximum(run[c], part)
                if do_values:
                    p = jnp.exp2(s_ref[slot][c, :, cols] - top[c])
                    acc[c] = acc[c] + _dot(p, vals1)
        if do_scores:
            for c in range(2):
                m_ref[other][c] = run[c]
        if do_values:
            o = acc[0][:, :hw] / acc[0][:, hw:] - lam * (acc[1][:, :hw] / acc[1][:, hw:])
            o = o * lax.rsqrt(jnp.mean(o * o, axis=-1, keepdims=True) + EPS) * g_ref[...] * (1.0 - lam_init)
            o_ref[0, rows(t), :] = o.astype(o_ref.dtype)

    phase(-1, 1, do_values=False)

    def step(t, carry):
        for slot in range(2):
            @pl.when(t % 2 == slot)
            def _():
                phase(t, slot)
        return carry

    lax.fori_loop(0, nt - 1, step, 0)
    phase(nt - 1, (nt - 1) % 2, do_scores=False)


def _attn(lam_params, q, kv_sources, subln_g, *, lam_init):
    b, n, aw = q.shape
    lens = [k.shape[1] for k, _ in kv_sources]
    nkeys = sum(lens)
    hw = 2 * HEAD_DIM
    heads = aw // hw
    pairs = heads // 2
    tq = _tile(n, 256)
    tkc = min(_tile(m, 256) for m in lens)
    comp0 = lambda bi, h: (bi, 0, h // 2)
    comp1 = lambda bi, h: (bi, 0, pairs + h // 2)
    key_specs, key_args = [], []
    for (k, v), m in zip(kv_sources, lens):
        key_specs += [pl.BlockSpec((1, m, hw), comp0), pl.BlockSpec((1, m, hw), comp1),
                      pl.BlockSpec((1, m, hw), lambda bi, h: (bi, 0, h))]
        key_args += [k, k, v]
    return pl.pallas_call(
        functools.partial(_attn_kernel, tq=tq, tkc=tkc, lam_init=lam_init),
        grid=(b, heads),
        in_specs=[pl.BlockSpec(lam_params.shape, lambda bi, h: (0, 0)),
                  pl.BlockSpec((1, n, hw), comp0), pl.BlockSpec((1, n, hw), comp1), *key_specs,
                  pl.BlockSpec((1, hw), lambda bi, h: (0, 0))],
        out_specs=pl.BlockSpec((1, n, hw), lambda bi, h: (bi, 0, h)),
        out_shape=jax.ShapeDtypeStruct((b, n, aw), MXU_DTYPE),
        scratch_shapes=[pltpu.VMEM((2, tq, nkeys), F32)] * 2 + [pltpu.VMEM((2, tq, LANES), F32)] * 2,
        compiler_params=_cparams("parallel", "parallel"), name=f"attn_{n}",
    )(lam_params, q, q, *key_args, subln_g)


def _out_kernel(o_ref, sg_ref, w_ref, x_ref, g1_ref, ng_ref, sh_ref, sc_ref, wr_ref, xo_ref, hm_ref, aff_ref, *, aw):
    y = _dot(o_ref[0], w_ref[:aw, :]) + _dot(sg_ref[0], w_ref[aw:, :])
    x = x_ref[0] + g1_ref[0] * y
    xo_ref[0] = x
    h = x * lax.rsqrt(jnp.mean(x * x, axis=-1, keepdims=True) + EPS) * ng_ref[...]
    h = h * (1.0 + sc_ref[0]) + sh_ref[0]
    hm_ref[0] = h.astype(hm_ref.dtype)
    logits = _dot_nt(wr_ref[...], h)
    e = jnp.exp(logits - jnp.max(logits, axis=0, keepdims=True))
    aff_ref[0] = e / jnp.sum(e, axis=0, keepdims=True)


def _outproj(o, sg, w_out, x, g1, ng, sh, sc, wr_t):
    b, n, d = x.shape
    aw = o.shape[2]
    sw = sg.shape[2]
    ne = wr_t.shape[0]
    tm = _tile(n, 512)
    tok = lambda bi, i: (bi, i, 0)
    per_b = lambda bi, i: (bi, 0, 0)
    const2 = lambda bi, i: (0, 0)
    return pl.pallas_call(
        functools.partial(_out_kernel, aw=aw),
        grid=(b, n // tm),
        in_specs=[pl.BlockSpec((1, tm, aw), tok), pl.BlockSpec((1, tm, sw), tok),
                  pl.BlockSpec((aw + sw, d), const2), pl.BlockSpec((1, tm, d), tok),
                  pl.BlockSpec((1, 1, d), per_b), pl.BlockSpec((1, d), const2),
                  pl.BlockSpec((1, 1, d), per_b), pl.BlockSpec((1, 1, d), per_b),
                  pl.BlockSpec((ne, d), const2)],
        out_specs=[pl.BlockSpec((1, tm, d), tok), pl.BlockSpec((1, tm, d), tok),
                   pl.BlockSpec((1, ne, tm), lambda bi, i: (bi, 0, i))],
        out_shape=[jax.ShapeDtypeStruct((b, n, d), F32), jax.ShapeDtypeStruct((b, n, d), MXU_DTYPE),
                   jax.ShapeDtypeStruct((b, ne, n), F32)],
        compiler_params=_cparams("parallel", "parallel"), name=f"outproj_{n}",
    )(o, sg, w_out, x, g1, ng, sh, sc, wr_t)


def _route_kernel(aff_ref, tri_ref, pos_ref, start_ref, *, cap, tt):
    a = aff_ref[0]
    ne, n = a.shape
    bits = lax.bitcast_convert_type(a, jnp.int32)
    t = jnp.zeros((ne, 1), jnp.int32)
    for bit in range(30, -1, -1):
        cand = t | (1 << bit)
        cnt = jnp.sum((bits >= cand).astype(jnp.int32), axis=1, keepdims=True)
        t = jnp.where(cnt >= cap, cand, t)
    gt = bits > t
    eq = bits == t
    need = cap - jnp.sum(gt.astype(jnp.int32), axis=1, keepdims=True)

    def exclusive_count(mask):
        out = []
        runs = []
        run = jnp.zeros((ne, 1), F32)
        for ci in range(n // LANES):
            runs.append(run)
            m = mask[:, ci * LANES:(ci + 1) * LANES].astype(F32)
            incl = _dot(m, tri_ref[...])
            out.append(run + incl - m)
            run = run + incl[:, LANES - 1:LANES]
        return jnp.concatenate(out, axis=1), runs

    sel = gt | (eq & (exclusive_count(eq)[0] < need.astype(F32)))
    pos, runs = exclusive_count(sel)
    pos_ref[0] = jnp.where(sel, pos, -1.0)
    lane = lax.broadcasted_iota(jnp.int32, (ne, LANES), 1)
    first = jnp.full((ne, LANES), float(cap), F32)
    for j in range(n // tt):
        first = jnp.where(lane == j, runs[j * tt // LANES], first)
    start_ref[0] = first


def _route(aff_t, tri, *, cap, tt):
    b, ne, n = aff_t.shape
    return pl.pallas_call(
        functools.partial(_route_kernel, cap=cap, tt=tt),
        grid=(b,),
        in_specs=[pl.BlockSpec((1, ne, n), lambda bi: (bi, 0, 0)), pl.BlockSpec((LANES, LANES), lambda bi: (0, 0))],
        out_specs=[pl.BlockSpec((1, ne, n), lambda bi: (bi, 0, 0)), pl.BlockSpec((1, ne, LANES), lambda bi: (bi, 0, 0))],
        out_shape=[jax.ShapeDtypeStruct((b, ne, n), F32), jax.ShapeDtypeStruct((b, ne, LANES), F32)],
        compiler_params=_cparams("parallel"), name=f"route_{n}",
    )(aff_t, tri)


def _window_hits(a_ref, pos_ref, *, win):
    bi, t, nt = pl.program_id(0), pl.program_id(1), pl.num_programs(1)
    ne, tt = pos_ref.shape[1], pos_ref.shape[2]
    base = lax.broadcasted_iota(jnp.int32, (win, tt), 0)
    offs, hits = [], []
    for e in range(ne):
        a_e = a_ref[(bi * ne + e) * nt + t]
        offs.append(a_e)
        hits.append(pos_ref[0, e:e + 1, :] == (base + a_e).astype(F32))
    return offs, hits


def _gather_win_kernel(a_ref, pos_ref, aff_ref, hm_ref, xs_ref, gs_ref, *, cap, win):
    @pl.when(pl.program_id(1) == 0)
    def _():
        xs_ref[...] = jnp.zeros_like(xs_ref)
        gs_ref[...] = jnp.zeros_like(gs_ref)

    ne = pos_ref.shape[1]
    offs, hits = _window_hits(a_ref, pos_ref, win=win)
    group = 8
    for e0 in range(0, ne, group):
        onehot = jnp.concatenate([h.astype(MXU_DTYPE) for h in hits[e0:e0 + group]], axis=0)
        picked = _dot(onehot, hm_ref[0]).astype(xs_ref.dtype)
        for e in range(e0, min(e0 + group, ne)):
            rows = pl.ds(pl.multiple_of(e * cap + offs[e], 16), win)
            gate = jnp.sum(jnp.where(hits[e], aff_ref[0, e:e + 1, :], 0.0), axis=1, keepdims=True)
            xs_ref[0, rows, :] += picked[(e - e0) * win:(e - e0 + 1) * win]
            gs_ref[0, rows, :] += jnp.broadcast_to(gate, (win, LANES))


def _gather_win(a_flat, pos, aff_t, hm, *, cap, win, tt):
    b, ne, n = pos.shape
    d = hm.shape[2]
    tile = lambda bi, t, a: (bi, 0, t)
    whole = lambda bi, t, a: (bi, 0, 0)
    return pl.pallas_call(
        functools.partial(_gather_win_kernel, cap=cap, win=win),
        grid_spec=pltpu.PrefetchScalarGridSpec(
            num_scalar_prefetch=1, grid=(b, n // tt),
            in_specs=[pl.BlockSpec((1, ne, tt), tile), pl.BlockSpec((1, ne, tt), tile),
                      pl.BlockSpec((1, tt, d), lambda bi, t, a: (bi, t, 0))],
            out_specs=[pl.BlockSpec((1, ne * cap, d), whole), pl.BlockSpec((1, ne * cap, LANES), whole)]),
        out_shape=[jax.ShapeDtypeStruct((b, ne * cap, d), MXU_DTYPE),
                   jax.ShapeDtypeStruct((b, ne * cap, LANES), F32)],
        compiler_params=_cparams("parallel", "arbitrary"), name=f"gather_win_{n}",
    )(a_flat, pos, aff_t, hm)


def _gather_kernel(pos_ref, aff_ref, hm_ref, xs_ref, gs_ref, *, cap, tn):
    n = hm_ref.shape[1]
    d = hm_ref.shape[2]
    slot = lax.broadcasted_iota(jnp.int32, (cap, tn), 0).astype(F32)
    xs = jnp.zeros((cap, d), F32)
    gs = jnp.zeros((cap, 1), F32)
    for ci in range(n // tn):
        cols = slice(ci * tn, (ci + 1) * tn)
        hit = pos_ref[0, 0, :, cols] == slot
        xs = xs + _dot(hit.astype(MXU_DTYPE), hm_ref[0, cols, :])
        gs = gs + jnp.sum(jnp.where(hit, aff_ref[0, 0, :, cols], 0.0), axis=1, keepdims=True)
    xs_ref[0] = xs.astype(xs_ref.dtype)
    gs_ref[0] = jnp.broadcast_to(gs, (cap, LANES))


def _gather(pos, aff_t, hm, *, cap):
    b, ne, n = pos.shape
    d = hm.shape[2]
    tn = _tile(n, 512)
    pos4 = pos.reshape(b, ne, 1, n)
    aff4 = aff_t.reshape(b, ne, 1, n)
    return pl.pallas_call(
        functools.partial(_gather_kernel, cap=cap, tn=tn),
        grid=(b, ne),
        in_specs=[pl.BlockSpec((1, 1, 1, n), lambda bi, e: (bi, e, 0, 0)),
                  pl.BlockSpec((1, 1, 1, n), lambda bi, e: (bi, e, 0, 0)),
                  pl.BlockSpec((1, n, d), lambda bi, e: (bi, 0, 0))],
        out_specs=[pl.BlockSpec((1, cap, d), lambda bi, e: (bi, e, 0)),
                   pl.BlockSpec((1, cap, LANES), lambda bi, e: (bi, e, 0))],
        out_shape=[jax.ShapeDtypeStruct((b, ne * cap, d), MXU_DTYPE),
                   jax.ShapeDtypeStruct((b, ne * cap, LANES), F32)],
        compiler_params=_cparams("parallel", "parallel"), name=f"gather_{n}",
    )(pos4, aff4, hm)


def _ffn_kernel(x_ref, wg_ref, wu_ref, wd_ref, gs_ref, y_ref, acc_ref):
    f = pl.program_id(2)

    @pl.when(f == 0)
    def _():
        acc_ref[...] = jnp.zeros_like(acc_ref)

    tm, d = acc_ref.shape
    x = x_ref[...].reshape(tm, d)
    hid = _silu(_dot(x, wg_ref[0, 0])) * _dot(x, wu_ref[0, 0])
    acc_ref[...] += _dot(hid, wd_ref[0, 0])

    @pl.when(f == pl.num_programs(2) - 1)
    def _():
        y_ref[0] = acc_ref[...] * gs_ref[...].reshape(tm, LANES)[:, :1]


def _ffn(xs, w_gate, w_up, w_down, gs, *, layer, cap):
    b, rows, d = xs.shape
    _, ne, _, ff = w_gate.shape
    bt = b
    while bt * cap > 1024 and bt % 2 == 0:
        bt //= 2
    tm = bt * cap
    tf = _tile(ff, 1024)
    return pl.pallas_call(
        _ffn_kernel,
        grid=(ne, b // bt, ff // tf),
        in_specs=[pl.BlockSpec((bt, 1, cap, d), lambda e, i, f: (i, e, 0, 0)),
                  pl.BlockSpec((1, 1, d, tf), lambda e, i, f: (layer, e, 0, f)),
                  pl.BlockSpec((1, 1, d, tf), lambda e, i, f: (layer, e, 0, f)),
                  pl.BlockSpec((1, 1, tf, d), lambda e, i, f: (layer, e, f, 0)),
                  pl.BlockSpec((bt, 1, cap, LANES), lambda e, i, f: (i, e, 0, 0))],
        out_specs=pl.BlockSpec((1, tm, d), lambda e, i, f: (e, i, 0)),
        out_shape=jax.ShapeDtypeStruct((ne, b * cap, d), F32),
        scratch_shapes=[pltpu.VMEM((tm, d), F32)],
        compiler_params=_cparams("parallel", "parallel", "arbitrary"), name=f"ffn_{b * cap}",
    )(xs.reshape(b, ne, cap, d), w_gate, w_up, w_down, gs.reshape(b, ne, cap, LANES))


def _combine_win_kernel(a_ref, pos_ref, *refs, win):
    y_refs, o_ref = refs[:-1], refs[-1]
    d = o_ref.shape[2]
    _, hits = _window_hits(a_ref, pos_ref, win=win)
    onehot = jnp.concatenate([h.astype(MXU_DTYPE) for h in hits], axis=0)
    parts = [_split_hi_lo(y_ref[0]) for y_ref in y_refs]
    y_all = jnp.concatenate([jnp.concatenate([p[0] for p in parts], axis=0),
                             jnp.concatenate([p[1] for p in parts], axis=0)], axis=1)
    r = _dot_tn(onehot, y_all)
    o_ref[0] = r[:, :d] + r[:, d:]


def _combine_win(a_flat, pos, yg, *, cap, win, tt):
    b, ne, n = pos.shape
    d = yg.shape[2]
    nt = n // tt

    def window(e):
        return pl.BlockSpec((pl.Element(1), pl.Element(win), pl.Element(d)),
                            lambda bi, t, a: (e, pl.multiple_of(bi * cap + a[(bi * ne + e) * nt + t], 16), 0))

    return pl.pallas_call(
        functools.partial(_combine_win_kernel, win=win),
        grid_spec=pltpu.PrefetchScalarGridSpec(
            num_scalar_prefetch=1, grid=(b, nt),
            in_specs=[pl.BlockSpec((1, ne, tt), lambda bi, t, a: (bi, 0, t))] + [window(e) for e in range(ne)],
            out_specs=pl.BlockSpec((1, tt, d), lambda bi, t, a: (bi, t, 0))),
        out_shape=jax.ShapeDtypeStruct((b, n, d), F32),
        compiler_params=_cparams("parallel", "parallel"), name=f"combine_win_{n}",
    )(a_flat, pos, *([yg] * ne))


def _combine_kernel(pos_ref, y_ref, o_ref, *, cap, tn):
    e = pl.program_id(1)

    @pl.when(e == 0)
    def _():
        o_ref[...] = jnp.zeros_like(o_ref)

    n = o_ref.shape[1]
    y_hi, y_lo = _split_hi_lo(y_ref[0])
    slot = lax.broadcasted_iota(jnp.int32, (cap, tn), 0).astype(F32)
    for ci in range(n // tn):
        cols = slice(ci * tn, (ci + 1) * tn)
        hit = (pos_ref[0, 0, :, cols] == slot).astype(MXU_DTYPE)
        o_ref[0, cols, :] += _dot_tn(hit, y_hi) + _dot_tn(hit, y_lo)


def _combine(pos, yg, *, cap):
    b, ne, n = pos.shape
    d = yg.shape[2]
    tn = _tile(n, 512)
    return pl.pallas_call(
        functools.partial(_combine_kernel, cap=cap, tn=tn),
        grid=(b, ne),
        in_specs=[pl.BlockSpec((1, 1, 1, n), lambda bi, e: (bi, e, 0, 0)),
                  pl.BlockSpec((1, cap, d), lambda bi, e: (e, bi, 0))],
        out_specs=pl.BlockSpec((1, n, d), lambda bi, e: (bi, 0, 0)),
        out_shape=jax.ShapeDtypeStruct((b, n, d), F32),
        compiler_params=_cparams("parallel", "arbitrary"), name=f"combine_{n}",
    )(pos.reshape(b, ne, 1, n), yg)


def _final_kernel(x_ref, moe_ref, g2_ref, ng_ref, o_ref):
    x = x_ref[0] + g2_ref[0] * moe_ref[0]
    o_ref[0] = x * lax.rsqrt(jnp.mean(x * x, axis=-1, keepdims=True) + EPS) * ng_ref[...]


def _final(x, moe, g2, ng):
    b, n, d = x.shape
    tm = _tile(n, 512)
    tok = lambda bi, i: (bi, i, 0)
    return pl.pallas_call(
        _final_kernel,
        grid=(b, n // tm),
        in_specs=[pl.BlockSpec((1, tm, d), tok), pl.BlockSpec((1, tm, d), tok),
                  pl.BlockSpec((1, 1, d), lambda bi, i: (bi, 0, 0)), pl.BlockSpec((1, d), lambda bi, i: (0, 0))],
        out_specs=pl.BlockSpec((1, tm, d), tok),
        out_shape=jax.ShapeDtypeStruct((b, n, d), F32),
        compiler_params=_cparams("parallel", "parallel"), name="final",
    )(x, moe, g2, ng)


def _rope_tables(n, heads):
    rows = n // GRID_W
    row = jnp.repeat(jnp.arange(rows), GRID_W).astype(F32)
    col = jnp.tile(jnp.arange(GRID_W), rows).astype(F32)
    n_freq = HEAD_DIM // 4
    freqs = ROPE_THETA ** (-jnp.arange(n_freq, dtype=F32) / n_freq)
    ang_r = row[:, None] * freqs
    ang_c = col[:, None] * freqs
    ang = jnp.concatenate([ang_r, ang_r, ang_c, ang_c], axis=-1)
    sign = jnp.where((jnp.arange(HEAD_DIM) % 32) < 16, -1.0, 1.0).astype(F32)
    return jnp.tile(jnp.cos(ang), (1, 2 * heads)), jnp.tile(jnp.sin(ang) * sign, (1, 2 * heads))


def _moe(hm, aff_t, tri, w_gate, w_up, w_down, layer):
    b, n, _ = hm.shape
    ne = aff_t.shape[1]
    cap = CAPACITY_FACTOR * n // ne
    tt = _tile(n, 512)
    nt = n // tt
    win = min(SLOT_WINDOW, cap)
    pos, first = _route(aff_t, tri, cap=cap, tt=tt)
    first = first[:, :, :nt + 1].astype(jnp.int32)
    offs = jnp.minimum(first[:, :, :nt] // 16 * 16, cap - win)
    fits = jnp.all(first[:, :, 1:] <= offs + win)
    a_flat = offs.reshape(-1)
    xs, gs = lax.cond(fits,
                      lambda: _gather_win(a_flat, pos, aff_t, hm, cap=cap, win=win, tt=tt),
                      lambda: _gather(pos, aff_t, hm, cap=cap))
    yg = _ffn(xs, w_gate, w_up, w_down, gs, layer=layer, cap=cap)
    return lax.cond(fits,
                    lambda: _combine_win(a_flat, pos, yg, cap=cap, win=win, tt=tt),
                    lambda: _combine(pos, yg, cap=cap))


def kernel(x, c, ctx, c_ctx, w_ada, b_ada, norm1_g, norm2_g, w_in, w_out, lambda_q1, lambda_k1, lambda_q2, lambda_k2, subln_g, sgu_norm_g, sgu_w, sgu_b, w_router, w_gate, w_up, w_down, norm_f_g):
    b, n, d = x.shape
    m = ctx.shape[1]
    depth = w_in.shape[0]
    sw = sgu_norm_g.shape[1]
    aw = (w_in.shape[2] - 2 * sw) // 3
    heads = aw // (2 * HEAD_DIM)
    bf = lambda t: t.astype(MXU_DTYPE)

    rows = -(-(b + 1) // 8) * 8
    cc = jnp.concatenate([c, c_ctx[None, :], jnp.zeros((rows - b - 1, d), F32)], axis=0)
    mod = _ada(cc, w_ada, b_ada)

    w_in_all = bf(w_in)
    cos, sin = _rope_tables(n, heads)
    cos_c = jnp.ones((m, aw), F32)
    sin_c = jnp.zeros((m, aw), F32)
    hw_s = sw // SGU_HEADS
    ones_blk = bf(jnp.kron(jnp.eye(SGU_HEADS, dtype=F32), jnp.full((hw_s, hw_s), 1.0 / hw_s, F32)))
    tri = bf(jnp.triu(jnp.ones((LANES, LANES), F32)))

    cx = ctx
    res_x = res_c = None
    for l in range(depth):
        last = l == depth - 1
        lam_init = 0.8 - 0.6 * math.exp(-0.3 * l)
        parts = [mod[l, :, i * d:(i + 1) * d] for i in range(6)]
        sh1, sc1, g1, sh2, sc2, g2 = [t[:b, None, :] for t in parts]
        csh1, csc1, cg1, csh2, csc2, cg2 = [jnp.broadcast_to(t[b][None, None, :], (b, 1, d)) for t in parts]

        w_in_p = w_in_all[l]
        w_out_l = bf(w_out[l])
        wr_t = bf(w_router[l].T)
        ng1, ng2 = norm1_g[l][None, :], norm2_g[l][None, :]
        sng = sgu_norm_g[l][None, :]
        ws = bf(sgu_w[l].reshape(SGU_HEADS * CHUNK, CHUNK))
        bs = jnp.repeat(sgu_b[l].T, hw_s, axis=1)
        lam_params = jnp.zeros((8, LANES), F32).at[:4, :HEAD_DIM].set(
            jnp.stack([lambda_q1[l], lambda_k1[l], lambda_q2[l], lambda_k2[l]]))
        sub_g = subln_g[l][None, :]
        shared = dict(aw=aw, sw=sw)

        x, (q, k, v, sg) = _proj(x, res_x, ng1, sh1, sc1, w_in_p, cos, sin, sng, ws, bs, ones_blk, rope=True, **shared)
        cx, (qc, kc, vc, sgc) = _proj(cx, res_c, ng1, csh1, csc1, w_in_p, cos_c, sin_c, sng, ws, bs, ones_blk,
                                      rope=False, **shared)
        o = _attn(lam_params, q, [(kc, vc), (k, v)], sub_g, lam_init=lam_init)
        x, hm, aff_t = _outproj(o, sg, w_out_l, x, g1, ng2, sh2, sc2, wr_t)
        res_x = (_moe(hm, aff_t, tri, w_gate, w_up, w_down, l), g2)
        if not last:
            oc = _attn(lam_params, qc, [(kc, vc)], sub_g, lam_init=lam_init)
            cx, hmc, aff_c = _outproj(oc, sgc, w_out_l, cx, cg1, ng2, csh2, csc2, wr_t)
            res_c = (_moe(hmc, aff_c, tri, w_gate, w_up, w_down, l), cg2)
    return _final(x, res_x[0], res_x[1], norm_f_g[None, :])
```

```python
import functools
import math

import jax
import jax.numpy as jnp
from jax import lax
from jax.experimental import pallas as pl
from jax.experimental.pallas import tpu as pltpu

F32 = jnp.float32
MXU_DTYPE = jnp.bfloat16

EPS = 1e-6
GRID_W = 64
ROPE_THETA = 10000.0
HEAD_DIM = 64
SGU_HEADS = 8
CHUNK = 128
N_EXPERTS = 16
CAPACITY_FACTOR = 2
LANES = 128
SLOT_WINDOW = 128
VMEM_LIMIT = 56 * 1024 * 1024


def _cparams(*sem):
    return pltpu.CompilerParams(dimension_semantics=sem, vmem_limit_bytes=VMEM_LIMIT)


def _dot(a, b):
    return jnp.dot(a.astype(MXU_DTYPE), b.astype(MXU_DTYPE), preferred_element_type=F32)


def _dot_nt(a, b):
    return lax.dot_general(a.astype(MXU_DTYPE), b.astype(MXU_DTYPE), (((1,), (1,)), ((), ())),
                           preferred_element_type=F32)


def _dot_tn(a, b):
    return lax.dot_general(a.astype(MXU_DTYPE), b.astype(MXU_DTYPE), (((0,), (0,)), ((), ())),
                           preferred_element_type=F32)


def _split_hi_lo(x):
    hi = x.astype(MXU_DTYPE)
    lo = (x - hi.astype(F32)).astype(MXU_DTYPE)
    return hi, lo


def _silu(x):
    return x * jax.nn.sigmoid(x)


def _gelu(x):
    return 0.5 * x * (1.0 + lax.erf(x * (2.0 ** -0.5)))


def _tile(n, pref):
    t = pref
    while t > 8 and n % t:
        t //= 2
    return t if n % t == 0 else n


def _ada_kernel(c_ref, w_ref, b_ref, o_ref):
    o_ref[0] = _dot(_silu(c_ref[...]), w_ref[0]) + b_ref[0]


def _ada(cc, w_ada, b_ada):
    depth, d, n6 = w_ada.shape
    rows = cc.shape[0]
    tn = _tile(n6, 1536)
    return pl.pallas_call(
        _ada_kernel,
        grid=(depth, n6 // tn),
        in_specs=[pl.BlockSpec((rows, d), lambda l, j: (0, 0)),
                  pl.BlockSpec((1, d, tn), lambda l, j: (l, 0, j)),
                  pl.BlockSpec((1, 1, tn), lambda l, j: (l, 0, j))],
        out_specs=pl.BlockSpec((1, rows, tn), lambda l, j: (l, 0, j)),
        out_shape=jax.ShapeDtypeStruct((depth, rows, n6), F32),
        compiler_params=_cparams("parallel", "parallel"), name="ada",
    )(cc, w_ada, b_ada.reshape(depth, 1, n6))


def _proj_kernel(*refs, has_res, rope, aw, sw):
    if has_res:
        x_ref, moe_ref, g2_ref, *refs = refs
    else:
        x_ref, *refs = refs
    (ng_ref, sh_ref, sc_ref, w_ref, cos_ref, sin_ref, sng_ref, ws_ref, bs_ref, ones_ref, *outs) = refs
    if has_res:
        xo_ref, q_ref, k_ref, v_ref, sg_ref = outs
    else:
        q_ref, k_ref, v_ref, sg_ref = outs

    x = x_ref[0]
    if has_res:
        x = x + g2_ref[0] * moe_ref[0]
        xo_ref[0] = x
    h = x * lax.rsqrt(jnp.mean(x * x, axis=-1, keepdims=True) + EPS) * ng_ref[...]
    h = h * (1.0 + sc_ref[0]) + sh_ref[0]
    p = _dot(h, w_ref[...])

    q = p[:, :aw]
    k = p[:, aw:2 * aw]
    if rope:
        first = (lax.broadcasted_iota(jnp.int32, q.shape, 1) % 32) < 16
        cos = cos_ref[...]
        sin = sin_ref[...]

        def rot(t):
            return jnp.where(first, pltpu.roll(t, aw - 16, 1), pltpu.roll(t, 16, 1))

        q = q * cos + rot(q) * sin
        k = k * cos + rot(k) * sin
    q_ref[0] = (q * (HEAD_DIM ** -0.5 * math.log2(math.e))).astype(q_ref.dtype)
    k_ref[0] = k.astype(k_ref.dtype)
    v_ref[0] = p[:, 2 * aw:3 * aw].astype(v_ref.dtype)

    u = _gelu(p[:, 3 * aw:3 * aw + sw])
    gv = _gelu(p[:, 3 * aw + sw:])
    sq_hi, sq_lo = _split_hi_lo(gv * gv)
    ms = _dot(sq_hi, ones_ref[...]) + _dot(sq_lo, ones_ref[...])
    gvn = (gv * lax.rsqrt(ms + EPS) * sng_ref[...]).astype(MXU_DTYPE)
    head_of_lane = lax.broadcasted_iota(jnp.int32, (CHUNK, sw), 1) // (sw // SGU_HEADS)
    tm = x.shape[0]
    for ci in range(tm // CHUNK):
        rows = slice(ci * CHUNK, (ci + 1) * CHUNK)
        r = _dot(ws_ref[...], gvn[rows])
        s = bs_ref[...]
        for hh in range(SGU_HEADS):
            s = s + jnp.where(head_of_lane == hh, r[hh * CHUNK:(hh + 1) * CHUNK], 0.0)
        sg_ref[0, rows, :] = (u[rows] * s).astype(sg_ref.dtype)


def _proj(x, res, ng, sh, sc, w_in_p, cos, sin, sng, ws, bs, ones_blk, *, rope, aw, sw):
    b, n, d = x.shape
    pw = w_in_p.shape[1]
    tm = _tile(n, 512)
    has_res = res is not None
    tok = lambda bi, i: (bi, i, 0)
    per_b = lambda bi, i: (bi, 0, 0)
    const2 = lambda bi, i: (0, 0)
    in_specs = [pl.BlockSpec((1, tm, d), tok)]
    args = [x]
    if has_res:
        moe, g2 = res
        in_specs += [pl.BlockSpec((1, tm, d), tok), pl.BlockSpec((1, 1, d), per_b)]
        args += [moe, g2]
    in_specs += [pl.BlockSpec((1, d), const2), pl.BlockSpec((1, 1, d), per_b), pl.BlockSpec((1, 1, d), per_b),
                 pl.BlockSpec((d, pw), const2),
                 pl.BlockSpec((tm, aw), lambda bi, i: (i, 0)), pl.BlockSpec((tm, aw), lambda bi, i: (i, 0)),
                 pl.BlockSpec((1, sw), const2), pl.BlockSpec((SGU_HEADS * CHUNK, CHUNK), const2),
                 pl.BlockSpec((CHUNK, sw), const2), pl.BlockSpec((sw, sw), const2)]
    args += [ng, sh, sc, w_in_p, cos, sin, sng, ws, bs, ones_blk]
    out_specs = [pl.BlockSpec((1, tm, aw), tok)] * 3 + [pl.BlockSpec((1, tm, sw), tok)]
    out_shape = [jax.ShapeDtypeStruct((b, n, aw), MXU_DTYPE)] * 3 + [jax.ShapeDtypeStruct((b, n, sw), MXU_DTYPE)]
    if has_res:
        out_specs = [pl.BlockSpec((1, tm, d), tok)] + out_specs
        out_shape = [jax.ShapeDtypeStruct((b, n, d), F32)] + out_shape
    outs = pl.pallas_call(
        functools.partial(_proj_kernel, has_res=has_res, rope=rope, aw=aw, sw=sw),
        grid=(b, n // tm),
        in_specs=in_specs, out_specs=out_specs, out_shape=out_shape,
        compiler_params=_cparams("parallel", "parallel"), name="proj_lat" if rope else "proj_ctx",
    )(*args)
    if has_res:
        return outs[0], outs[1:]
    return x, outs


def _attn_kernel(lam_ref, q0_ref, q1_ref, *refs, tq, tkc, lam_init):
    *key_refs, g_ref, o_ref, s0_ref, s1_ref, m0_ref, m1_ref = refs
    sources = [key_refs[i:i + 3] for i in range(0, len(key_refs), 3)]
    chunks, col = [], 0
    for k0_ref, k1_ref, v_ref in sources:
        for r0 in range(0, v_ref.shape[1], tkc):
            chunks.append(((k0_ref, k1_ref), v_ref, r0, col))
            col += tkc
    q_ref = (q0_ref, q1_ref)
    s_ref, m_ref = (s0_ref, s1_ref), (m0_ref, m1_ref)
    hw = q0_ref.shape[2]
    nt = q0_ref.shape[1] // tq
    lane = lax.broadcasted_iota(jnp.int32, (tq, hw), 1)
    mine = (lane >= HEAD_DIM) == (pl.program_id(1) % 2 == 1)
    lp = lam_ref[...]
    lam = (jnp.exp(jnp.sum(lp[0:1] * lp[1:2], axis=-1, keepdims=True))
           - jnp.exp(jnp.sum(lp[2:3] * lp[3:4], axis=-1, keepdims=True)) + lam_init)

    def rows(t):
        return pl.ds(t * tq if isinstance(t, int) else pl.multiple_of(t * tq, tq), tq)

    def phase(t, slot, *, do_scores=True, do_values=True):
        other = 1 - slot
        if do_scores:
            qc = []
            for c in range(2):
                q = q_ref[c][0, rows(t + 1), :]
                qc.append(jnp.where(mine, q, jnp.zeros_like(q)))
            run = [None, None]
        if do_values:
            top = [jnp.max(m_ref[slot][c], axis=-1, keepdims=True) for c in range(2)]
            acc = [jnp.zeros((tq, 2 * hw), F32), jnp.zeros((tq, 2 * hw), F32)]
        for k_ref, v_ref, r0, col in chunks:
            cols = slice(col, col + tkc)
            if do_values:
                vals = v_ref[0, r0:r0 + tkc, :]
                vals1 = jnp.concatenate([vals, jnp.ones_like(vals)], axis=1)
            for c in range(2):
                if do_scores:
                    s = _dot_nt(qc[c], k_ref[c][0, r0:r0 + tkc, :])
                    s_ref[other][c, :, cols] = s
                    for j in range(tkc // LANES):
                        part = s[:, j * LANES:(j + 1) * LANES]
                        run[c] = part if run[c] is None else jnp.maximum(run[c], part)
                if do_values:
                    p = jnp.exp2(s_ref[slot][c, :, cols] - top[c])
                    acc[c] = acc[c] + _dot(p, vals1)
        if do_scores:
            for c in range(2):
                m_ref[other][c] = run[c]
        if do_values:
            o = acc[0][:, :hw] / acc[0][:, hw:] - lam * (acc[1][:, :hw] / acc[1][:, hw:])
            o = o * lax.rsqrt(jnp.mean(o * o, axis=-1, keepdims=True) + EPS) * g_ref[...] * (1.0 - lam_init)
            o_ref[0, rows(t), :] = o.astype(o_ref.dtype)

    phase(-1, 1, do_values=False)

    def step(t, carry):
        for slot in range(2):
            @pl.when(t % 2 == slot)
            def _():
                phase(t, slot)
        return carry

    lax.fori_loop(0, nt - 1, step, 0)
    phase(nt - 1, (nt - 1) % 2, do_scores=False)


def _attn(lam_params, q, kv_sources, subln_g, *, lam_init):
    b, n, aw = q.shape
    lens = [k.shape[1] for k, _ in kv_sources]
    nkeys = sum(lens)
    hw = 2 * HEAD_DIM
    heads = aw // hw
    pairs = heads // 2
    tq = _tile(n, 512)
    tkc = min(_tile(m, 256) for m in lens)
    comp0 = lambda bi, h: (bi, 0, h // 2)
    comp1 = lambda bi, h: (bi, 0, pairs + h // 2)
    key_specs, key_args = [], []
    for (k, v), m in zip(kv_sources, lens):
        key_specs += [pl.BlockSpec((1, m, hw), comp0), pl.BlockSpec((1, m, hw), comp1),
                      pl.BlockSpec((1, m, hw), lambda bi, h: (bi, 0, h))]
        key_args += [k, k, v]
    return pl.pallas_call(
        functools.partial(_attn_kernel, tq=tq, tkc=tkc, lam_init=lam_init),
        grid=(b, heads),
        in_specs=[pl.BlockSpec(lam_params.shape, lambda bi, h: (0, 0)),
                  pl.BlockSpec((1, n, hw), comp0), pl.BlockSpec((1, n, hw), comp1), *key_specs,
                  pl.BlockSpec((1, hw), lambda bi, h: (0, 0))],
        out_specs=pl.BlockSpec((1, n, hw), lambda bi, h: (bi, 0, h)),
        out_shape=jax.ShapeDtypeStruct((b, n, aw), MXU_DTYPE),
        scratch_shapes=[pltpu.VMEM((2, tq, nkeys), F32)] * 2 + [pltpu.VMEM((2, tq, LANES), F32)] * 2,
        compiler_params=_cparams("parallel", "parallel"), name=f"attn_{n}",
    )(lam_params, q, q, *key_args, subln_g)


def _out_kernel(o_ref, sg_ref, w_ref, x_ref, g1_ref, ng_ref, sh_ref, sc_ref, wr_ref, xo_ref, hm_ref, aff_ref, *, aw):
    y = _dot(o_ref[0], w_ref[:aw, :]) + _dot(sg_ref[0], w_ref[aw:, :])
    x = x_ref[0] + g1_ref[0] * y
    xo_ref[0] = x
    h = x * lax.rsqrt(jnp.mean(x * x, axis=-1, keepdims=True) + EPS) * ng_ref[...]
    h = h * (1.0 + sc_ref[0]) + sh_ref[0]
    hm_ref[0] = h.astype(hm_ref.dtype)
    logits = _dot_nt(wr_ref[...], h)
    e = jnp.exp(logits - jnp.max(logits, axis=0, keepdims=True))
    aff_ref[0] = e / jnp.sum(e, axis=0, keepdims=True)


def _outproj(o, sg, w_out, x, g1, ng, sh, sc, wr_t):
    b, n, d = x.shape
    aw = o.shape[2]
    sw = sg.shape[2]
    ne = wr_t.shape[0]
    tm = _tile(n, 512)
    tok = lambda bi, i: (bi, i, 0)
    per_b = lambda bi, i: (bi, 0, 0)
    const2 = lambda bi, i: (0, 0)
    return pl.pallas_call(
        functools.partial(_out_kernel, aw=aw),
        grid=(b, n // tm),
        in_specs=[pl.BlockSpec((1, tm, aw), tok), pl.BlockSpec((1, tm, sw), tok),
                  pl.BlockSpec((aw + sw, d), const2), pl.BlockSpec((1, tm, d), tok),
                  pl.BlockSpec((1, 1, d), per_b), pl.BlockSpec((1, d), const2),
                  pl.BlockSpec((1, 1, d), per_b), pl.BlockSpec((1, 1, d), per_b),
                  pl.BlockSpec((ne, d), const2)],
        out_specs=[pl.BlockSpec((1, tm, d), tok), pl.BlockSpec((1, tm, d), tok),
                   pl.BlockSpec((1, ne, tm), lambda bi, i: (bi, 0, i))],
        out_shape=[jax.ShapeDtypeStruct((b, n, d), F32), jax.ShapeDtypeStruct((b, n, d), MXU_DTYPE),
                   jax.ShapeDtypeStruct((b, ne, n), F32)],
        compiler_params=_cparams("parallel", "parallel"), name=f"outproj_{n}",
    )(o, sg, w_out, x, g1, ng, sh, sc, wr_t)


def _route_kernel(aff_ref, tri_ref, pos_ref, start_ref, *, cap, tt):
    a = aff_ref[0]
    ne, n = a.shape
    bits = lax.bitcast_convert_type(a, jnp.int32)
    t = jnp.zeros((ne, 1), jnp.int32)
    for bit in range(30, -1, -1):
        cand = t | (1 << bit)
        cnt = jnp.sum((bits >= cand).astype(jnp.int32), axis=1, keepdims=True)
        t = jnp.where(cnt >= cap, cand, t)
    gt = bits > t
    eq = bits == t
    need = cap - jnp.sum(gt.astype(jnp.int32), axis=1, keepdims=True)

    def exclusive_count(mask):
        out = []
        runs = []
        run = jnp.zeros((ne, 1), F32)
        for ci in range(n // LANES):
            runs.append(run)
            m = mask[:, ci * LANES:(ci + 1) * LANES].astype(F32)
            incl = _dot(m, tri_ref[...])
            out.append(run + incl - m)
            run = run + incl[:, LANES - 1:LANES]
        return jnp.concatenate(out, axis=1), runs

    sel = gt | (eq & (exclusive_count(eq)[0] < need.astype(F32)))
    pos, runs = exclusive_count(sel)
    pos_ref[0] = jnp.where(sel, pos, -1.0)
    lane = lax.broadcasted_iota(jnp.int32, (ne, LANES), 1)
    first = jnp.full((ne, LANES), float(cap), F32)
    for j in range(n // tt):
        first = jnp.where(lane == j, runs[j * tt // LANES], first)
    start_ref[0] = first


def _route(aff_t, tri, *, cap, tt):
    b, ne, n = aff_t.shape
    return pl.pallas_call(
        functools.partial(_route_kernel, cap=cap, tt=tt),
        grid=(b,),
        in_specs=[pl.BlockSpec((1, ne, n), lambda bi: (bi, 0, 0)), pl.BlockSpec((LANES, LANES), lambda bi: (0, 0))],
        out_specs=[pl.BlockSpec((1, ne, n), lambda bi: (bi, 0, 0)), pl.BlockSpec((1, ne, LANES), lambda bi: (bi, 0, 0))],
        out_shape=[jax.ShapeDtypeStruct((b, ne, n), F32), jax.ShapeDtypeStruct((b, ne, LANES), F32)],
        compiler_params=_cparams("parallel"), name=f"route_{n}",
    )(aff_t, tri)


def _window_hits(a_ref, pos_ref, *, win):
    bi, t, nt = pl.program_id(0), pl.program_id(1), pl.num_programs(1)
    ne, tt = pos_ref.shape[1], pos_ref.shape[2]
    base = lax.broadcasted_iota(jnp.int32, (win, tt), 0)
    offs, hits = [], []
    for e in range(ne):
        a_e = a_ref[(bi * ne + e) * nt + t]
        offs.append(a_e)
        hits.append(pos_ref[0, e:e + 1, :] == (base + a_e).astype(F32))
    return offs, hits


def _gather_win_kernel(a_ref, pos_ref, aff_ref, hm_ref, xs_ref, gs_ref, *, cap, win):
    @pl.when(pl.program_id(1) == 0)
    def _():
        xs_ref[...] = jnp.zeros_like(xs_ref)
        gs_ref[...] = jnp.zeros_like(gs_ref)

    ne = pos_ref.shape[1]
    offs, hits = _window_hits(a_ref, pos_ref, win=win)
    group = 8
    for e0 in range(0, ne, group):
        onehot = jnp.concatenate([h.astype(MXU_DTYPE) for h in hits[e0:e0 + group]], axis=0)
        picked = _dot(onehot, hm_ref[0]).astype(xs_ref.dtype)
        for e in range(e0, min(e0 + group, ne)):
            rows = pl.ds(pl.multiple_of(e * cap + offs[e], 16), win)
            gate = jnp.sum(jnp.where(hits[e], aff_ref[0, e:e + 1, :], 0.0), axis=1, keepdims=True)
            xs_ref[0, rows, :] += picked[(e - e0) * win:(e - e0 + 1) * win]
            gs_ref[0, rows, :] += jnp.broadcast_to(gate, (win, LANES))


def _gather_win(a_flat, pos, aff_t, hm, *, cap, win, tt):
    b, ne, n = pos.shape
    d = hm.shape[2]
    tile = lambda bi, t, a: (bi, 0, t)
    whole = lambda bi, t, a: (bi, 0, 0)
    return pl.pallas_call(
        functools.partial(_gather_win_kernel, cap=cap, win=win),
        grid_spec=pltpu.PrefetchScalarGridSpec(
            num_scalar_prefetch=1, grid=(b, n // tt),
            in_specs=[pl.BlockSpec((1, ne, tt), tile), pl.BlockSpec((1, ne, tt), tile),
                      pl.BlockSpec((1, tt, d), lambda bi, t, a: (bi, t, 0))],
            out_specs=[pl.BlockSpec((1, ne * cap, d), whole), pl.BlockSpec((1, ne * cap, LANES), whole)]),
        out_shape=[jax.ShapeDtypeStruct((b, ne * cap, d), MXU_DTYPE),
                   jax.ShapeDtypeStruct((b, ne * cap, LANES), F32)],
        compiler_params=_cparams("parallel", "arbitrary"), name=f"gather_win_{n}",
    )(a_flat, pos, aff_t, hm)


def _gather_kernel(pos_ref, aff_ref, hm_ref, xs_ref, gs_ref, *, cap, tn):
    n = hm_ref.shape[1]
    d = hm_ref.shape[2]
    slot = lax.broadcasted_iota(jnp.int32, (cap, tn), 0).astype(F32)
    xs = jnp.zeros((cap, d), F32)
    gs = jnp.zeros((cap, 1), F32)
    for ci in range(n // tn):
        cols = slice(ci * tn, (ci + 1) * tn)
        hit = pos_ref[0, 0, :, cols] == slot
        xs = xs + _dot(hit.astype(MXU_DTYPE), hm_ref[0, cols, :])
        gs = gs + jnp.sum(jnp.where(hit, aff_ref[0, 0, :, cols], 0.0), axis=1, keepdims=True)
    xs_ref[0] = xs.astype(xs_ref.dtype)
    gs_ref[0] = jnp.broadcast_to(gs, (cap, LANES))


def _gather(pos, aff_t, hm, *, cap):
    b, ne, n = pos.shape
    d = hm.shape[2]
    tn = _tile(n, 512)
    pos4 = pos.reshape(b, ne, 1, n)
    aff4 = aff_t.reshape(b, ne, 1, n)
    return pl.pallas_call(
        functools.partial(_gather_kernel, cap=cap, tn=tn),
        grid=(b, ne),
        in_specs=[pl.BlockSpec((1, 1, 1, n), lambda bi, e: (bi, e, 0, 0)),
                  pl.BlockSpec((1, 1, 1, n), lambda bi, e: (bi, e, 0, 0)),
                  pl.BlockSpec((1, n, d), lambda bi, e: (bi, 0, 0))],
        out_specs=[pl.BlockSpec((1, cap, d), lambda bi, e: (bi, e, 0)),
                   pl.BlockSpec((1, cap, LANES), lambda bi, e: (bi, e, 0))],
        out_shape=[jax.ShapeDtypeStruct((b, ne * cap, d), MXU_DTYPE),
                   jax.ShapeDtypeStruct((b, ne * cap, LANES), F32)],
        compiler_params=_cparams("parallel", "parallel"), name=f"gather_{n}",
    )(pos4, aff4, hm)


def _ffn_kernel(x_ref, wg_ref, wu_ref, wd_ref, gs_ref, y_ref, acc_ref, *, nf):
    f = pl.program_id(2)
    tm, d = acc_ref.shape

    def partial_out():
        x = x_ref[...].reshape(tm, d)
        hid = _silu(_dot(x, wg_ref[0, 0])) * _dot(x, wu_ref[0, 0])
        return _dot(hid, wd_ref[0, 0])

    def finish(total):
        y_ref[0] = total * gs_ref[...].reshape(tm, LANES)[:, :1]

    if nf == 1:
        finish(partial_out())
        return

    @pl.when(f == 0)
    def _():
        acc_ref[...] = partial_out()

    if nf > 2:
        @pl.when((f > 0) & (f < nf - 1))
        def _():
            acc_ref[...] += partial_out()

    @pl.when(f == nf - 1)
    def _():
        finish(acc_ref[...] + partial_out())


def _ffn(xs, w_gate, w_up, w_down, gs, *, layer, cap):
    b, rows, d = xs.shape
    _, ne, _, ff = w_gate.shape
    bt = b
    while bt * cap > 1024 and bt % 2 == 0:
        bt //= 2
    tm = bt * cap
    tf = _tile(ff, 1024)
    return pl.pallas_call(
        functools.partial(_ffn_kernel, nf=ff // tf),
        grid=(ne, b // bt, ff // tf),
        in_specs=[pl.BlockSpec((bt, 1, cap, d), lambda e, i, f: (i, e, 0, 0)),
                  pl.BlockSpec((1, 1, d, tf), lambda e, i, f: (layer, e, 0, f)),
                  pl.BlockSpec((1, 1, d, tf), lambda e, i, f: (layer, e, 0, f)),
                  pl.BlockSpec((1, 1, tf, d), lambda e, i, f: (layer, e, f, 0)),
                  pl.BlockSpec((bt, 1, cap, LANES), lambda e, i, f: (i, e, 0, 0))],
        out_specs=pl.BlockSpec((1, tm, d), lambda e, i, f: (e, i, 0)),
        out_shape=jax.ShapeDtypeStruct((ne, b * cap, d), F32),
        scratch_shapes=[pltpu.VMEM((tm, d), F32)],
        compiler_params=_cparams("parallel", "parallel", "arbitrary"), name=f"ffn_{b * cap}",
    )(xs.reshape(b, ne, cap, d), w_gate, w_up, w_down, gs.reshape(b, ne, cap, LANES))


def _combine_win_kernel(a_ref, pos_ref, *refs, win):
    y_refs, o_ref = refs[:-1], refs[-1]
    d = o_ref.shape[2]
    _, hits = _window_hits(a_ref, pos_ref, win=win)
    onehot = jnp.concatenate([h.astype(MXU_DTYPE) for h in hits], axis=0)
    parts = [_split_hi_lo(y_ref[0]) for y_ref in y_refs]
    y_all = jnp.concatenate([jnp.concatenate([p[0] for p in parts], axis=0),
                             jnp.concatenate([p[1] for p in parts], axis=0)], axis=1)
    r = _dot_tn(onehot, y_all)
    o_ref[0] = r[:, :d] + r[:, d:]


def _combine_win(a_flat, pos, yg, *, cap, win, tt):
    b, ne, n = pos.shape
    d = yg.shape[2]
    nt = n // tt

    def window(e):
        return pl.BlockSpec((pl.Element(1), pl.Element(win), pl.Element(d)),
                            lambda bi, t, a: (e, pl.multiple_of(bi * cap + a[(bi * ne + e) * nt + t], 16), 0))

    return pl.pallas_call(
        functools.partial(_combine_win_kernel, win=win),
        grid_spec=pltpu.PrefetchScalarGridSpec(
            num_scalar_prefetch=1, grid=(b, nt),
            in_specs=[pl.BlockSpec((1, ne, tt), lambda bi, t, a: (bi, 0, t))] + [window(e) for e in range(ne)],
            out_specs=pl.BlockSpec((1, tt, d), lambda bi, t, a: (bi, t, 0))),
        out_shape=jax.ShapeDtypeStruct((b, n, d), F32),
        compiler_params=_cparams("parallel", "parallel"), name=f"combine_win_{n}",
    )(a_flat, pos, *([yg] * ne))


def _combine_kernel(pos_ref, y_ref, o_ref, *, cap, tn):
    e = pl.program_id(1)

    @pl.when(e == 0)
    def _():
        o_ref[...] = jnp.zeros_like(o_ref)

    n = o_ref.shape[1]
    y_hi, y_lo = _split_hi_lo(y_ref[0])
    slot = lax.broadcasted_iota(jnp.int32, (cap, tn), 0).astype(F32)
    for ci in range(n // tn):
        cols = slice(ci * tn, (ci + 1) * tn)
        hit = (pos_ref[0, 0, :, cols] == slot).astype(MXU_DTYPE)
        o_ref[0, cols, :] += _dot_tn(hit, y_hi) + _dot_tn(hit, y_lo)


def _combine(pos, yg, *, cap):
    b, ne, n = pos.shape
    d = yg.shape[2]
    tn = _tile(n, 512)
    return pl.pallas_call(
        functools.partial(_combine_kernel, cap=cap, tn=tn),
        grid=(b, ne),
        in_specs=[pl.BlockSpec((1, 1, 1, n), lambda bi, e: (bi, e, 0, 0)),
                  pl.BlockSpec((1, cap, d), lambda bi, e: (e, bi, 0))],
        out_specs=pl.BlockSpec((1, n, d), lambda bi, e: (bi, 0, 0)),
        out_shape=jax.ShapeDtypeStruct((b, n, d), F32),
        compiler_params=_cparams("parallel", "arbitrary"), name=f"combine_{n}",
    )(pos.reshape(b, ne, 1, n), yg)


def _final_kernel(x_ref, moe_ref, g2_ref, ng_ref, o_ref):
    x = x_ref[0] + g2_ref[0] * moe_ref[0]
    o_ref[0] = x * lax.rsqrt(jnp.mean(x * x, axis=-1, keepdims=True) + EPS) * ng_ref[...]


def _final(x, moe, g2, ng):
    b, n, d = x.shape
    tm = _tile(n, 512)
    tok = lambda bi, i: (bi, i, 0)
    return pl.pallas_call(
        _final_kernel,
        grid=(b, n // tm),
        in_specs=[pl.BlockSpec((1, tm, d), tok), pl.BlockSpec((1, tm, d), tok),
                  pl.BlockSpec((1, 1, d), lambda bi, i: (bi, 0, 0)), pl.BlockSpec((1, d), lambda bi, i: (0, 0))],
        out_specs=pl.BlockSpec((1, tm, d), tok),
        out_shape=jax.ShapeDtypeStruct((b, n, d), F32),
        compiler_params=_cparams("parallel", "parallel"), name="final",
    )(x, moe, g2, ng)


def _rope_tables(n, heads):
    rows = n // GRID_W
    row = jnp.repeat(jnp.arange(rows), GRID_W).astype(F32)
    col = jnp.tile(jnp.arange(GRID_W), rows).astype(F32)
    n_freq = HEAD_DIM // 4
    freqs = ROPE_THETA ** (-jnp.arange(n_freq, dtype=F32) / n_freq)
    ang_r = row[:, None] * freqs
    ang_c = col[:, None] * freqs
    ang = jnp.concatenate([ang_r, ang_r, ang_c, ang_c], axis=-1)
    sign = jnp.where((jnp.arange(HEAD_DIM) % 32) < 16, -1.0, 1.0).astype(F32)
    return jnp.tile(jnp.cos(ang), (1, 2 * heads)), jnp.tile(jnp.sin(ang) * sign, (1, 2 * heads))


def _moe(hm, aff_t, tri, w_gate, w_up, w_down, layer):
    b, n, _ = hm.shape
    ne = aff_t.shape[1]
    cap = CAPACITY_FACTOR * n // ne
    tt = _tile(n, 512)
    nt = n // tt
    win = min(SLOT_WINDOW, cap)
    pos, first = _route(aff_t, tri, cap=cap, tt=tt)
    first = first[:, :, :nt + 1].astype(jnp.int32)
    offs = jnp.minimum(first[:, :, :nt] // 16 * 16, cap - win)
    fits = jnp.all(first[:, :, 1:] <= offs + win)
    a_flat = offs.reshape(-1)
    xs, gs = lax.cond(fits,
                      lambda: _gather_win(a_flat, pos, aff_t, hm, cap=cap, win=win, tt=tt),
                      lambda: _gather(pos, aff_t, hm, cap=cap))
    yg = _ffn(xs, w_gate, w_up, w_down, gs, layer=layer, cap=cap)
    return lax.cond(fits,
                    lambda: _combine_win(a_flat, pos, yg, cap=cap, win=win, tt=tt),
                    lambda: _combine(pos, yg, cap=cap))


def kernel(x, c, ctx, c_ctx, w_ada, b_ada, norm1_g, norm2_g, w_in, w_out, lambda_q1, lambda_k1, lambda_q2, lambda_k2, subln_g, sgu_norm_g, sgu_w, sgu_b, w_router, w_gate, w_up, w_down, norm_f_g):
    b, n, d = x.shape
    m = ctx.shape[1]
    depth = w_in.shape[0]
    sw = sgu_norm_g.shape[1]
    aw = (w_in.shape[2] - 2 * sw) // 3
    heads = aw // (2 * HEAD_DIM)
    bf = lambda t: t.astype(MXU_DTYPE)

    rows = -(-(b + 1) // 8) * 8
    cc = jnp.concatenate([c, c_ctx[None, :], jnp.zeros((rows - b - 1, d), F32)], axis=0)
    mod = _ada(cc, w_ada, b_ada)

    w_in_all = bf(w_in)
    cos, sin = _rope_tables(n, heads)
    cos_c = jnp.ones((m, aw), F32)
    sin_c = jnp.zeros((m, aw), F32)
    hw_s = sw // SGU_HEADS
    ones_blk = bf(jnp.kron(jnp.eye(SGU_HEADS, dtype=F32), jnp.full((hw_s, hw_s), 1.0 / hw_s, F32)))
    tri = bf(jnp.triu(jnp.ones((LANES, LANES), F32)))

    cx = ctx
    res_x = res_c = None
    for l in range(depth):
        last = l == depth - 1
        lam_init = 0.8 - 0.6 * math.exp(-0.3 * l)
        parts = [mod[l, :, i * d:(i + 1) * d] for i in range(6)]
        sh1, sc1, g1, sh2, sc2, g2 = [t[:b, None, :] for t in parts]
        csh1, csc1, cg1, csh2, csc2, cg2 = [jnp.broadcast_to(t[b][None, None, :], (b, 1, d)) for t in parts]

        w_in_p = w_in_all[l]
        w_out_l = bf(w_out[l])
        wr_t = bf(w_router[l].T)
        ng1, ng2 = norm1_g[l][None, :], norm2_g[l][None, :]
        sng = sgu_norm_g[l][None, :]
        ws = bf(sgu_w[l].reshape(SGU_HEADS * CHUNK, CHUNK))
        bs = jnp.repeat(sgu_b[l].T, hw_s, axis=1)
        lam_params = jnp.zeros((8, LANES), F32).at[:4, :HEAD_DIM].set(
            jnp.stack([lambda_q1[l], lambda_k1[l], lambda_q2[l], lambda_k2[l]]))
        sub_g = subln_g[l][None, :]
        shared = dict(aw=aw, sw=sw)

        x, (q, k, v, sg) = _proj(x, res_x, ng1, sh1, sc1, w_in_p, cos, sin, sng, ws, bs, ones_blk, rope=True, **shared)
        cx, (qc, kc, vc, sgc) = _proj(cx, res_c, ng1, csh1, csc1, w_in_p, cos_c, sin_c, sng, ws, bs, ones_blk,
                                      rope=False, **shared)
        o = _attn(lam_params, q, [(kc, vc), (k, v)], sub_g, lam_init=lam_init)
        x, hm, aff_t = _outproj(o, sg, w_out_l, x, g1, ng2, sh2, sc2, wr_t)
        res_x = (_moe(hm, aff_t, tri, w_gate, w_up, w_down, l), g2)
        if not last:
            oc = _attn(lam_params, qc, [(kc, vc)], sub_g, lam_init=lam_init)
            cx, hmc, aff_c = _outproj(oc, sgc, w_out_l, cx, cg1, ng2, csh2, csc2, wr_t)
            res_c = (_moe(hmc, aff_c, tri, w_gate, w_up, w_down, l), cg2)
    return _final(x, res_x[0], res_x[1], norm_f_g[None, :])
```

```python
import functools
import math

import jax
import jax.numpy as jnp
from jax import lax
from jax.experimental import pallas as pl
from jax.experimental.pallas import tpu as pltpu

F32 = jnp.float32
MXU_DTYPE = jnp.bfloat16

EPS = 1e-6
GRID_W = 64
ROPE_THETA = 10000.0
HEAD_DIM = 64
SGU_HEADS = 8
CHUNK = 128
ROPE_GROUP = HEAD_DIM // 2
CAPACITY_FACTOR = 2
LANES = 128
PACKED_ROWS = 16
TOKEN_TILE = 512
ATTN_KEY_CHUNK = 256
ADA_TILE = 1536
FFN_ROWS = 1024
FFN_HIDDEN_TILE = 1024
SLOT_WINDOW = 128
VMEM_LIMIT = 56 * 1024 * 1024


def _cparams(*sem):
    return pltpu.CompilerParams(dimension_semantics=sem, vmem_limit_bytes=VMEM_LIMIT)


def _dot(a, b):
    return jnp.dot(a.astype(MXU_DTYPE), b.astype(MXU_DTYPE), preferred_element_type=F32)


def _dot_nt(a, b):
    return lax.dot_general(a.astype(MXU_DTYPE), b.astype(MXU_DTYPE), (((1,), (1,)), ((), ())),
                           preferred_element_type=F32)


def _dot_tn(a, b):
    return lax.dot_general(a.astype(MXU_DTYPE), b.astype(MXU_DTYPE), (((0,), (0,)), ((), ())),
                           preferred_element_type=F32)


def _split_hi_lo(x):
    hi = x.astype(MXU_DTYPE)
    lo = (x - hi.astype(F32)).astype(MXU_DTYPE)
    return hi, lo


def _silu(x):
    return x * jax.nn.sigmoid(x)


def _gelu(x):
    return 0.5 * x * (1.0 + lax.erf(x * (2.0 ** -0.5)))


def _tile(n, pref):
    t = pref
    while t > 8 and n % t:
        t //= 2
    return t if n % t == 0 else n


def _ada_kernel(c_ref, w_ref, b_ref, o_ref):
    o_ref[0] = _dot(_silu(c_ref[...]), w_ref[0]) + b_ref[0]


def _ada(cc, w_ada, b_ada):
    depth, d, n6 = w_ada.shape
    rows = cc.shape[0]
    tn = _tile(n6, ADA_TILE)
    return pl.pallas_call(
        _ada_kernel,
        grid=(depth, n6 // tn),
        in_specs=[pl.BlockSpec((rows, d), lambda l, j: (0, 0)),
                  pl.BlockSpec((1, d, tn), lambda l, j: (l, 0, j)),
                  pl.BlockSpec((1, 1, tn), lambda l, j: (l, 0, j))],
        out_specs=pl.BlockSpec((1, rows, tn), lambda l, j: (l, 0, j)),
        out_shape=jax.ShapeDtypeStruct((depth, rows, n6), F32),
        compiler_params=_cparams("parallel", "parallel"), name="ada",
    )(cc, w_ada, b_ada.reshape(depth, 1, n6))


def _proj_kernel(*refs, has_res, rope, aw, sw):
    if has_res:
        x_ref, moe_ref, g2_ref, *refs = refs
    else:
        x_ref, *refs = refs
    (ng_ref, sh_ref, sc_ref, w_ref, cos_ref, sin_ref, sng_ref, ws_ref, bs_ref, ones_ref, *outs) = refs
    if has_res:
        xo_ref, q_ref, k_ref, v_ref, sg_ref = outs
    else:
        q_ref, k_ref, v_ref, sg_ref = outs

    x = x_ref[0]
    if has_res:
        x = x + g2_ref[0] * moe_ref[0]
        xo_ref[0] = x
    h = x * lax.rsqrt(jnp.mean(x * x, axis=-1, keepdims=True) + EPS) * ng_ref[...]
    h = h * (1.0 + sc_ref[0]) + sh_ref[0]
    p = _dot(h, w_ref[...])

    q = p[:, :aw]
    k = p[:, aw:2 * aw]
    if rope:
        half = ROPE_GROUP // 2
        first = (lax.broadcasted_iota(jnp.int32, q.shape, 1) % ROPE_GROUP) < half
        cos = cos_ref[...]
        sin = sin_ref[...]

        def rot(t):
            return jnp.where(first, pltpu.roll(t, aw - half, 1), pltpu.roll(t, half, 1))

        q = q * cos + rot(q) * sin
        k = k * cos + rot(k) * sin
    q_ref[0] = (q * (HEAD_DIM ** -0.5 * math.log2(math.e))).astype(q_ref.dtype)
    k_ref[0] = k.astype(k_ref.dtype)
    v_ref[0] = p[:, 2 * aw:3 * aw].astype(v_ref.dtype)

    u = _gelu(p[:, 3 * aw:3 * aw + sw])
    gv = _gelu(p[:, 3 * aw + sw:])
    sq_hi, sq_lo = _split_hi_lo(gv * gv)
    ms = _dot(sq_hi, ones_ref[...]) + _dot(sq_lo, ones_ref[...])
    gvn = (gv * lax.rsqrt(ms + EPS) * sng_ref[...]).astype(MXU_DTYPE)
    head_of_lane = lax.broadcasted_iota(jnp.int32, (CHUNK, sw), 1) // (sw // SGU_HEADS)
    tm = x.shape[0]
    for ci in range(tm // CHUNK):
        rows = slice(ci * CHUNK, (ci + 1) * CHUNK)
        r = _dot(ws_ref[...], gvn[rows])
        s = bs_ref[...]
        for hh in range(SGU_HEADS):
            s = s + jnp.where(head_of_lane == hh, r[hh * CHUNK:(hh + 1) * CHUNK], 0.0)
        sg_ref[0, rows, :] = (u[rows] * s).astype(sg_ref.dtype)


def _proj(x, res, ng, sh, sc, w_in_p, cos, sin, sng, ws, bs, ones_blk, *, rope, aw, sw):
    b, n, d = x.shape
    pw = w_in_p.shape[1]
    tm = _tile(n, TOKEN_TILE)
    has_res = res is not None
    tok = lambda bi, i: (bi, i, 0)
    per_b = lambda bi, i: (bi, 0, 0)
    const2 = lambda bi, i: (0, 0)
    in_specs = [pl.BlockSpec((1, tm, d), tok)]
    args = [x]
    if has_res:
        moe, g2 = res
        in_specs += [pl.BlockSpec((1, tm, d), tok), pl.BlockSpec((1, 1, d), per_b)]
        args += [moe, g2]
    in_specs += [pl.BlockSpec((1, d), const2), pl.BlockSpec((1, 1, d), per_b), pl.BlockSpec((1, 1, d), per_b),
                 pl.BlockSpec((d, pw), const2),
                 pl.BlockSpec((tm, aw), lambda bi, i: (i, 0)), pl.BlockSpec((tm, aw), lambda bi, i: (i, 0)),
                 pl.BlockSpec((1, sw), const2), pl.BlockSpec((SGU_HEADS * CHUNK, CHUNK), const2),
                 pl.BlockSpec((CHUNK, sw), const2), pl.BlockSpec((sw, sw), const2)]
    args += [ng, sh, sc, w_in_p, cos, sin, sng, ws, bs, ones_blk]
    out_specs = [pl.BlockSpec((1, tm, aw), tok)] * 3 + [pl.BlockSpec((1, tm, sw), tok)]
    out_shape = [jax.ShapeDtypeStruct((b, n, aw), MXU_DTYPE)] * 3 + [jax.ShapeDtypeStruct((b, n, sw), MXU_DTYPE)]
    if has_res:
        out_specs = [pl.BlockSpec((1, tm, d), tok)] + out_specs
        out_shape = [jax.ShapeDtypeStruct((b, n, d), F32)] + out_shape
    outs = pl.pallas_call(
        functools.partial(_proj_kernel, has_res=has_res, rope=rope, aw=aw, sw=sw),
        grid=(b, n // tm),
        in_specs=in_specs, out_specs=out_specs, out_shape=out_shape,
        compiler_params=_cparams("parallel", "parallel"), name="proj_lat" if rope else "proj_ctx",
    )(*args)
    if has_res:
        return outs[0], outs[1:]
    return x, outs


def _attn_kernel(lam_ref, q0_ref, q1_ref, *refs, tq, tkc, lam_init):
    *key_refs, g_ref, o_ref, s0_ref, s1_ref, m0_ref, m1_ref = refs
    sources = [key_refs[i:i + 3] for i in range(0, len(key_refs), 3)]
    chunks, col = [], 0
    for k0_ref, k1_ref, v_ref in sources:
        for r0 in range(0, v_ref.shape[1], tkc):
            chunks.append(((k0_ref, k1_ref), v_ref, r0, col))
            col += tkc
    q_ref = (q0_ref, q1_ref)
    s_ref, m_ref = (s0_ref, s1_ref), (m0_ref, m1_ref)
    hw = q0_ref.shape[2]
    nt = q0_ref.shape[1] // tq
    lane = lax.broadcasted_iota(jnp.int32, (tq, hw), 1)
    mine = (lane >= HEAD_DIM) == (pl.program_id(1) % 2 == 1)
    lp = lam_ref[...]
    lam = (jnp.exp(jnp.sum(lp[0:1] * lp[1:2], axis=-1, keepdims=True))
           - jnp.exp(jnp.sum(lp[2:3] * lp[3:4], axis=-1, keepdims=True)) + lam_init)

    def rows(t):
        return pl.ds(t * tq if isinstance(t, int) else pl.multiple_of(t * tq, tq), tq)

    def phase(t, slot, *, do_scores=True, do_values=True):
        other = 1 - slot
        if do_scores:
            qc = []
            for c in range(2):
                q = q_ref[c][0, rows(t + 1), :]
                qc.append(jnp.where(mine, q, jnp.zeros_like(q)))
            run = [None, None]
        if do_values:
            top = [jnp.max(m_ref[slot][c], axis=-1, keepdims=True) for c in range(2)]
            acc = [jnp.zeros((tq, 2 * hw), F32), jnp.zeros((tq, 2 * hw), F32)]
        for k_ref, v_ref, r0, col in chunks:
            cols = slice(col, col + tkc)
            if do_values:
                vals = v_ref[0, r0:r0 + tkc, :]
                vals1 = jnp.concatenate([vals, jnp.ones_like(vals)], axis=1)
            for c in range(2):
                if do_scores:
                    s = _dot_nt(qc[c], k_ref[c][0, r0:r0 + tkc, :])
                    s_ref[other][c, :, cols] = s
                    for j in range(tkc // LANES):
                        part = s[:, j * LANES:(j + 1) * LANES]
                        run[c] = part if run[c] is None else jnp.maximum(run[c], part)
                if do_values:
                    p = jnp.exp2(s_ref[slot][c, :, cols] - top[c])
                    acc[c] = acc[c] + _dot(p, vals1)
        if do_scores:
            for c in range(2):
                m_ref[other][c] = run[c]
        if do_values:
            o = acc[0][:, :hw] / acc[0][:, hw:] - lam * (acc[1][:, :hw] / acc[1][:, hw:])
            o = o * lax.rsqrt(jnp.mean(o * o, axis=-1, keepdims=True) + EPS) * g_ref[...] * (1.0 - lam_init)
            o_ref[0, rows(t), :] = o.astype(o_ref.dtype)

    phase(-1, 1, do_values=False)

    def step(t, carry):
        for slot in range(2):
            @pl.when(t % 2 == slot)
            def _():
                phase(t, slot)
        return carry

    lax.fori_loop(0, nt - 1, step, 0)
    phase(nt - 1, (nt - 1) % 2, do_scores=False)


def _attn(lam_params, q, kv_sources, subln_g, *, lam_init):
    b, n, aw = q.shape
    lens = [k.shape[1] for k, _ in kv_sources]
    nkeys = sum(lens)
    hw = 2 * HEAD_DIM
    heads = aw // hw
    pairs = heads // 2
    tq = _tile(n, TOKEN_TILE)
    tkc = min(_tile(m, ATTN_KEY_CHUNK) for m in lens)
    comp0 = lambda bi, h: (bi, 0, h // 2)
    comp1 = lambda bi, h: (bi, 0, pairs + h // 2)
    key_specs, key_args = [], []
    for (k, v), m in zip(kv_sources, lens):
        key_specs += [pl.BlockSpec((1, m, hw), comp0), pl.BlockSpec((1, m, hw), comp1),
                      pl.BlockSpec((1, m, hw), lambda bi, h: (bi, 0, h))]
        key_args += [k, k, v]
    return pl.pallas_call(
        functools.partial(_attn_kernel, tq=tq, tkc=tkc, lam_init=lam_init),
        grid=(b, heads),
        in_specs=[pl.BlockSpec(lam_params.shape, lambda bi, h: (0, 0)),
                  pl.BlockSpec((1, n, hw), comp0), pl.BlockSpec((1, n, hw), comp1), *key_specs,
                  pl.BlockSpec((1, hw), lambda bi, h: (0, 0))],
        out_specs=pl.BlockSpec((1, n, hw), lambda bi, h: (bi, 0, h)),
        out_shape=jax.ShapeDtypeStruct((b, n, aw), MXU_DTYPE),
        scratch_shapes=[pltpu.VMEM((2, tq, nkeys), F32)] * 2 + [pltpu.VMEM((2, tq, LANES), F32)] * 2,
        compiler_params=_cparams("parallel", "parallel"), name=f"attn_{n}",
    )(lam_params, q, q, *key_args, subln_g)


def _out_kernel(o_ref, sg_ref, w_ref, x_ref, g1_ref, ng_ref, sh_ref, sc_ref, wr_ref, xo_ref, hm_ref, aff_ref, *, aw):
    y = _dot(o_ref[0], w_ref[:aw, :]) + _dot(sg_ref[0], w_ref[aw:, :])
    x = x_ref[0] + g1_ref[0] * y
    xo_ref[0] = x
    h = x * lax.rsqrt(jnp.mean(x * x, axis=-1, keepdims=True) + EPS) * ng_ref[...]
    h = h * (1.0 + sc_ref[0]) + sh_ref[0]
    hm_ref[0] = h.astype(hm_ref.dtype)
    logits = _dot_nt(wr_ref[...], h)
    e = jnp.exp(logits - jnp.max(logits, axis=0, keepdims=True))
    aff_ref[0] = e / jnp.sum(e, axis=0, keepdims=True)


def _outproj(o, sg, w_out, x, g1, ng, sh, sc, wr_t):
    b, n, d = x.shape
    aw = o.shape[2]
    sw = sg.shape[2]
    ne = wr_t.shape[0]
    tm = _tile(n, TOKEN_TILE)
    tok = lambda bi, i: (bi, i, 0)
    per_b = lambda bi, i: (bi, 0, 0)
    const2 = lambda bi, i: (0, 0)
    return pl.pallas_call(
        functools.partial(_out_kernel, aw=aw),
        grid=(b, n // tm),
        in_specs=[pl.BlockSpec((1, tm, aw), tok), pl.BlockSpec((1, tm, sw), tok),
                  pl.BlockSpec((aw + sw, d), const2), pl.BlockSpec((1, tm, d), tok),
                  pl.BlockSpec((1, 1, d), per_b), pl.BlockSpec((1, d), const2),
                  pl.BlockSpec((1, 1, d), per_b), pl.BlockSpec((1, 1, d), per_b),
                  pl.BlockSpec((ne, d), const2)],
        out_specs=[pl.BlockSpec((1, tm, d), tok), pl.BlockSpec((1, tm, d), tok),
                   pl.BlockSpec((1, ne, tm), lambda bi, i: (bi, 0, i))],
        out_shape=[jax.ShapeDtypeStruct((b, n, d), F32), jax.ShapeDtypeStruct((b, n, d), MXU_DTYPE),
                   jax.ShapeDtypeStruct((b, ne, n), F32)],
        compiler_params=_cparams("parallel", "parallel"), name=f"outproj_{n}",
    )(o, sg, w_out, x, g1, ng, sh, sc, wr_t)


def _route_kernel(aff_ref, tri_ref, pos_ref, start_ref, *, cap, tt):
    a = aff_ref[0]
    ne, n = a.shape
    bits = lax.bitcast_convert_type(a, jnp.int32)
    t = jnp.zeros((ne, 1), jnp.int32)
    for bit in range(30, -1, -1):
        cand = t | (1 << bit)
        cnt = jnp.sum((bits >= cand).astype(jnp.int32), axis=1, keepdims=True)
        t = jnp.where(cnt >= cap, cand, t)
    gt = bits > t
    eq = bits == t
    need = cap - jnp.sum(gt.astype(jnp.int32), axis=1, keepdims=True)

    def exclusive_count(mask):
        out = []
        runs = []
        run = jnp.zeros((ne, 1), F32)
        for ci in range(n // LANES):
            runs.append(run)
            m = mask[:, ci * LANES:(ci + 1) * LANES].astype(F32)
            incl = _dot(m, tri_ref[...])
            out.append(run + incl - m)
            run = run + incl[:, LANES - 1:LANES]
        return jnp.concatenate(out, axis=1), runs

    sel = gt | (eq & (exclusive_count(eq)[0] < need.astype(F32)))
    pos, runs = exclusive_count(sel)
    pos_ref[0] = jnp.where(sel, pos, -1.0)
    lane = lax.broadcasted_iota(jnp.int32, (ne, LANES), 1)
    first = jnp.full((ne, LANES), float(cap), F32)
    for j in range(n // tt):
        first = jnp.where(lane == j, runs[j * tt // LANES], first)
    start_ref[0] = first


def _route(aff_t, tri, *, cap, tt):
    b, ne, n = aff_t.shape
    return pl.pallas_call(
        functools.partial(_route_kernel, cap=cap, tt=tt),
        grid=(b,),
        in_specs=[pl.BlockSpec((1, ne, n), lambda bi: (bi, 0, 0)), pl.BlockSpec((LANES, LANES), lambda bi: (0, 0))],
        out_specs=[pl.BlockSpec((1, ne, n), lambda bi: (bi, 0, 0)), pl.BlockSpec((1, ne, LANES), lambda bi: (bi, 0, 0))],
        out_shape=[jax.ShapeDtypeStruct((b, ne, n), F32), jax.ShapeDtypeStruct((b, ne, LANES), F32)],
        compiler_params=_cparams("parallel"), name=f"route_{n}",
    )(aff_t, tri)


def _window_hits(a_ref, pos_ref, *, win):
    bi, t, nt = pl.program_id(0), pl.program_id(1), pl.num_programs(1)
    ne, tt = pos_ref.shape[1], pos_ref.shape[2]
    base = lax.broadcasted_iota(jnp.int32, (win, tt), 0)
    offs, hits = [], []
    for e in range(ne):
        a_e = a_ref[(bi * ne + e) * nt + t]
        offs.append(a_e)
        hits.append(pos_ref[0, e:e + 1, :] == (base + a_e).astype(F32))
    return offs, hits


def _gather_win_kernel(a_ref, pos_ref, aff_ref, hm_ref, xs_ref, gs_ref, *, cap, win):
    @pl.when(pl.program_id(1) == 0)
    def _():
        xs_ref[...] = jnp.zeros_like(xs_ref)
        gs_ref[...] = jnp.zeros_like(gs_ref)

    ne = pos_ref.shape[1]
    offs, hits = _window_hits(a_ref, pos_ref, win=win)
    group = 8
    for e0 in range(0, ne, group):
        onehot = jnp.concatenate([h.astype(MXU_DTYPE) for h in hits[e0:e0 + group]], axis=0)
        picked = _dot(onehot, hm_ref[0]).astype(xs_ref.dtype)
        for e in range(e0, min(e0 + group, ne)):
            rows = pl.ds(pl.multiple_of(e * cap + offs[e], PACKED_ROWS), win)
            gate = jnp.sum(jnp.where(hits[e], aff_ref[0, e:e + 1, :], 0.0), axis=1, keepdims=True)
            xs_ref[0, rows, :] += picked[(e - e0) * win:(e - e0 + 1) * win]
            gs_ref[0, rows, :] += jnp.broadcast_to(gate, (win, LANES))


def _gather_win(a_flat, pos, aff_t, hm, *, cap, win, tt):
    b, ne, n = pos.shape
    d = hm.shape[2]
    tile = lambda bi, t, a: (bi, 0, t)
    whole = lambda bi, t, a: (bi, 0, 0)
    return pl.pallas_call(
        functools.partial(_gather_win_kernel, cap=cap, win=win),
        grid_spec=pltpu.PrefetchScalarGridSpec(
            num_scalar_prefetch=1, grid=(b, n // tt),
            in_specs=[pl.BlockSpec((1, ne, tt), tile), pl.BlockSpec((1, ne, tt), tile),
                      pl.BlockSpec((1, tt, d), lambda bi, t, a: (bi, t, 0))],
            out_specs=[pl.BlockSpec((1, ne * cap, d), whole), pl.BlockSpec((1, ne * cap, LANES), whole)]),
        out_shape=[jax.ShapeDtypeStruct((b, ne * cap, d), MXU_DTYPE),
                   jax.ShapeDtypeStruct((b, ne * cap, LANES), F32)],
        compiler_params=_cparams("parallel", "arbitrary"), name=f"gather_win_{n}",
    )(a_flat, pos, aff_t, hm)


def _gather_kernel(pos_ref, aff_ref, hm_ref, xs_ref, gs_ref, *, cap, tn):
    n = hm_ref.shape[1]
    d = hm_ref.shape[2]
    slot = lax.broadcasted_iota(jnp.int32, (cap, tn), 0).astype(F32)
    xs = jnp.zeros((cap, d), F32)
    gs = jnp.zeros((cap, 1), F32)
    for ci in range(n // tn):
        cols = slice(ci * tn, (ci + 1) * tn)
        hit = pos_ref[0, 0, :, cols] == slot
        xs = xs + _dot(hit.astype(MXU_DTYPE), hm_ref[0, cols, :])
        gs = gs + jnp.sum(jnp.where(hit, aff_ref[0, 0, :, cols], 0.0), axis=1, keepdims=True)
    xs_ref[0] = xs.astype(xs_ref.dtype)
    gs_ref[0] = jnp.broadcast_to(gs, (cap, LANES))


def _gather(pos, aff_t, hm, *, cap):
    b, ne, n = pos.shape
    d = hm.shape[2]
    tn = _tile(n, TOKEN_TILE)
    pos4 = pos.reshape(b, ne, 1, n)
    aff4 = aff_t.reshape(b, ne, 1, n)
    return pl.pallas_call(
        functools.partial(_gather_kernel, cap=cap, tn=tn),
        grid=(b, ne),
        in_specs=[pl.BlockSpec((1, 1, 1, n), lambda bi, e: (bi, e, 0, 0)),
                  pl.BlockSpec((1, 1, 1, n), lambda bi, e: (bi, e, 0, 0)),
                  pl.BlockSpec((1, n, d), lambda bi, e: (bi, 0, 0))],
        out_specs=[pl.BlockSpec((1, cap, d), lambda bi, e: (bi, e, 0)),
                   pl.BlockSpec((1, cap, LANES), lambda bi, e: (bi, e, 0))],
        out_shape=[jax.ShapeDtypeStruct((b, ne * cap, d), MXU_DTYPE),
                   jax.ShapeDtypeStruct((b, ne * cap, LANES), F32)],
        compiler_params=_cparams("parallel", "parallel"), name=f"gather_{n}",
    )(pos4, aff4, hm)


def _ffn_kernel(x_ref, wg_ref, wu_ref, wd_ref, gs_ref, y_ref, *, nf):
    f = pl.program_id(2)
    _, tm, d = y_ref.shape

    def partial_out():
        x = x_ref[...].reshape(tm, d)
        hid = _silu(_dot(x, wg_ref[0, 0])) * _dot(x, wu_ref[0, 0])
        return _dot(hid, wd_ref[0, 0])

    def finish(total):
        y_ref[0] = total * gs_ref[...].reshape(tm, LANES)[:, :1]

    if nf == 1:
        finish(partial_out())
        return

    @pl.when(f == 0)
    def _():
        y_ref[0] = partial_out()

    if nf > 2:
        @pl.when((f > 0) & (f < nf - 1))
        def _():
            y_ref[0] += partial_out()

    @pl.when(f == nf - 1)
    def _():
        finish(y_ref[0] + partial_out())


def _ffn(xs, w_gate, w_up, w_down, gs, *, layer, cap):
    b, rows, d = xs.shape
    _, ne, _, ff = w_gate.shape
    bt = b
    while bt * cap > FFN_ROWS and bt % 2 == 0:
        bt //= 2
    tm = bt * cap
    tf = _tile(ff, FFN_HIDDEN_TILE)
    return pl.pallas_call(
        functools.partial(_ffn_kernel, nf=ff // tf),
        grid=(ne, b // bt, ff // tf),
        in_specs=[pl.BlockSpec((bt, 1, cap, d), lambda e, i, f: (i, e, 0, 0)),
                  pl.BlockSpec((1, 1, d, tf), lambda e, i, f: (layer, e, 0, f)),
                  pl.BlockSpec((1, 1, d, tf), lambda e, i, f: (layer, e, 0, f)),
                  pl.BlockSpec((1, 1, tf, d), lambda e, i, f: (layer, e, f, 0)),
                  pl.BlockSpec((bt, 1, cap, LANES), lambda e, i, f: (i, e, 0, 0))],
        out_specs=pl.BlockSpec((1, tm, d), lambda e, i, f: (e, i, 0)),
        out_shape=jax.ShapeDtypeStruct((ne, b * cap, d), F32),
        compiler_params=_cparams("parallel", "parallel", "arbitrary"), name=f"ffn_{b * cap}",
    )(xs.reshape(b, ne, cap, d), w_gate, w_up, w_down, gs.reshape(b, ne, cap, LANES))


def _combine_win_kernel(a_ref, pos_ref, *refs, win):
    y_refs, o_ref = refs[:-1], refs[-1]
    d = o_ref.shape[2]
    _, hits = _window_hits(a_ref, pos_ref, win=win)
    onehot = jnp.concatenate([h.astype(MXU_DTYPE) for h in hits], axis=0)
    parts = [_split_hi_lo(y_ref[0]) for y_ref in y_refs]
    y_all = jnp.concatenate([jnp.concatenate([p[0] for p in parts], axis=0),
                             jnp.concatenate([p[1] for p in parts], axis=0)], axis=1)
    r = _dot_tn(onehot, y_all)
    o_ref[0] = r[:, :d] + r[:, d:]


def _combine_win(a_flat, pos, yg, *, cap, win, tt):
    b, ne, n = pos.shape
    d = yg.shape[2]
    nt = n // tt

    def window(e):
        return pl.BlockSpec((pl.Element(1), pl.Element(win), pl.Element(d)),
                            lambda bi, t, a: (e, pl.multiple_of(bi * cap + a[(bi * ne + e) * nt + t], PACKED_ROWS), 0))

    return pl.pallas_call(
        functools.partial(_combine_win_kernel, win=win),
        grid_spec=pltpu.PrefetchScalarGridSpec(
            num_scalar_prefetch=1, grid=(b, nt),
            in_specs=[pl.BlockSpec((1, ne, tt), lambda bi, t, a: (bi, 0, t))] + [window(e) for e in range(ne)],
            out_specs=pl.BlockSpec((1, tt, d), lambda bi, t, a: (bi, t, 0))),
        out_shape=jax.ShapeDtypeStruct((b, n, d), F32),
        compiler_params=_cparams("parallel", "parallel"), name=f"combine_win_{n}",
    )(a_flat, pos, *([yg] * ne))


def _combine_kernel(pos_ref, y_ref, o_ref, *, cap, tn):
    e = pl.program_id(1)

    @pl.when(e == 0)
    def _():
        o_ref[...] = jnp.zeros_like(o_ref)

    n = o_ref.shape[1]
    y_hi, y_lo = _split_hi_lo(y_ref[0])
    slot = lax.broadcasted_iota(jnp.int32, (cap, tn), 0).astype(F32)
    for ci in range(n // tn):
        cols = slice(ci * tn, (ci + 1) * tn)
        hit = (pos_ref[0, 0, :, cols] == slot).astype(MXU_DTYPE)
        o_ref[0, cols, :] += _dot_tn(hit, y_hi) + _dot_tn(hit, y_lo)


def _combine(pos, yg, *, cap):
    b, ne, n = pos.shape
    d = yg.shape[2]
    tn = _tile(n, TOKEN_TILE)
    return pl.pallas_call(
        functools.partial(_combine_kernel, cap=cap, tn=tn),
        grid=(b, ne),
        in_specs=[pl.BlockSpec((1, 1, 1, n), lambda bi, e: (bi, e, 0, 0)),
                  pl.BlockSpec((1, cap, d), lambda bi, e: (e, bi, 0))],
        out_specs=pl.BlockSpec((1, n, d), lambda bi, e: (bi, 0, 0)),
        out_shape=jax.ShapeDtypeStruct((b, n, d), F32),
        compiler_params=_cparams("parallel", "arbitrary"), name=f"combine_{n}",
    )(pos.reshape(b, ne, 1, n), yg)


def _final_kernel(x_ref, moe_ref, g2_ref, ng_ref, o_ref):
    x = x_ref[0] + g2_ref[0] * moe_ref[0]
    o_ref[0] = x * lax.rsqrt(jnp.mean(x * x, axis=-1, keepdims=True) + EPS) * ng_ref[...]


def _final(x, moe, g2, ng):
    b, n, d = x.shape
    tm = _tile(n, TOKEN_TILE)
    tok = lambda bi, i: (bi, i, 0)
    return pl.pallas_call(
        _final_kernel,
        grid=(b, n // tm),
        in_specs=[pl.BlockSpec((1, tm, d), tok), pl.BlockSpec((1, tm, d), tok),
                  pl.BlockSpec((1, 1, d), lambda bi, i: (bi, 0, 0)), pl.BlockSpec((1, d), lambda bi, i: (0, 0))],
        out_specs=pl.BlockSpec((1, tm, d), tok),
        out_shape=jax.ShapeDtypeStruct((b, n, d), F32),
        compiler_params=_cparams("parallel", "parallel"), name="final",
    )(x, moe, g2, ng)


def _rope_tables(n, heads):
    rows = n // GRID_W
    row = jnp.repeat(jnp.arange(rows), GRID_W).astype(F32)
    col = jnp.tile(jnp.arange(GRID_W), rows).astype(F32)
    n_freq = HEAD_DIM // 4
    freqs = ROPE_THETA ** (-jnp.arange(n_freq, dtype=F32) / n_freq)
    ang_r = row[:, None] * freqs
    ang_c = col[:, None] * freqs
    ang = jnp.concatenate([ang_r, ang_r, ang_c, ang_c], axis=-1)
    sign = jnp.where((jnp.arange(HEAD_DIM) % ROPE_GROUP) < ROPE_GROUP // 2, -1.0, 1.0).astype(F32)
    return jnp.tile(jnp.cos(ang), (1, 2 * heads)), jnp.tile(jnp.sin(ang) * sign, (1, 2 * heads))


def _moe(hm, aff_t, tri, w_gate, w_up, w_down, layer):
    b, n, _ = hm.shape
    ne = aff_t.shape[1]
    cap = CAPACITY_FACTOR * n // ne
    tt = _tile(n, TOKEN_TILE)
    nt = n // tt
    win = min(SLOT_WINDOW, cap)
    pos, first = _route(aff_t, tri, cap=cap, tt=tt)
    first = first[:, :, :nt + 1].astype(jnp.int32)
    offs = jnp.minimum(first[:, :, :nt] // PACKED_ROWS * PACKED_ROWS, cap - win)
    fits = jnp.all(first[:, :, 1:] <= offs + win)
    a_flat = offs.reshape(-1)
    xs, gs = lax.cond(fits,
                      lambda: _gather_win(a_flat, pos, aff_t, hm, cap=cap, win=win, tt=tt),
                      lambda: _gather(pos, aff_t, hm, cap=cap))
    yg = _ffn(xs, w_gate, w_up, w_down, gs, layer=layer, cap=cap)
    return lax.cond(fits,
                    lambda: _combine_win(a_flat, pos, yg, cap=cap, win=win, tt=tt),
                    lambda: _combine(pos, yg, cap=cap))


def kernel(x, c, ctx, c_ctx, w_ada, b_ada, norm1_g, norm2_g, w_in, w_out, lambda_q1, lambda_k1, lambda_q2, lambda_k2, subln_g, sgu_norm_g, sgu_w, sgu_b, w_router, w_gate, w_up, w_down, norm_f_g):
    b, n, d = x.shape
    m = ctx.shape[1]
    depth = w_in.shape[0]
    sw = sgu_norm_g.shape[1]
    aw = (w_in.shape[2] - 2 * sw) // 3
    heads = aw // (2 * HEAD_DIM)
    bf = lambda t: t.astype(MXU_DTYPE)

    rows = -(-(b + 1) // 8) * 8
    cc = jnp.concatenate([c, c_ctx[None, :], jnp.zeros((rows - b - 1, d), F32)], axis=0)
    mod = _ada(cc, w_ada, b_ada)

    w_in_all = bf(w_in)
    cos, sin = _rope_tables(n, heads)
    cos_c = jnp.ones((m, aw), F32)
    sin_c = jnp.zeros((m, aw), F32)
    hw_s = sw // SGU_HEADS
    ones_blk = bf(jnp.kron(jnp.eye(SGU_HEADS, dtype=F32), jnp.full((hw_s, hw_s), 1.0 / hw_s, F32)))
    tri = bf(jnp.triu(jnp.ones((LANES, LANES), F32)))

    cx = ctx
    res_x = res_c = None
    for l in range(depth):
        last = l == depth - 1
        lam_init = 0.8 - 0.6 * math.exp(-0.3 * l)
        parts = [mod[l, :, i * d:(i + 1) * d] for i in range(6)]
        sh1, sc1, g1, sh2, sc2, g2 = [t[:b, None, :] for t in parts]
        csh1, csc1, cg1, csh2, csc2, cg2 = [jnp.broadcast_to(t[b][None, None, :], (b, 1, d)) for t in parts]

        w_in_p = w_in_all[l]
        w_out_l = bf(w_out[l])
        wr_t = bf(w_router[l].T)
        ng1, ng2 = norm1_g[l][None, :], norm2_g[l][None, :]
        sng = sgu_norm_g[l][None, :]
        ws = bf(sgu_w[l].reshape(SGU_HEADS * CHUNK, CHUNK))
        bs = jnp.repeat(sgu_b[l].T, hw_s, axis=1)
        lam_params = jnp.zeros((8, LANES), F32).at[:4, :HEAD_DIM].set(
            jnp.stack([lambda_q1[l], lambda_k1[l], lambda_q2[l], lambda_k2[l]]))
        sub_g = subln_g[l][None, :]
        shared = dict(aw=aw, sw=sw)

        x, (q, k, v, sg) = _proj(x, res_x, ng1, sh1, sc1, w_in_p, cos, sin, sng, ws, bs, ones_blk, rope=True, **shared)
        cx, (qc, kc, vc, sgc) = _proj(cx, res_c, ng1, csh1, csc1, w_in_p, cos_c, sin_c, sng, ws, bs, ones_blk,
                                      rope=False, **shared)
        o = _attn(lam_params, q, [(kc, vc), (k, v)], sub_g, lam_init=lam_init)
        x, hm, aff_t = _outproj(o, sg, w_out_l, x, g1, ng2, sh2, sc2, wr_t)
        res_x = (_moe(hm, aff_t, tri, w_gate, w_up, w_down, l), g2)
        if not last:
            oc = _attn(lam_params, qc, [(kc, vc)], sub_g, lam_init=lam_init)
            cx, hmc, aff_c = _outproj(oc, sgc, w_out_l, cx, cg1, ng2, csh2, csc2, wr_t)
            res_c = (_moe(hmc, aff_c, tri, w_gate, w_up, w_down, l), cg2)
    return _final(x, res_x[0], res_x[1], norm_f_g[None, :])
```

```python
import functools
import math

import jax
import jax.numpy as jnp
from jax import lax
from jax.experimental import pallas as pl
from jax.experimental.pallas import tpu as pltpu

F32 = jnp.float32
MXU_DTYPE = jnp.bfloat16

EPS = 1e-6
GRID_W = 64
ROPE_THETA = 10000.0
HEAD_DIM = 64
SGU_HEADS = 8
CHUNK = 128
ROPE_GROUP = HEAD_DIM // 2
CAPACITY_FACTOR = 2
LANES = 128
PACKED_ROWS = 16
TOKEN_TILE = 512
ATTN_KEY_CHUNK = 256
ADA_TILE = 1536
FFN_ROWS = 1024
FFN_HIDDEN_TILE = 1024
MOE_TOKEN_TILE = 256
SLOT_WINDOW = 80
VMEM_LIMIT = 56 * 1024 * 1024


def _cparams(*sem):
    return pltpu.CompilerParams(dimension_semantics=sem, vmem_limit_bytes=VMEM_LIMIT)


def _dot(a, b):
    return jnp.dot(a.astype(MXU_DTYPE), b.astype(MXU_DTYPE), preferred_element_type=F32)


def _dot_nt(a, b):
    return lax.dot_general(a.astype(MXU_DTYPE), b.astype(MXU_DTYPE), (((1,), (1,)), ((), ())),
                           preferred_element_type=F32)


def _dot_tn(a, b):
    return lax.dot_general(a.astype(MXU_DTYPE), b.astype(MXU_DTYPE), (((0,), (0,)), ((), ())),
                           preferred_element_type=F32)


def _split_hi_lo(x):
    hi = x.astype(MXU_DTYPE)
    lo = (x - hi.astype(F32)).astype(MXU_DTYPE)
    return hi, lo


def _silu(x):
    return x * jax.nn.sigmoid(x)


def _gelu(x):
    return 0.5 * x * (1.0 + lax.erf(x * (2.0 ** -0.5)))


def _tile(n, pref):
    t = pref
    while t > 8 and n % t:
        t //= 2
    return t if n % t == 0 else n


def _ada_kernel(c_ref, w_ref, b_ref, o_ref):
    o_ref[0] = _dot(_silu(c_ref[...]), w_ref[0]) + b_ref[0]


def _ada(cc, w_ada, b_ada):
    depth, d, n6 = w_ada.shape
    rows = cc.shape[0]
    tn = _tile(n6, ADA_TILE)
    return pl.pallas_call(
        _ada_kernel,
        grid=(depth, n6 // tn),
        in_specs=[pl.BlockSpec((rows, d), lambda l, j: (0, 0)),
                  pl.BlockSpec((1, d, tn), lambda l, j: (l, 0, j)),
                  pl.BlockSpec((1, 1, tn), lambda l, j: (l, 0, j))],
        out_specs=pl.BlockSpec((1, rows, tn), lambda l, j: (l, 0, j)),
        out_shape=jax.ShapeDtypeStruct((depth, rows, n6), F32),
        compiler_params=_cparams("parallel", "parallel"), name="ada",
    )(cc, w_ada, b_ada.reshape(depth, 1, n6))


def _proj_kernel(*refs, has_res, rope, aw, sw):
    if has_res:
        x_ref, moe_ref, g2_ref, *refs = refs
    else:
        x_ref, *refs = refs
    (ng_ref, sh_ref, sc_ref, w_ref, cos_ref, sin_ref, sng_ref, ws_ref, bs_ref, ones_ref, *outs) = refs
    if has_res:
        xo_ref, q_ref, k_ref, v_ref, sg_ref = outs
    else:
        q_ref, k_ref, v_ref, sg_ref = outs

    x = x_ref[0]
    if has_res:
        x = x + g2_ref[0] * moe_ref[0]
        xo_ref[0] = x
    h = x * lax.rsqrt(jnp.mean(x * x, axis=-1, keepdims=True) + EPS) * ng_ref[...]
    h = h * (1.0 + sc_ref[0]) + sh_ref[0]
    p = _dot(h, w_ref[...])

    q = p[:, :aw]
    k = p[:, aw:2 * aw]
    if rope:
        half = ROPE_GROUP // 2
        first = (lax.broadcasted_iota(jnp.int32, q.shape, 1) % ROPE_GROUP) < half
        cos = cos_ref[...]
        sin = sin_ref[...]

        def rot(t):
            return jnp.where(first, pltpu.roll(t, aw - half, 1), pltpu.roll(t, half, 1))

        q = q * cos + rot(q) * sin
        k = k * cos + rot(k) * sin
    q_ref[0] = (q * (HEAD_DIM ** -0.5 * math.log2(math.e))).astype(q_ref.dtype)
    k_ref[0] = k.astype(k_ref.dtype)
    v_ref[0] = p[:, 2 * aw:3 * aw].astype(v_ref.dtype)

    u = _gelu(p[:, 3 * aw:3 * aw + sw])
    gv = _gelu(p[:, 3 * aw + sw:])
    sq_hi, sq_lo = _split_hi_lo(gv * gv)
    ms = _dot(sq_hi, ones_ref[...]) + _dot(sq_lo, ones_ref[...])
    gvn = (gv * lax.rsqrt(ms + EPS) * sng_ref[...]).astype(MXU_DTYPE)
    head_of_lane = lax.broadcasted_iota(jnp.int32, (CHUNK, sw), 1) // (sw // SGU_HEADS)
    tm = x.shape[0]
    for ci in range(tm // CHUNK):
        rows = slice(ci * CHUNK, (ci + 1) * CHUNK)
        r = _dot(ws_ref[...], gvn[rows])
        s = bs_ref[...]
        for hh in range(SGU_HEADS):
            s = s + jnp.where(head_of_lane == hh, r[hh * CHUNK:(hh + 1) * CHUNK], 0.0)
        sg_ref[0, rows, :] = (u[rows] * s).astype(sg_ref.dtype)


def _proj(x, res, ng, sh, sc, w_in_p, cos, sin, sng, ws, bs, ones_blk, *, rope, aw, sw):
    b, n, d = x.shape
    pw = w_in_p.shape[1]
    tm = _tile(n, TOKEN_TILE)
    has_res = res is not None
    tok = lambda bi, i: (bi, i, 0)
    per_b = lambda bi, i: (bi, 0, 0)
    const2 = lambda bi, i: (0, 0)
    in_specs = [pl.BlockSpec((1, tm, d), tok)]
    args = [x]
    if has_res:
        moe, g2 = res
        in_specs += [pl.BlockSpec((1, tm, d), tok), pl.BlockSpec((1, 1, d), per_b)]
        args += [moe, g2]
    in_specs += [pl.BlockSpec((1, d), const2), pl.BlockSpec((1, 1, d), per_b), pl.BlockSpec((1, 1, d), per_b),
                 pl.BlockSpec((d, pw), const2),
                 pl.BlockSpec((tm, aw), lambda bi, i: (i, 0)), pl.BlockSpec((tm, aw), lambda bi, i: (i, 0)),
                 pl.BlockSpec((1, sw), const2), pl.BlockSpec((SGU_HEADS * CHUNK, CHUNK), const2),
                 pl.BlockSpec((CHUNK, sw), const2), pl.BlockSpec((sw, sw), const2)]
    args += [ng, sh, sc, w_in_p, cos, sin, sng, ws, bs, ones_blk]
    out_specs = [pl.BlockSpec((1, tm, aw), tok)] * 3 + [pl.BlockSpec((1, tm, sw), tok)]
    out_shape = [jax.ShapeDtypeStruct((b, n, aw), MXU_DTYPE)] * 3 + [jax.ShapeDtypeStruct((b, n, sw), MXU_DTYPE)]
    if has_res:
        out_specs = [pl.BlockSpec((1, tm, d), tok)] + out_specs
        out_shape = [jax.ShapeDtypeStruct((b, n, d), F32)] + out_shape
    outs = pl.pallas_call(
        functools.partial(_proj_kernel, has_res=has_res, rope=rope, aw=aw, sw=sw),
        grid=(b, n // tm),
        in_specs=in_specs, out_specs=out_specs, out_shape=out_shape,
        compiler_params=_cparams("parallel", "parallel"), name="proj_lat" if rope else "proj_ctx",
    )(*args)
    if has_res:
        return outs[0], outs[1:]
    return x, outs


def _attn_kernel(lam_ref, q0_ref, q1_ref, *refs, tq, tkc, lam_init):
    *key_refs, g_ref, o_ref, s0_ref, s1_ref, m0_ref, m1_ref = refs
    sources = [key_refs[i:i + 3] for i in range(0, len(key_refs), 3)]
    chunks, col = [], 0
    for k0_ref, k1_ref, v_ref in sources:
        for r0 in range(0, v_ref.shape[1], tkc):
            chunks.append(((k0_ref, k1_ref), v_ref, r0, col))
            col += tkc
    q_ref = (q0_ref, q1_ref)
    s_ref, m_ref = (s0_ref, s1_ref), (m0_ref, m1_ref)
    hw = q0_ref.shape[2]
    nt = q0_ref.shape[1] // tq
    lane = lax.broadcasted_iota(jnp.int32, (tq, hw), 1)
    mine = (lane >= HEAD_DIM) == (pl.program_id(1) % 2 == 1)
    lp = lam_ref[...]
    lam = (jnp.exp(jnp.sum(lp[0:1] * lp[1:2], axis=-1, keepdims=True))
           - jnp.exp(jnp.sum(lp[2:3] * lp[3:4], axis=-1, keepdims=True)) + lam_init)

    def rows(t):
        return pl.ds(t * tq if isinstance(t, int) else pl.multiple_of(t * tq, tq), tq)

    def phase(t, slot, *, do_scores=True, do_values=True):
        other = 1 - slot
        if do_scores:
            qc = []
            for c in range(2):
                q = q_ref[c][0, rows(t + 1), :]
                qc.append(jnp.where(mine, q, jnp.zeros_like(q)))
            run = [None, None]
        if do_values:
            top = [jnp.max(m_ref[slot][c], axis=-1, keepdims=True) for c in range(2)]
            acc = [jnp.zeros((tq, 2 * hw), F32), jnp.zeros((tq, 2 * hw), F32)]
        for k_ref, v_ref, r0, col in chunks:
            cols = slice(col, col + tkc)
            if do_values:
                vals = v_ref[0, r0:r0 + tkc, :]
                vals1 = jnp.concatenate([vals, jnp.ones_like(vals)], axis=1)
            for c in range(2):
                if do_scores:
                    s = _dot_nt(qc[c], k_ref[c][0, r0:r0 + tkc, :])
                    s_ref[other][c, :, cols] = s
                    for j in range(tkc // LANES):
                        part = s[:, j * LANES:(j + 1) * LANES]
                        run[c] = part if run[c] is None else jnp.maximum(run[c], part)
                if do_values:
                    p = jnp.exp2(s_ref[slot][c, :, cols] - top[c])
                    acc[c] = acc[c] + _dot(p, vals1)
        if do_scores:
            for c in range(2):
                m_ref[other][c] = run[c]
        if do_values:
            o = acc[0][:, :hw] / acc[0][:, hw:] - lam * (acc[1][:, :hw] / acc[1][:, hw:])
            o = o * lax.rsqrt(jnp.mean(o * o, axis=-1, keepdims=True) + EPS) * g_ref[...] * (1.0 - lam_init)
            o_ref[0, rows(t), :] = o.astype(o_ref.dtype)

    phase(-1, 1, do_values=False)

    def step(t, carry):
        for slot in range(2):
            @pl.when(t % 2 == slot)
            def _():
                phase(t, slot)
        return carry

    lax.fori_loop(0, nt - 1, step, 0)
    phase(nt - 1, (nt - 1) % 2, do_scores=False)


def _attn(lam_params, q, kv_sources, subln_g, *, lam_init):
    b, n, aw = q.shape
    lens = [k.shape[1] for k, _ in kv_sources]
    nkeys = sum(lens)
    hw = 2 * HEAD_DIM
    heads = aw // hw
    pairs = heads // 2
    tq = _tile(n, TOKEN_TILE)
    tkc = min(_tile(m, ATTN_KEY_CHUNK) for m in lens)
    comp0 = lambda bi, h: (bi, 0, h // 2)
    comp1 = lambda bi, h: (bi, 0, pairs + h // 2)
    key_specs, key_args = [], []
    for (k, v), m in zip(kv_sources, lens):
        key_specs += [pl.BlockSpec((1, m, hw), comp0), pl.BlockSpec((1, m, hw), comp1),
                      pl.BlockSpec((1, m, hw), lambda bi, h: (bi, 0, h))]
        key_args += [k, k, v]
    return pl.pallas_call(
        functools.partial(_attn_kernel, tq=tq, tkc=tkc, lam_init=lam_init),
        grid=(b, heads),
        in_specs=[pl.BlockSpec(lam_params.shape, lambda bi, h: (0, 0)),
                  pl.BlockSpec((1, n, hw), comp0), pl.BlockSpec((1, n, hw), comp1), *key_specs,
                  pl.BlockSpec((1, hw), lambda bi, h: (0, 0))],
        out_specs=pl.BlockSpec((1, n, hw), lambda bi, h: (bi, 0, h)),
        out_shape=jax.ShapeDtypeStruct((b, n, aw), MXU_DTYPE),
        scratch_shapes=[pltpu.VMEM((2, tq, nkeys), F32)] * 2 + [pltpu.VMEM((2, tq, LANES), F32)] * 2,
        compiler_params=_cparams("parallel", "parallel"), name=f"attn_{n}",
    )(lam_params, q, q, *key_args, subln_g)


def _out_kernel(o_ref, sg_ref, w_ref, x_ref, g1_ref, ng_ref, sh_ref, sc_ref, wr_ref, xo_ref, hm_ref, aff_ref, *, aw):
    y = _dot(o_ref[0], w_ref[:aw, :]) + _dot(sg_ref[0], w_ref[aw:, :])
    x = x_ref[0] + g1_ref[0] * y
    xo_ref[0] = x
    h = x * lax.rsqrt(jnp.mean(x * x, axis=-1, keepdims=True) + EPS) * ng_ref[...]
    h = h * (1.0 + sc_ref[0]) + sh_ref[0]
    hm_ref[0] = h.astype(hm_ref.dtype)
    logits = _dot_nt(wr_ref[...], h)
    e = jnp.exp(logits - jnp.max(logits, axis=0, keepdims=True))
    aff_ref[0] = e / jnp.sum(e, axis=0, keepdims=True)


def _outproj(o, sg, w_out, x, g1, ng, sh, sc, wr_t):
    b, n, d = x.shape
    aw = o.shape[2]
    sw = sg.shape[2]
    ne = wr_t.shape[0]
    tm = _tile(n, TOKEN_TILE)
    tok = lambda bi, i: (bi, i, 0)
    per_b = lambda bi, i: (bi, 0, 0)
    const2 = lambda bi, i: (0, 0)
    return pl.pallas_call(
        functools.partial(_out_kernel, aw=aw),
        grid=(b, n // tm),
        in_specs=[pl.BlockSpec((1, tm, aw), tok), pl.BlockSpec((1, tm, sw), tok),
                  pl.BlockSpec((aw + sw, d), const2), pl.BlockSpec((1, tm, d), tok),
                  pl.BlockSpec((1, 1, d), per_b), pl.BlockSpec((1, d), const2),
                  pl.BlockSpec((1, 1, d), per_b), pl.BlockSpec((1, 1, d), per_b),
                  pl.BlockSpec((ne, d), const2)],
        out_specs=[pl.BlockSpec((1, tm, d), tok), pl.BlockSpec((1, tm, d), tok),
                   pl.BlockSpec((1, ne, tm), lambda bi, i: (bi, 0, i))],
        out_shape=[jax.ShapeDtypeStruct((b, n, d), F32), jax.ShapeDtypeStruct((b, n, d), MXU_DTYPE),
                   jax.ShapeDtypeStruct((b, ne, n), F32)],
        compiler_params=_cparams("parallel", "parallel"), name=f"outproj_{n}",
    )(o, sg, w_out, x, g1, ng, sh, sc, wr_t)


def _route_kernel(aff_ref, tri_ref, pos_ref, start_ref, *, cap, tt):
    a = aff_ref[0]
    ne, n = a.shape
    bits = lax.bitcast_convert_type(a, jnp.int32)
    t = jnp.zeros((ne, 1), jnp.int32)
    for bit in range(30, -1, -1):
        cand = t | (1 << bit)
        cnt = jnp.sum((bits >= cand).astype(jnp.int32), axis=1, keepdims=True)
        t = jnp.where(cnt >= cap, cand, t)
    gt = bits > t
    eq = bits == t
    need = cap - jnp.sum(gt.astype(jnp.int32), axis=1, keepdims=True)

    def exclusive_count(mask):
        out = []
        runs = []
        run = jnp.zeros((ne, 1), F32)
        for ci in range(n // LANES):
            runs.append(run)
            m = mask[:, ci * LANES:(ci + 1) * LANES].astype(F32)
            incl = _dot(m, tri_ref[...])
            out.append(run + incl - m)
            run = run + incl[:, LANES - 1:LANES]
        return jnp.concatenate(out, axis=1), runs

    sel = gt | (eq & (exclusive_count(eq)[0] < need.astype(F32)))
    pos, runs = exclusive_count(sel)
    pos_ref[0] = jnp.where(sel, pos, -1.0)
    lane = lax.broadcasted_iota(jnp.int32, (ne, LANES), 1)
    first = jnp.full((ne, LANES), float(cap), F32)
    for j in range(n // tt):
        first = jnp.where(lane == j, runs[j * tt // LANES], first)
    start_ref[0] = first


def _route(aff_t, tri, *, cap, tt):
    b, ne, n = aff_t.shape
    rows = b * ne
    pos, first = pl.pallas_call(
        functools.partial(_route_kernel, cap=cap, tt=tt),
        grid=(1,),
        in_specs=[pl.BlockSpec((1, rows, n), lambda i: (0, 0, 0)), pl.BlockSpec((LANES, LANES), lambda i: (0, 0))],
        out_specs=[pl.BlockSpec((1, rows, n), lambda i: (0, 0, 0)), pl.BlockSpec((1, rows, LANES), lambda i: (0, 0, 0))],
        out_shape=[jax.ShapeDtypeStruct((1, rows, n), F32), jax.ShapeDtypeStruct((1, rows, LANES), F32)],
        compiler_params=_cparams("arbitrary"), name=f"route_{n}",
    )(aff_t.reshape(1, rows, n), tri)
    return pos.reshape(b, ne, n), first.reshape(b, ne, LANES)


def _window_hits(a_ref, pos_ref, *, win):
    bi, t, nt = pl.program_id(0), pl.program_id(1), pl.num_programs(1)
    ne, tt = pos_ref.shape[1], pos_ref.shape[2]
    base = lax.broadcasted_iota(jnp.int32, (win, tt), 0)
    offs, hits = [], []
    for e in range(ne):
        a_e = a_ref[(bi * ne + e) * nt + t]
        offs.append(a_e)
        hits.append(pos_ref[0, e:e + 1, :] == (base + a_e).astype(F32))
    return offs, hits


def _gather_win_kernel(a_ref, pos_ref, aff_ref, hm_ref, xs_ref, gs_ref, *, cap, win):
    @pl.when(pl.program_id(1) == 0)
    def _():
        xs_ref[...] = jnp.zeros_like(xs_ref)
        gs_ref[...] = jnp.zeros_like(gs_ref)

    ne = pos_ref.shape[1]
    offs, hits = _window_hits(a_ref, pos_ref, win=win)
    group = 8
    for e0 in range(0, ne, group):
        onehot = jnp.concatenate([h.astype(MXU_DTYPE) for h in hits[e0:e0 + group]], axis=0)
        picked = _dot(onehot, hm_ref[0]).astype(xs_ref.dtype)
        for e in range(e0, min(e0 + group, ne)):
            rows = pl.ds(pl.multiple_of(e * cap + offs[e], PACKED_ROWS), win)
            gate = jnp.sum(jnp.where(hits[e], aff_ref[0, e:e + 1, :], 0.0), axis=1, keepdims=True)
            xs_ref[0, rows, :] += picked[(e - e0) * win:(e - e0 + 1) * win]
            gs_ref[0, rows, :] += jnp.broadcast_to(gate, (win, LANES))


def _gather_win(a_flat, pos, aff_t, hm, *, cap, win, tt):
    b, ne, n = pos.shape
    d = hm.shape[2]
    tile = lambda bi, t, a: (bi, 0, t)
    whole = lambda bi, t, a: (bi, 0, 0)
    return pl.pallas_call(
        functools.partial(_gather_win_kernel, cap=cap, win=win),
        grid_spec=pltpu.PrefetchScalarGridSpec(
            num_scalar_prefetch=1, grid=(b, n // tt),
            in_specs=[pl.BlockSpec((1, ne, tt), tile), pl.BlockSpec((1, ne, tt), tile),
                      pl.BlockSpec((1, tt, d), lambda bi, t, a: (bi, t, 0))],
            out_specs=[pl.BlockSpec((1, ne * cap, d), whole), pl.BlockSpec((1, ne * cap, LANES), whole)]),
        out_shape=[jax.ShapeDtypeStruct((b, ne * cap, d), MXU_DTYPE),
                   jax.ShapeDtypeStruct((b, ne * cap, LANES), F32)],
        compiler_params=_cparams("parallel", "arbitrary"), name=f"gather_win_{n}",
    )(a_flat, pos, aff_t, hm)


def _gather_kernel(pos_ref, aff_ref, hm_ref, xs_ref, gs_ref, *, cap, tn):
    n = hm_ref.shape[1]
    d = hm_ref.shape[2]
    slot = lax.broadcasted_iota(jnp.int32, (cap, tn), 0).astype(F32)
    xs = jnp.zeros((cap, d), F32)
    gs = jnp.zeros((cap, 1), F32)
    for ci in range(n // tn):
        cols = slice(ci * tn, (ci + 1) * tn)
        hit = pos_ref[0, 0, :, cols] == slot
        xs = xs + _dot(hit.astype(MXU_DTYPE), hm_ref[0, cols, :])
        gs = gs + jnp.sum(jnp.where(hit, aff_ref[0, 0, :, cols], 0.0), axis=1, keepdims=True)
    xs_ref[0] = xs.astype(xs_ref.dtype)
    gs_ref[0] = jnp.broadcast_to(gs, (cap, LANES))


def _gather(pos, aff_t, hm, *, cap):
    b, ne, n = pos.shape
    d = hm.shape[2]
    tn = _tile(n, TOKEN_TILE)
    pos4 = pos.reshape(b, ne, 1, n)
    aff4 = aff_t.reshape(b, ne, 1, n)
    return pl.pallas_call(
        functools.partial(_gather_kernel, cap=cap, tn=tn),
        grid=(b, ne),
        in_specs=[pl.BlockSpec((1, 1, 1, n), lambda bi, e: (bi, e, 0, 0)),
                  pl.BlockSpec((1, 1, 1, n), lambda bi, e: (bi, e, 0, 0)),
                  pl.BlockSpec((1, n, d), lambda bi, e: (bi, 0, 0))],
        out_specs=[pl.BlockSpec((1, cap, d), lambda bi, e: (bi, e, 0)),
                   pl.BlockSpec((1, cap, LANES), lambda bi, e: (bi, e, 0))],
        out_shape=[jax.ShapeDtypeStruct((b, ne * cap, d), MXU_DTYPE),
                   jax.ShapeDtypeStruct((b, ne * cap, LANES), F32)],
        compiler_params=_cparams("parallel", "parallel"), name=f"gather_{n}",
    )(pos4, aff4, hm)


def _ffn_kernel(x_ref, wg_ref, wu_ref, wd_ref, gs_ref, y_ref, *, nf):
    f = pl.program_id(2)
    _, tm, d = y_ref.shape

    def partial_out():
        x = x_ref[...].reshape(tm, d)
        hid = _silu(_dot(x, wg_ref[0, 0])) * _dot(x, wu_ref[0, 0])
        return _dot(hid, wd_ref[0, 0])

    def finish(total):
        y_ref[0] = total * gs_ref[...].reshape(tm, LANES)[:, :1]

    if nf == 1:
        finish(partial_out())
        return

    @pl.when(f == 0)
    def _():
        y_ref[0] = partial_out()

    if nf > 2:
        @pl.when((f > 0) & (f < nf - 1))
        def _():
            y_ref[0] += partial_out()

    @pl.when(f == nf - 1)
    def _():
        finish(y_ref[0] + partial_out())


def _ffn(xs, w_gate, w_up, w_down, gs, *, layer, cap):
    b, rows, d = xs.shape
    _, ne, _, ff = w_gate.shape
    bt = b
    while bt * cap > FFN_ROWS and bt % 2 == 0:
        bt //= 2
    tm = bt * cap
    tf = _tile(ff, FFN_HIDDEN_TILE)
    return pl.pallas_call(
        functools.partial(_ffn_kernel, nf=ff // tf),
        grid=(ne, b // bt, ff // tf),
        in_specs=[pl.BlockSpec((bt, 1, cap, d), lambda e, i, f: (i, e, 0, 0)),
                  pl.BlockSpec((1, 1, d, tf), lambda e, i, f: (layer, e, 0, f)),
                  pl.BlockSpec((1, 1, d, tf), lambda e, i, f: (layer, e, 0, f)),
                  pl.BlockSpec((1, 1, tf, d), lambda e, i, f: (layer, e, f, 0)),
                  pl.BlockSpec((bt, 1, cap, LANES), lambda e, i, f: (i, e, 0, 0))],
        out_specs=pl.BlockSpec((1, tm, d), lambda e, i, f: (e, i, 0)),
        out_shape=jax.ShapeDtypeStruct((ne, b * cap, d), F32),
        compiler_params=_cparams("parallel", "parallel", "arbitrary"), name=f"ffn_{b * cap}",
    )(xs.reshape(b, ne, cap, d), w_gate, w_up, w_down, gs.reshape(b, ne, cap, LANES))


def _combine_win_kernel(a_ref, pos_ref, *refs, win):
    y_refs, o_ref = refs[:-1], refs[-1]
    d = o_ref.shape[2]
    _, hits = _window_hits(a_ref, pos_ref, win=win)
    onehot = jnp.concatenate([h.astype(MXU_DTYPE) for h in hits], axis=0)
    parts = [_split_hi_lo(y_ref[0]) for y_ref in y_refs]
    y_all = jnp.concatenate([jnp.concatenate([p[0] for p in parts], axis=0),
                             jnp.concatenate([p[1] for p in parts], axis=0)], axis=1)
    r = _dot_tn(onehot, y_all)
    o_ref[0] = r[:, :d] + r[:, d:]


def _combine_win(a_flat, pos, yg, *, cap, win, tt):
    b, ne, n = pos.shape
    d = yg.shape[2]
    nt = n // tt

    def window(e):
        return pl.BlockSpec((pl.Element(1), pl.Element(win), pl.Element(d)),
                            lambda bi, t, a: (e, pl.multiple_of(bi * cap + a[(bi * ne + e) * nt + t], PACKED_ROWS), 0))

    return pl.pallas_call(
        functools.partial(_combine_win_kernel, win=win),
        grid_spec=pltpu.PrefetchScalarGridSpec(
            num_scalar_prefetch=1, grid=(b, nt),
            in_specs=[pl.BlockSpec((1, ne, tt), lambda bi, t, a: (bi, 0, t))] + [window(e) for e in range(ne)],
            out_specs=pl.BlockSpec((1, tt, d), lambda bi, t, a: (bi, t, 0))),
        out_shape=jax.ShapeDtypeStruct((b, n, d), F32),
        compiler_params=_cparams("parallel", "parallel"), name=f"combine_win_{n}",
    )(a_flat, pos, *([yg] * ne))


def _combine_kernel(pos_ref, y_ref, o_ref, *, cap, tn):
    e = pl.program_id(1)

    @pl.when(e == 0)
    def _():
        o_ref[...] = jnp.zeros_like(o_ref)

    n = o_ref.shape[1]
    y_hi, y_lo = _split_hi_lo(y_ref[0])
    slot = lax.broadcasted_iota(jnp.int32, (cap, tn), 0).astype(F32)
    for ci in range(n // tn):
        cols = slice(ci * tn, (ci + 1) * tn)
        hit = (pos_ref[0, 0, :, cols] == slot).astype(MXU_DTYPE)
        o_ref[0, cols, :] += _dot_tn(hit, y_hi) + _dot_tn(hit, y_lo)


def _combine(pos, yg, *, cap):
    b, ne, n = pos.shape
    d = yg.shape[2]
    tn = _tile(n, TOKEN_TILE)
    return pl.pallas_call(
        functools.partial(_combine_kernel, cap=cap, tn=tn),
        grid=(b, ne),
        in_specs=[pl.BlockSpec((1, 1, 1, n), lambda bi, e: (bi, e, 0, 0)),
                  pl.BlockSpec((1, cap, d), lambda bi, e: (e, bi, 0))],
        out_specs=pl.BlockSpec((1, n, d), lambda bi, e: (bi, 0, 0)),
        out_shape=jax.ShapeDtypeStruct((b, n, d), F32),
        compiler_params=_cparams("parallel", "arbitrary"), name=f"combine_{n}",
    )(pos.reshape(b, ne, 1, n), yg)


def _final_kernel(x_ref, moe_ref, g2_ref, ng_ref, o_ref):
    x = x_ref[0] + g2_ref[0] * moe_ref[0]
    o_ref[0] = x * lax.rsqrt(jnp.mean(x * x, axis=-1, keepdims=True) + EPS) * ng_ref[...]


def _final(x, moe, g2, ng):
    b, n, d = x.shape
    tm = _tile(n, TOKEN_TILE)
    tok = lambda bi, i: (bi, i, 0)
    return pl.pallas_call(
        _final_kernel,
        grid=(b, n // tm),
        in_specs=[pl.BlockSpec((1, tm, d), tok), pl.BlockSpec((1, tm, d), tok),
                  pl.BlockSpec((1, 1, d), lambda bi, i: (bi, 0, 0)), pl.BlockSpec((1, d), lambda bi, i: (0, 0))],
        out_specs=pl.BlockSpec((1, tm, d), tok),
        out_shape=jax.ShapeDtypeStruct((b, n, d), F32),
        compiler_params=_cparams("parallel", "parallel"), name="final",
    )(x, moe, g2, ng)


def _rope_tables(n, heads):
    rows = n // GRID_W
    row = jnp.repeat(jnp.arange(rows), GRID_W).astype(F32)
    col = jnp.tile(jnp.arange(GRID_W), rows).astype(F32)
    n_freq = HEAD_DIM // 4
    freqs = ROPE_THETA ** (-jnp.arange(n_freq, dtype=F32) / n_freq)
    ang_r = row[:, None] * freqs
    ang_c = col[:, None] * freqs
    ang = jnp.concatenate([ang_r, ang_r, ang_c, ang_c], axis=-1)
    sign = jnp.where((jnp.arange(HEAD_DIM) % ROPE_GROUP) < ROPE_GROUP // 2, -1.0, 1.0).astype(F32)
    return jnp.tile(jnp.cos(ang), (1, 2 * heads)), jnp.tile(jnp.sin(ang) * sign, (1, 2 * heads))


def _moe(hm, aff_t, tri, w_gate, w_up, w_down, layer):
    b, n, _ = hm.shape
    ne = aff_t.shape[1]
    cap = CAPACITY_FACTOR * n // ne
    tt = _tile(n, MOE_TOKEN_TILE)
    nt = n // tt
    win = min(SLOT_WINDOW, cap)
    pos, first = _route(aff_t, tri, cap=cap, tt=tt)
    first = first[:, :, :nt + 1].astype(jnp.int32)
    offs = jnp.minimum(first[:, :, :nt] // PACKED_ROWS * PACKED_ROWS, cap - win)
    fits = jnp.all(first[:, :, 1:] <= offs + win)
    a_flat = offs.reshape(-1)
    xs, gs = lax.cond(fits,
                      lambda: _gather_win(a_flat, pos, aff_t, hm, cap=cap, win=win, tt=tt),
                      lambda: _gather(pos, aff_t, hm, cap=cap))
    yg = _ffn(xs, w_gate, w_up, w_down, gs, layer=layer, cap=cap)
    return lax.cond(fits,
                    lambda: _combine_win(a_flat, pos, yg, cap=cap, win=win, tt=tt),
                    lambda: _combine(pos, yg, cap=cap))


def kernel(x, c, ctx, c_ctx, w_ada, b_ada, norm1_g, norm2_g, w_in, w_out, lambda_q1, lambda_k1, lambda_q2, lambda_k2, subln_g, sgu_norm_g, sgu_w, sgu_b, w_router, w_gate, w_up, w_down, norm_f_g):
    b, n, d = x.shape
    m = ctx.shape[1]
    depth = w_in.shape[0]
    sw = sgu_norm_g.shape[1]
    aw = (w_in.shape[2] - 2 * sw) // 3
    heads = aw // (2 * HEAD_DIM)
    bf = lambda t: t.astype(MXU_DTYPE)

    rows = -(-(b + 1) // 8) * 8
    cc = jnp.concatenate([c, c_ctx[None, :], jnp.zeros((rows - b - 1, d), F32)], axis=0)
    mod = _ada(cc, w_ada, b_ada)

    w_in_all = bf(w_in)
    cos, sin = _rope_tables(n, heads)
    cos_c = jnp.ones((m, aw), F32)
    sin_c = jnp.zeros((m, aw), F32)
    hw_s = sw // SGU_HEADS
    ones_blk = bf(jnp.kron(jnp.eye(SGU_HEADS, dtype=F32), jnp.full((hw_s, hw_s), 1.0 / hw_s, F32)))
    tri = bf(jnp.triu(jnp.ones((LANES, LANES), F32)))

    cx = ctx
    res_x = res_c = None
    for l in range(depth):
        last = l == depth - 1
        lam_init = 0.8 - 0.6 * math.exp(-0.3 * l)
        parts = [mod[l, :, i * d:(i + 1) * d] for i in range(6)]
        sh1, sc1, g1, sh2, sc2, g2 = [t[:b, None, :] for t in parts]
        csh1, csc1, cg1, csh2, csc2, cg2 = [jnp.broadcast_to(t[b][None, None, :], (b, 1, d)) for t in parts]

        w_in_p = w_in_all[l]
        w_out_l = bf(w_out[l])
        wr_t = bf(w_router[l].T)
        ng1, ng2 = norm1_g[l][None, :], norm2_g[l][None, :]
        sng = sgu_norm_g[l][None, :]
        ws = bf(sgu_w[l].reshape(SGU_HEADS * CHUNK, CHUNK))
        bs = jnp.repeat(sgu_b[l].T, hw_s, axis=1)
        lam_params = jnp.zeros((8, LANES), F32).at[:4, :HEAD_DIM].set(
            jnp.stack([lambda_q1[l], lambda_k1[l], lambda_q2[l], lambda_k2[l]]))
        sub_g = subln_g[l][None, :]
        shared = dict(aw=aw, sw=sw)

        x, (q, k, v, sg) = _proj(x, res_x, ng1, sh1, sc1, w_in_p, cos, sin, sng, ws, bs, ones_blk, rope=True, **shared)
        cx, (qc, kc, vc, sgc) = _proj(cx, res_c, ng1, csh1, csc1, w_in_p, cos_c, sin_c, sng, ws, bs, ones_blk,
                                      rope=False, **shared)
        o = _attn(lam_params, q, [(kc, vc), (k, v)], sub_g, lam_init=lam_init)
        x, hm, aff_t = _outproj(o, sg, w_out_l, x, g1, ng2, sh2, sc2, wr_t)
        res_x = (_moe(hm, aff_t, tri, w_gate, w_up, w_down, l), g2)
        if not last:
            oc = _attn(lam_params, qc, [(kc, vc)], sub_g, lam_init=lam_init)
            cx, hmc, aff_c = _outproj(oc, sgc, w_out_l, cx, cg1, ng2, csh2, csc2, wr_t)
            res_c = (_moe(hmc, aff_c, tri, w_gate, w_up, w_down, l), cg2)
    return _final(x, res_x[0], res_x[1], norm_f_g[None, :])
```

```python
import functools
import math

import jax
import jax.numpy as jnp
from jax import lax
from jax.experimental import pallas as pl
from jax.experimental.pallas import tpu as pltpu

F32 = jnp.float32
MXU_DTYPE = jnp.bfloat16

EPS = 1e-6
GRID_W = 64
ROPE_THETA = 10000.0
HEAD_DIM = 64
SGU_HEADS = 8
CHUNK = 128
ROPE_GROUP = HEAD_DIM // 2
CAPACITY_FACTOR = 2
LANES = 128
PACKED_ROWS = 16
TOKEN_TILE = 512
ATTN_KEY_CHUNK = 256
ADA_TILE = 1536
FFN_ROWS = 1024
FFN_HIDDEN_TILE = 1024
MOE_TOKEN_TILE = 256
SLOT_WINDOW = 80
VMEM_LIMIT = 56 * 1024 * 1024


def _cparams(*sem):
    return pltpu.CompilerParams(dimension_semantics=sem, vmem_limit_bytes=VMEM_LIMIT)


def _dot(a, b):
    return jnp.dot(a.astype(MXU_DTYPE), b.astype(MXU_DTYPE), preferred_element_type=F32)


def _dot_nt(a, b):
    return lax.dot_general(a.astype(MXU_DTYPE), b.astype(MXU_DTYPE), (((1,), (1,)), ((), ())),
                           preferred_element_type=F32)


def _dot_tn(a, b):
    return lax.dot_general(a.astype(MXU_DTYPE), b.astype(MXU_DTYPE), (((0,), (0,)), ((), ())),
                           preferred_element_type=F32)


def _split_hi_lo(x):
    hi = x.astype(MXU_DTYPE)
    lo = (x - hi.astype(F32)).astype(MXU_DTYPE)
    return hi, lo


def _silu(x):
    return x * jax.nn.sigmoid(x)


def _gelu(x):
    return 0.5 * x * (1.0 + lax.erf(x * (2.0 ** -0.5)))


def _tile(n, pref):
    t = pref
    while t > 8 and n % t:
        t //= 2
    return t if n % t == 0 else n


def _ada_kernel(c_ref, w_ref, b_ref, o_ref):
    o_ref[0] = _dot(_silu(c_ref[...]), w_ref[0]) + b_ref[0]


def _ada(cc, w_ada, b_ada):
    depth, d, n6 = w_ada.shape
    rows = cc.shape[0]
    tn = _tile(n6, ADA_TILE)
    return pl.pallas_call(
        _ada_kernel,
        grid=(depth, n6 // tn),
        in_specs=[pl.BlockSpec((rows, d), lambda l, j: (0, 0)),
                  pl.BlockSpec((1, d, tn), lambda l, j: (l, 0, j)),
                  pl.BlockSpec((1, 1, tn), lambda l, j: (l, 0, j))],
        out_specs=pl.BlockSpec((1, rows, tn), lambda l, j: (l, 0, j)),
        out_shape=jax.ShapeDtypeStruct((depth, rows, n6), F32),
        compiler_params=_cparams("parallel", "parallel"), name="ada",
    )(cc, w_ada, b_ada.reshape(depth, 1, n6))


def _proj_kernel(*refs, has_res, rope, aw, sw):
    if has_res:
        x_ref, moe_ref, g2_ref, *refs = refs
    else:
        x_ref, *refs = refs
    (ng_ref, sh_ref, sc_ref, w_ref, cos_ref, sin_ref, sng_ref, ones_ref, *outs) = refs
    if has_res:
        xo_ref, q_ref, k_ref, v_ref, u_ref, gvn_ref = outs
    else:
        q_ref, k_ref, v_ref, u_ref, gvn_ref = outs

    x = x_ref[0]
    if has_res:
        x = x + g2_ref[0] * moe_ref[0]
        xo_ref[0] = x
    h = x * lax.rsqrt(jnp.mean(x * x, axis=-1, keepdims=True) + EPS) * ng_ref[...]
    h = h * (1.0 + sc_ref[0]) + sh_ref[0]
    p = _dot(h, w_ref[...])

    q = p[:, :aw]
    k = p[:, aw:2 * aw]
    if rope:
        half = ROPE_GROUP // 2
        first = (lax.broadcasted_iota(jnp.int32, q.shape, 1) % ROPE_GROUP) < half
        cos = cos_ref[...]
        sin = sin_ref[...]

        def rot(t):
            return jnp.where(first, pltpu.roll(t, aw - half, 1), pltpu.roll(t, half, 1))

        q = q * cos + rot(q) * sin
        k = k * cos + rot(k) * sin
    q_ref[0] = (q * (HEAD_DIM ** -0.5 * math.log2(math.e))).astype(q_ref.dtype)
    k_ref[0] = k.astype(k_ref.dtype)
    v_ref[0] = p[:, 2 * aw:3 * aw].astype(v_ref.dtype)

    u_ref[0] = _gelu(p[:, 3 * aw:3 * aw + sw]).astype(u_ref.dtype)
    gv = _gelu(p[:, 3 * aw + sw:])
    sq_hi, sq_lo = _split_hi_lo(gv * gv)
    ms = _dot(sq_hi, ones_ref[...]) + _dot(sq_lo, ones_ref[...])
    gvn_ref[0] = (gv * lax.rsqrt(ms + EPS) * sng_ref[...]).astype(gvn_ref.dtype)


def _proj(x, res, ng, sh, sc, w_in_p, cos, sin, sng, ones_blk, *, rope, aw, sw):
    b, n, d = x.shape
    pw = w_in_p.shape[1]
    tm = _tile(n, TOKEN_TILE)
    has_res = res is not None
    tok = lambda bi, i: (bi, i, 0)
    per_b = lambda bi, i: (bi, 0, 0)
    const2 = lambda bi, i: (0, 0)
    in_specs = [pl.BlockSpec((1, tm, d), tok)]
    args = [x]
    if has_res:
        moe, g2 = res
        in_specs += [pl.BlockSpec((1, tm, d), tok), pl.BlockSpec((1, 1, d), per_b)]
        args += [moe, g2]
    in_specs += [pl.BlockSpec((1, d), const2), pl.BlockSpec((1, 1, d), per_b), pl.BlockSpec((1, 1, d), per_b),
                 pl.BlockSpec((d, pw), const2),
                 pl.BlockSpec((tm, aw), lambda bi, i: (i, 0)), pl.BlockSpec((tm, aw), lambda bi, i: (i, 0)),
                 pl.BlockSpec((1, sw), const2), pl.BlockSpec((sw, sw), const2)]
    args += [ng, sh, sc, w_in_p, cos, sin, sng, ones_blk]
    out_specs = [pl.BlockSpec((1, tm, aw), tok)] * 3 + [pl.BlockSpec((1, tm, sw), tok)] * 2
    out_shape = [jax.ShapeDtypeStruct((b, n, aw), MXU_DTYPE)] * 3 + [jax.ShapeDtypeStruct((b, n, sw), MXU_DTYPE)] * 2
    if has_res:
        out_specs = [pl.BlockSpec((1, tm, d), tok)] + out_specs
        out_shape = [jax.ShapeDtypeStruct((b, n, d), F32)] + out_shape
    outs = pl.pallas_call(
        functools.partial(_proj_kernel, has_res=has_res, rope=rope, aw=aw, sw=sw),
        grid=(b, n // tm),
        in_specs=in_specs, out_specs=out_specs, out_shape=out_shape,
        compiler_params=_cparams("parallel", "parallel"), name="proj_lat" if rope else "proj_ctx",
    )(*args)
    if has_res:
        return outs[0], outs[1:]
    return x, outs


def _attn_kernel(lam_ref, q0_ref, q1_ref, *refs, tq, tkc, lam_init):
    *key_refs, g_ref, o_ref, s0_ref, s1_ref, m0_ref, m1_ref = refs
    sources = [key_refs[i:i + 3] for i in range(0, len(key_refs), 3)]
    chunks, col = [], 0
    for k0_ref, k1_ref, v_ref in sources:
        for r0 in range(0, v_ref.shape[1], tkc):
            chunks.append(((k0_ref, k1_ref), v_ref, r0, col))
            col += tkc
    q_ref = (q0_ref, q1_ref)
    s_ref, m_ref = (s0_ref, s1_ref), (m0_ref, m1_ref)
    hw = q0_ref.shape[2]
    nt = q0_ref.shape[1] // tq
    lane = lax.broadcasted_iota(jnp.int32, (tq, hw), 1)
    mine = (lane >= HEAD_DIM) == (pl.program_id(1) % 2 == 1)
    lp = lam_ref[...]
    lam = (jnp.exp(jnp.sum(lp[0:1] * lp[1:2], axis=-1, keepdims=True))
           - jnp.exp(jnp.sum(lp[2:3] * lp[3:4], axis=-1, keepdims=True)) + lam_init)

    def rows(t):
        return pl.ds(t * tq if isinstance(t, int) else pl.multiple_of(t * tq, tq), tq)

    def phase(t, slot, *, do_scores=True, do_values=True):
        other = 1 - slot
        if do_scores:
            qc = []
            for c in range(2):
                q = q_ref[c][0, rows(t + 1), :]
                qc.append(jnp.where(mine, q, jnp.zeros_like(q)))
            run = [None, None]
        if do_values:
            top = [jnp.max(m_ref[slot][c], axis=-1, keepdims=True) for c in range(2)]
            acc = [jnp.zeros((tq, 2 * hw), F32), jnp.zeros((tq, 2 * hw), F32)]
        for k_ref, v_ref, r0, col in chunks:
            cols = slice(col, col + tkc)
            if do_values:
                vals = v_ref[0, r0:r0 + tkc, :]
                vals1 = jnp.concatenate([vals, jnp.ones_like(vals)], axis=1)
            for c in range(2):
                if do_scores:
                    s = _dot_nt(qc[c], k_ref[c][0, r0:r0 + tkc, :])
                    s_ref[other][c, :, cols] = s
                    for j in range(tkc // LANES):
                        part = s[:, j * LANES:(j + 1) * LANES]
                        run[c] = part if run[c] is None else jnp.maximum(run[c], part)
                if do_values:
                    p = jnp.exp2(s_ref[slot][c, :, cols] - top[c])
                    acc[c] = acc[c] + _dot(p, vals1)
        if do_scores:
            for c in range(2):
                m_ref[other][c] = run[c]
        if do_values:
            o = acc[0][:, :hw] / acc[0][:, hw:] - lam * (acc[1][:, :hw] / acc[1][:, hw:])
            o = o * lax.rsqrt(jnp.mean(o * o, axis=-1, keepdims=True) + EPS) * g_ref[...] * (1.0 - lam_init)
            o_ref[0, rows(t), :] = o.astype(o_ref.dtype)

    phase(-1, 1, do_values=False)

    def step(t, carry):
        for slot in range(2):
            @pl.when(t % 2 == slot)
            def _():
                phase(t, slot)
        return carry

    lax.fori_loop(0, nt - 1, step, 0)
    phase(nt - 1, (nt - 1) % 2, do_scores=False)


def _attn(lam_params, q, kv_sources, subln_g, *, lam_init):
    b, n, aw = q.shape
    lens = [k.shape[1] for k, _ in kv_sources]
    nkeys = sum(lens)
    hw = 2 * HEAD_DIM
    heads = aw // hw
    pairs = heads // 2
    tq = _tile(n, TOKEN_TILE)
    tkc = min(_tile(m, ATTN_KEY_CHUNK) for m in lens)
    comp0 = lambda bi, h: (bi, 0, h // 2)
    comp1 = lambda bi, h: (bi, 0, pairs + h // 2)
    key_specs, key_args = [], []
    for (k, v), m in zip(kv_sources, lens):
        key_specs += [pl.BlockSpec((1, m, hw), comp0), pl.BlockSpec((1, m, hw), comp1),
                      pl.BlockSpec((1, m, hw), lambda bi, h: (bi, 0, h))]
        key_args += [k, k, v]
    return pl.pallas_call(
        functools.partial(_attn_kernel, tq=tq, tkc=tkc, lam_init=lam_init),
        grid=(b, heads),
        in_specs=[pl.BlockSpec(lam_params.shape, lambda bi, h: (0, 0)),
                  pl.BlockSpec((1, n, hw), comp0), pl.BlockSpec((1, n, hw), comp1), *key_specs,
                  pl.BlockSpec((1, hw), lambda bi, h: (0, 0))],
        out_specs=pl.BlockSpec((1, n, hw), lambda bi, h: (bi, 0, h)),
        out_shape=jax.ShapeDtypeStruct((b, n, aw), MXU_DTYPE),
        scratch_shapes=[pltpu.VMEM((2, tq, nkeys), F32)] * 2 + [pltpu.VMEM((2, tq, LANES), F32)] * 2,
        compiler_params=_cparams("parallel", "parallel"), name=f"attn_{n}",
    )(lam_params, q, q, *key_args, subln_g)


def _out_kernel(o_ref, u_ref, gvn_ref, ws_ref, bs_ref, w_ref, x_ref, g1_ref, ng_ref, sh_ref, sc_ref, wr_ref,
                xo_ref, hm_ref, aff_ref, *, aw):
    tm, sw = u_ref.shape[1], u_ref.shape[2]
    head_of_lane = lax.broadcasted_iota(jnp.int32, (CHUNK, sw), 1) // (sw // SGU_HEADS)
    gated = []
    for ci in range(tm // CHUNK):
        rows = slice(ci * CHUNK, (ci + 1) * CHUNK)
        r = _dot(ws_ref[...], gvn_ref[0, rows, :])
        s = bs_ref[...]
        for hh in range(SGU_HEADS):
            s = s + jnp.where(head_of_lane == hh, r[hh * CHUNK:(hh + 1) * CHUNK], 0.0)
        gated.append((u_ref[0, rows, :].astype(F32) * s).astype(MXU_DTYPE))
    y = _dot(o_ref[0], w_ref[:aw, :]) + _dot(jnp.concatenate(gated, axis=0), w_ref[aw:, :])
    x = x_ref[0] + g1_ref[0] * y
    xo_ref[0] = x
    h = x * lax.rsqrt(jnp.mean(x * x, axis=-1, keepdims=True) + EPS) * ng_ref[...]
    h = h * (1.0 + sc_ref[0]) + sh_ref[0]
    hm_ref[0] = h.astype(hm_ref.dtype)
    logits = _dot_nt(wr_ref[...], h)
    e = jnp.exp(logits - jnp.max(logits, axis=0, keepdims=True))
    aff_ref[0] = e / jnp.sum(e, axis=0, keepdims=True)


def _outproj(o, u, gvn, ws, bs, w_out, x, g1, ng, sh, sc, wr_t):
    b, n, d = x.shape
    aw = o.shape[2]
    sw = u.shape[2]
    ne = wr_t.shape[0]
    tm = _tile(n, TOKEN_TILE)
    tok = lambda bi, i: (bi, i, 0)
    per_b = lambda bi, i: (bi, 0, 0)
    const2 = lambda bi, i: (0, 0)
    return pl.pallas_call(
        functools.partial(_out_kernel, aw=aw),
        grid=(b, n // tm),
        in_specs=[pl.BlockSpec((1, tm, aw), tok), pl.BlockSpec((1, tm, sw), tok), pl.BlockSpec((1, tm, sw), tok),
                  pl.BlockSpec((SGU_HEADS * CHUNK, CHUNK), const2), pl.BlockSpec((CHUNK, sw), const2),
                  pl.BlockSpec((aw + sw, d), const2), pl.BlockSpec((1, tm, d), tok),
                  pl.BlockSpec((1, 1, d), per_b), pl.BlockSpec((1, d), const2),
                  pl.BlockSpec((1, 1, d), per_b), pl.BlockSpec((1, 1, d), per_b),
                  pl.BlockSpec((ne, d), const2)],
        out_specs=[pl.BlockSpec((1, tm, d), tok), pl.BlockSpec((1, tm, d), tok),
                   pl.BlockSpec((1, ne, tm), lambda bi, i: (bi, 0, i))],
        out_shape=[jax.ShapeDtypeStruct((b, n, d), F32), jax.ShapeDtypeStruct((b, n, d), MXU_DTYPE),
                   jax.ShapeDtypeStruct((b, ne, n), F32)],
        compiler_params=_cparams("parallel", "parallel"), name=f"outproj_{n}",
    )(o, u, gvn, ws, bs, w_out, x, g1, ng, sh, sc, wr_t)


def _route_kernel(aff_ref, tri_ref, pos_ref, start_ref, *, cap, tt):
    a = aff_ref[0]
    ne, n = a.shape
    bits = lax.bitcast_convert_type(a, jnp.int32)
    t = jnp.zeros((ne, 1), jnp.int32)
    for bit in range(30, -1, -1):
        cand = t | (1 << bit)
        cnt = jnp.sum((bits >= cand).astype(jnp.int32), axis=1, keepdims=True)
        t = jnp.where(cnt >= cap, cand, t)
    gt = bits > t
    eq = bits == t
    need = cap - jnp.sum(gt.astype(jnp.int32), axis=1, keepdims=True)

    def exclusive_count(mask):
        out = []
        runs = []
        run = jnp.zeros((ne, 1), F32)
        for ci in range(n // LANES):
            runs.append(run)
            m = mask[:, ci * LANES:(ci + 1) * LANES].astype(F32)
            incl = _dot(m, tri_ref[...])
            out.append(run + incl - m)
            run = run + incl[:, LANES - 1:LANES]
        return jnp.concatenate(out, axis=1), runs

    sel = gt | (eq & (exclusive_count(eq)[0] < need.astype(F32)))
    pos, runs = exclusive_count(sel)
    pos_ref[0] = jnp.where(sel, pos, -1.0)
    lane = lax.broadcasted_iota(jnp.int32, (ne, LANES), 1)
    first = jnp.full((ne, LANES), float(cap), F32)
    for j in range(n // tt):
        first = jnp.where(lane == j, runs[j * tt // LANES], first)
    start_ref[0] = first


def _route(aff_t, tri, *, cap, tt):
    b, ne, n = aff_t.shape
    rows = b * ne
    pos, first = pl.pallas_call(
        functools.partial(_route_kernel, cap=cap, tt=tt),
        grid=(1,),
        in_specs=[pl.BlockSpec((1, rows, n), lambda i: (0, 0, 0)), pl.BlockSpec((LANES, LANES), lambda i: (0, 0))],
        out_specs=[pl.BlockSpec((1, rows, n), lambda i: (0, 0, 0)), pl.BlockSpec((1, rows, LANES), lambda i: (0, 0, 0))],
        out_shape=[jax.ShapeDtypeStruct((1, rows, n), F32), jax.ShapeDtypeStruct((1, rows, LANES), F32)],
        compiler_params=_cparams("arbitrary"), name=f"route_{n}",
    )(aff_t.reshape(1, rows, n), tri)
    return pos.reshape(b, ne, n), first.reshape(b, ne, LANES)


def _window_hits(a_ref, pos_ref, *, win):
    bi, t, nt = pl.program_id(0), pl.program_id(1), pl.num_programs(1)
    ne, tt = pos_ref.shape[1], pos_ref.shape[2]
    base = lax.broadcasted_iota(jnp.int32, (win, tt), 0)
    offs, hits = [], []
    for e in range(ne):
        a_e = a_ref[(bi * ne + e) * nt + t]
        offs.append(a_e)
        hits.append(pos_ref[0, e:e + 1, :] == (base + a_e).astype(F32))
    return offs, hits


def _gather_win_kernel(a_ref, pos_ref, aff_ref, hm_ref, xs_ref, gs_ref, *, cap, win):
    @pl.when(pl.program_id(1) == 0)
    def _():
        xs_ref[...] = jnp.zeros_like(xs_ref)
        gs_ref[...] = jnp.zeros_like(gs_ref)

    ne = pos_ref.shape[1]
    offs, hits = _window_hits(a_ref, pos_ref, win=win)
    group = 8
    for e0 in range(0, ne, group):
        onehot = jnp.concatenate([h.astype(MXU_DTYPE) for h in hits[e0:e0 + group]], axis=0)
        picked = _dot(onehot, hm_ref[0]).astype(xs_ref.dtype)
        for e in range(e0, min(e0 + group, ne)):
            rows = pl.ds(pl.multiple_of(e * cap + offs[e], PACKED_ROWS), win)
            gate = jnp.sum(jnp.where(hits[e], aff_ref[0, e:e + 1, :], 0.0), axis=1, keepdims=True)
            xs_ref[0, rows, :] += picked[(e - e0) * win:(e - e0 + 1) * win]
            gs_ref[0, rows, :] += jnp.broadcast_to(gate, (win, LANES))


def _gather_win(a_flat, pos, aff_t, hm, *, cap, win, tt):
    b, ne, n = pos.shape
    d = hm.shape[2]
    tile = lambda bi, t, a: (bi, 0, t)
    whole = lambda bi, t, a: (bi, 0, 0)
    return pl.pallas_call(
        functools.partial(_gather_win_kernel, cap=cap, win=win),
        grid_spec=pltpu.PrefetchScalarGridSpec(
            num_scalar_prefetch=1, grid=(b, n // tt),
            in_specs=[pl.BlockSpec((1, ne, tt), tile), pl.BlockSpec((1, ne, tt), tile),
                      pl.BlockSpec((1, tt, d), lambda bi, t, a: (bi, t, 0))],
            out_specs=[pl.BlockSpec((1, ne * cap, d), whole), pl.BlockSpec((1, ne * cap, LANES), whole)]),
        out_shape=[jax.ShapeDtypeStruct((b, ne * cap, d), MXU_DTYPE),
                   jax.ShapeDtypeStruct((b, ne * cap, LANES), F32)],
        compiler_params=_cparams("parallel", "arbitrary"), name=f"gather_win_{n}",
    )(a_flat, pos, aff_t, hm)


def _gather_kernel(pos_ref, aff_ref, hm_ref, xs_ref, gs_ref, *, cap, tn):
    n = hm_ref.shape[1]
    d = hm_ref.shape[2]
    slot = lax.broadcasted_iota(jnp.int32, (cap, tn), 0).astype(F32)
    xs = jnp.zeros((cap, d), F32)
    gs = jnp.zeros((cap, 1), F32)
    for ci in range(n // tn):
        cols = slice(ci * tn, (ci + 1) * tn)
        hit = pos_ref[0, 0, :, cols] == slot
        xs = xs + _dot(hit.astype(MXU_DTYPE), hm_ref[0, cols, :])
        gs = gs + jnp.sum(jnp.where(hit, aff_ref[0, 0, :, cols], 0.0), axis=1, keepdims=True)
    xs_ref[0] = xs.astype(xs_ref.dtype)
    gs_ref[0] = jnp.broadcast_to(gs, (cap, LANES))


def _gather(pos, aff_t, hm, *, cap):
    b, ne, n = pos.shape
    d = hm.shape[2]
    tn = _tile(n, TOKEN_TILE)
    pos4 = pos.reshape(b, ne, 1, n)
    aff4 = aff_t.reshape(b, ne, 1, n)
    return pl.pallas_call(
        functools.partial(_gather_kernel, cap=cap, tn=tn),
        grid=(b, ne),
        in_specs=[pl.BlockSpec((1, 1, 1, n), lambda bi, e: (bi, e, 0, 0)),
                  pl.BlockSpec((1, 1, 1, n), lambda bi, e: (bi, e, 0, 0)),
                  pl.BlockSpec((1, n, d), lambda bi, e: (bi, 0, 0))],
        out_specs=[pl.BlockSpec((1, cap, d), lambda bi, e: (bi, e, 0)),
                   pl.BlockSpec((1, cap, LANES), lambda bi, e: (bi, e, 0))],
        out_shape=[jax.ShapeDtypeStruct((b, ne * cap, d), MXU_DTYPE),
                   jax.ShapeDtypeStruct((b, ne * cap, LANES), F32)],
        compiler_params=_cparams("parallel", "parallel"), name=f"gather_{n}",
    )(pos4, aff4, hm)


def _ffn_kernel(x_ref, wg_ref, wu_ref, wd_ref, gs_ref, y_ref, *, nf):
    f = pl.program_id(2)
    _, tm, d = y_ref.shape

    def partial_out():
        x = x_ref[...].reshape(tm, d)
        hid = _silu(_dot(x, wg_ref[0, 0])) * _dot(x, wu_ref[0, 0])
        return _dot(hid, wd_ref[0, 0])

    def finish(total):
        y_ref[0] = total * gs_ref[...].reshape(tm, LANES)[:, :1]

    if nf == 1:
        finish(partial_out())
        return

    @pl.when(f == 0)
    def _():
        y_ref[0] = partial_out()

    if nf > 2:
        @pl.when((f > 0) & (f < nf - 1))
        def _():
            y_ref[0] += partial_out()

    @pl.when(f == nf - 1)
    def _():
        finish(y_ref[0] + partial_out())


def _ffn(xs, w_gate, w_up, w_down, gs, *, layer, cap):
    b, rows, d = xs.shape
    _, ne, _, ff = w_gate.shape
    bt = b
    while bt * cap > FFN_ROWS and bt % 2 == 0:
        bt //= 2
    tm = bt * cap
    tf = _tile(ff, FFN_HIDDEN_TILE)
    return pl.pallas_call(
        functools.partial(_ffn_kernel, nf=ff // tf),
        grid=(ne, b // bt, ff // tf),
        in_specs=[pl.BlockSpec((bt, 1, cap, d), lambda e, i, f: (i, e, 0, 0)),
                  pl.BlockSpec((1, 1, d, tf), lambda e, i, f: (layer, e, 0, f)),
                  pl.BlockSpec((1, 1, d, tf), lambda e, i, f: (layer, e, 0, f)),
                  pl.BlockSpec((1, 1, tf, d), lambda e, i, f: (layer, e, f, 0)),
                  pl.BlockSpec((bt, 1, cap, LANES), lambda e, i, f: (i, e, 0, 0))],
        out_specs=pl.BlockSpec((1, tm, d), lambda e, i, f: (e, i, 0)),
        out_shape=jax.ShapeDtypeStruct((ne, b * cap, d), F32),
        compiler_params=_cparams("parallel", "parallel", "arbitrary"), name=f"ffn_{b * cap}",
    )(xs.reshape(b, ne, cap, d), w_gate, w_up, w_down, gs.reshape(b, ne, cap, LANES))


def _combine_win_kernel(a_ref, pos_ref, *refs, win):
    y_refs, o_ref = refs[:-1], refs[-1]
    d = o_ref.shape[2]
    _, hits = _window_hits(a_ref, pos_ref, win=win)
    onehot = jnp.concatenate([h.astype(MXU_DTYPE) for h in hits], axis=0)
    parts = [_split_hi_lo(y_ref[0]) for y_ref in y_refs]
    y_all = jnp.concatenate([jnp.concatenate([p[0] for p in parts], axis=0),
                             jnp.concatenate([p[1] for p in parts], axis=0)], axis=1)
    r = _dot_tn(onehot, y_all)
    o_ref[0] = r[:, :d] + r[:, d:]


def _combine_win(a_flat, pos, yg, *, cap, win, tt):
    b, ne, n = pos.shape
    d = yg.shape[2]
    nt = n // tt

    def window(e):
        return pl.BlockSpec((pl.Element(1), pl.Element(win), pl.Element(d)),
                            lambda bi, t, a: (e, pl.multiple_of(bi * cap + a[(bi * ne + e) * nt + t], PACKED_ROWS), 0))

    return pl.pallas_call(
        functools.partial(_combine_win_kernel, win=win),
        grid_spec=pltpu.PrefetchScalarGridSpec(
            num_scalar_prefetch=1, grid=(b, nt),
            in_specs=[pl.BlockSpec((1, ne, tt), lambda bi, t, a: (bi, 0, t))] + [window(e) for e in range(ne)],
            out_specs=pl.BlockSpec((1, tt, d), lambda bi, t, a: (bi, t, 0))),
        out_shape=jax.ShapeDtypeStruct((b, n, d), F32),
        compiler_params=_cparams("parallel", "parallel"), name=f"combine_win_{n}",
    )(a_flat, pos, *([yg] * ne))


def _combine_kernel(pos_ref, y_ref, o_ref, *, cap, tn):
    e = pl.program_id(1)

    @pl.when(e == 0)
    def _():
        o_ref[...] = jnp.zeros_like(o_ref)

    n = o_ref.shape[1]
    y_hi, y_lo = _split_hi_lo(y_ref[0])
    slot = lax.broadcasted_iota(jnp.int32, (cap, tn), 0).astype(F32)
    for ci in range(n // tn):
        cols = slice(ci * tn, (ci + 1) * tn)
        hit = (pos_ref[0, 0, :, cols] == slot).astype(MXU_DTYPE)
        o_ref[0, cols, :] += _dot_tn(hit, y_hi) + _dot_tn(hit, y_lo)


def _combine(pos, yg, *, cap):
    b, ne, n = pos.shape
    d = yg.shape[2]
    tn = _tile(n, TOKEN_TILE)
    return pl.pallas_call(
        functools.partial(_combine_kernel, cap=cap, tn=tn),
        grid=(b, ne),
        in_specs=[pl.BlockSpec((1, 1, 1, n), lambda bi, e: (bi, e, 0, 0)),
                  pl.BlockSpec((1, cap, d), lambda bi, e: (e, bi, 0))],
        out_specs=pl.BlockSpec((1, n, d), lambda bi, e: (bi, 0, 0)),
        out_shape=jax.ShapeDtypeStruct((b, n, d), F32),
        compiler_params=_cparams("parallel", "arbitrary"), name=f"combine_{n}",
    )(pos.reshape(b, ne, 1, n), yg)


def _final_kernel(x_ref, moe_ref, g2_ref, ng_ref, o_ref):
    x = x_ref[0] + g2_ref[0] * moe_ref[0]
    o_ref[0] = x * lax.rsqrt(jnp.mean(x * x, axis=-1, keepdims=True) + EPS) * ng_ref[...]


def _final(x, moe, g2, ng):
    b, n, d = x.shape
    tm = _tile(n, TOKEN_TILE)
    tok = lambda bi, i: (bi, i, 0)
    return pl.pallas_call(
        _final_kernel,
        grid=(b, n // tm),
        in_specs=[pl.BlockSpec((1, tm, d), tok), pl.BlockSpec((1, tm, d), tok),
                  pl.BlockSpec((1, 1, d), lambda bi, i: (bi, 0, 0)), pl.BlockSpec((1, d), lambda bi, i: (0, 0))],
        out_specs=pl.BlockSpec((1, tm, d), tok),
        out_shape=jax.ShapeDtypeStruct((b, n, d), F32),
        compiler_params=_cparams("parallel", "parallel"), name="final",
    )(x, moe, g2, ng)


def _rope_tables(n, heads):
    rows = n // GRID_W
    row = jnp.repeat(jnp.arange(rows), GRID_W).astype(F32)
    col = jnp.tile(jnp.arange(GRID_W), rows).astype(F32)
    n_freq = HEAD_DIM // 4
    freqs = ROPE_THETA ** (-jnp.arange(n_freq, dtype=F32) / n_freq)
    ang_r = row[:, None] * freqs
    ang_c = col[:, None] * freqs
    ang = jnp.concatenate([ang_r, ang_r, ang_c, ang_c], axis=-1)
    sign = jnp.where((jnp.arange(HEAD_DIM) % ROPE_GROUP) < ROPE_GROUP // 2, -1.0, 1.0).astype(F32)
    return jnp.tile(jnp.cos(ang), (1, 2 * heads)), jnp.tile(jnp.sin(ang) * sign, (1, 2 * heads))


def _moe(hm, aff_t, tri, w_gate, w_up, w_down, layer):
    b, n, _ = hm.shape
    ne = aff_t.shape[1]
    cap = CAPACITY_FACTOR * n // ne
    tt = _tile(n, MOE_TOKEN_TILE)
    nt = n // tt
    win = min(SLOT_WINDOW, cap)
    pos, first = _route(aff_t, tri, cap=cap, tt=tt)
    first = first[:, :, :nt + 1].astype(jnp.int32)
    offs = jnp.minimum(first[:, :, :nt] // PACKED_ROWS * PACKED_ROWS, cap - win)
    fits = jnp.all(first[:, :, 1:] <= offs + win)
    a_flat = offs.reshape(-1)
    xs, gs = lax.cond(fits,
                      lambda: _gather_win(a_flat, pos, aff_t, hm, cap=cap, win=win, tt=tt),
                      lambda: _gather(pos, aff_t, hm, cap=cap))
    yg = _ffn(xs, w_gate, w_up, w_down, gs, layer=layer, cap=cap)
    return lax.cond(fits,
                    lambda: _combine_win(a_flat, pos, yg, cap=cap, win=win, tt=tt),
                    lambda: _combine(pos, yg, cap=cap))


def kernel(x, c, ctx, c_ctx, w_ada, b_ada, norm1_g, norm2_g, w_in, w_out, lambda_q1, lambda_k1, lambda_q2, lambda_k2, subln_g, sgu_norm_g, sgu_w, sgu_b, w_router, w_gate, w_up, w_down, norm_f_g):
    b, n, d = x.shape
    m = ctx.shape[1]
    depth = w_in.shape[0]
    sw = sgu_norm_g.shape[1]
    aw = (w_in.shape[2] - 2 * sw) // 3
    heads = aw // (2 * HEAD_DIM)
    bf = lambda t: t.astype(MXU_DTYPE)

    rows = -(-(b + 1) // 8) * 8
    cc = jnp.concatenate([c, c_ctx[None, :], jnp.zeros((rows - b - 1, d), F32)], axis=0)
    mod = _ada(cc, w_ada, b_ada)

    w_in_all = bf(w_in)
    cos, sin = _rope_tables(n, heads)
    cos_c = jnp.ones((m, aw), F32)
    sin_c = jnp.zeros((m, aw), F32)
    hw_s = sw // SGU_HEADS
    ones_blk = bf(jnp.kron(jnp.eye(SGU_HEADS, dtype=F32), jnp.full((hw_s, hw_s), 1.0 / hw_s, F32)))
    tri = bf(jnp.triu(jnp.ones((LANES, LANES), F32)))

    cx = ctx
    res_x = res_c = None
    for l in range(depth):
        last = l == depth - 1
        lam_init = 0.8 - 0.6 * math.exp(-0.3 * l)
        parts = [mod[l, :, i * d:(i + 1) * d] for i in range(6)]
        sh1, sc1, g1, sh2, sc2, g2 = [t[:b, None, :] for t in parts]
        csh1, csc1, cg1, csh2, csc2, cg2 = [jnp.broadcast_to(t[b][None, None, :], (b, 1, d)) for t in parts]

        w_in_p = w_in_all[l]
        w_out_l = bf(w_out[l])
        wr_t = bf(w_router[l].T)
        ng1, ng2 = norm1_g[l][None, :], norm2_g[l][None, :]
        sng = sgu_norm_g[l][None, :]
        ws = bf(sgu_w[l].reshape(SGU_HEADS * CHUNK, CHUNK))
        bs = jnp.repeat(sgu_b[l].T, hw_s, axis=1)
        lam_params = jnp.zeros((8, LANES), F32).at[:4, :HEAD_DIM].set(
            jnp.stack([lambda_q1[l], lambda_k1[l], lambda_q2[l], lambda_k2[l]]))
        sub_g = subln_g[l][None, :]
        shared = dict(aw=aw, sw=sw)

        x, (q, k, v, u, gvn) = _proj(x, res_x, ng1, sh1, sc1, w_in_p, cos, sin, sng, ones_blk, rope=True, **shared)
        cx, (qc, kc, vc, uc, gvnc) = _proj(cx, res_c, ng1, csh1, csc1, w_in_p, cos_c, sin_c, sng, ones_blk,
                                           rope=False, **shared)
        o = _attn(lam_params, q, [(kc, vc), (k, v)], sub_g, lam_init=lam_init)
        x, hm, aff_t = _outproj(o, u, gvn, ws, bs, w_out_l, x, g1, ng2, sh2, sc2, wr_t)
        res_x = (_moe(hm, aff_t, tri, w_gate, w_up, w_down, l), g2)
        if not last:
            oc = _attn(lam_params, qc, [(kc, vc)], sub_g, lam_init=lam_init)
            cx, hmc, aff_c = _outproj(oc, uc, gvnc, ws, bs, w_out_l, cx, cg1, ng2, csh2, csc2, wr_t)
            res_c = (_moe(hmc, aff_c, tri, w_gate, w_up, w_down, l), cg2)
    return _final(x, res_x[0], res_x[1], norm_f_g[None, :])
```

```python
import functools
import math

import jax
import jax.numpy as jnp
from jax import lax
from jax.experimental import pallas as pl
from jax.experimental.pallas import tpu as pltpu

F32 = jnp.float32
MXU_DTYPE = jnp.bfloat16

EPS = 1e-6
GRID_W = 64
ROPE_THETA = 10000.0
HEAD_DIM = 64
SGU_HEADS = 8
CHUNK = 128
ROPE_GROUP = HEAD_DIM // 2
CAPACITY_FACTOR = 2
LANES = 128
PACKED_ROWS = 16
TOKEN_TILE = 512
ATTN_KEY_CHUNK = 256
ADA_TILE = 1536
FFN_ROWS = 1024
FFN_HIDDEN_TILE = 1024
MOE_TOKEN_TILE = 256
SLOT_WINDOW = 80
VMEM_LIMIT = 56 * 1024 * 1024


def _cparams(*sem):
    return pltpu.CompilerParams(dimension_semantics=sem, vmem_limit_bytes=VMEM_LIMIT)


def _dot(a, b):
    return jnp.dot(a.astype(MXU_DTYPE), b.astype(MXU_DTYPE), preferred_element_type=F32)


def _dot_nt(a, b):
    return lax.dot_general(a.astype(MXU_DTYPE), b.astype(MXU_DTYPE), (((1,), (1,)), ((), ())),
                           preferred_element_type=F32)


def _dot_tn(a, b):
    return lax.dot_general(a.astype(MXU_DTYPE), b.astype(MXU_DTYPE), (((0,), (0,)), ((), ())),
                           preferred_element_type=F32)


def _split_hi_lo(x):
    hi = x.astype(MXU_DTYPE)
    lo = (x - hi.astype(F32)).astype(MXU_DTYPE)
    return hi, lo


def _silu(x):
    return x * jax.nn.sigmoid(x)


def _gelu(x):
    return 0.5 * x * (1.0 + lax.erf(x * (2.0 ** -0.5)))


def _tile(n, pref):
    t = pref
    while t > 8 and n % t:
        t //= 2
    return t if n % t == 0 else n


def _ada_kernel(c_ref, w_ref, b_ref, o_ref):
    o_ref[0] = _dot(_silu(c_ref[...]), w_ref[0]) + b_ref[0]


def _ada(cc, w_ada, b_ada):
    depth, d, n6 = w_ada.shape
    rows = cc.shape[0]
    tn = _tile(n6, ADA_TILE)
    return pl.pallas_call(
        _ada_kernel,
        grid=(depth, n6 // tn),
        in_specs=[pl.BlockSpec((rows, d), lambda l, j: (0, 0)),
                  pl.BlockSpec((1, d, tn), lambda l, j: (l, 0, j)),
                  pl.BlockSpec((1, 1, tn), lambda l, j: (l, 0, j))],
        out_specs=pl.BlockSpec((1, rows, tn), lambda l, j: (l, 0, j)),
        out_shape=jax.ShapeDtypeStruct((depth, rows, n6), F32),
        compiler_params=_cparams("parallel", "parallel"), name="ada",
    )(cc, w_ada, b_ada.reshape(depth, 1, n6))


def _proj_kernel(*refs, has_res, rope, aw, sw):
    if has_res:
        x_ref, moe_ref, g2_ref, *refs = refs
    else:
        x_ref, *refs = refs
    (ng_ref, sh_ref, sc_ref, w_ref, cos_ref, sin_ref, sng_ref, ones_ref, *outs) = refs
    if has_res:
        xo_ref, q_ref, k_ref, v_ref, u_ref, gvn_ref = outs
    else:
        q_ref, k_ref, v_ref, u_ref, gvn_ref = outs

    x = x_ref[0]
    if has_res:
        x = x + g2_ref[0] * moe_ref[0]
        xo_ref[0] = x
    h = x * lax.rsqrt(jnp.mean(x * x, axis=-1, keepdims=True) + EPS) * ng_ref[...]
    h = h * (1.0 + sc_ref[0]) + sh_ref[0]
    p = _dot(h, w_ref[...])

    q = p[:, :aw]
    k = p[:, aw:2 * aw]
    if rope:
        half = ROPE_GROUP // 2
        first = (lax.broadcasted_iota(jnp.int32, q.shape, 1) % ROPE_GROUP) < half
        cos = cos_ref[...]
        sin = sin_ref[...]

        def rot(t):
            return jnp.where(first, pltpu.roll(t, aw - half, 1), pltpu.roll(t, half, 1))

        q = q * cos + rot(q) * sin
        k = k * cos + rot(k) * sin
    q_ref[0] = (q * (HEAD_DIM ** -0.5 * math.log2(math.e))).astype(q_ref.dtype)
    k_ref[0] = k.astype(k_ref.dtype)
    v_ref[0] = p[:, 2 * aw:3 * aw].astype(v_ref.dtype)

    u_ref[0] = _gelu(p[:, 3 * aw:3 * aw + sw]).astype(u_ref.dtype)
    gv = _gelu(p[:, 3 * aw + sw:])
    sq_hi, sq_lo = _split_hi_lo(gv * gv)
    ms = _dot(sq_hi, ones_ref[...]) + _dot(sq_lo, ones_ref[...])
    gvn_ref[0] = (gv * lax.rsqrt(ms + EPS) * sng_ref[...]).astype(gvn_ref.dtype)


def _proj(x, res, ng, sh, sc, w_in_p, cos, sin, sng, ones_blk, *, rope, aw, sw):
    b, n, d = x.shape
    pw = w_in_p.shape[1]
    tm = _tile(n, TOKEN_TILE)
    has_res = res is not None
    tok = lambda bi, i: (bi, i, 0)
    per_b = lambda bi, i: (bi, 0, 0)
    const2 = lambda bi, i: (0, 0)
    in_specs = [pl.BlockSpec((1, tm, d), tok)]
    args = [x]
    if has_res:
        moe, g2 = res
        in_specs += [pl.BlockSpec((1, tm, d), tok), pl.BlockSpec((1, 1, d), per_b)]
        args += [moe, g2]
    in_specs += [pl.BlockSpec((1, d), const2), pl.BlockSpec((1, 1, d), per_b), pl.BlockSpec((1, 1, d), per_b),
                 pl.BlockSpec((d, pw), const2),
                 pl.BlockSpec((tm, aw), lambda bi, i: (i, 0)), pl.BlockSpec((tm, aw), lambda bi, i: (i, 0)),
                 pl.BlockSpec((1, sw), const2), pl.BlockSpec((sw, sw), const2)]
    args += [ng, sh, sc, w_in_p, cos, sin, sng, ones_blk]
    out_specs = [pl.BlockSpec((1, tm, aw), tok)] * 3 + [pl.BlockSpec((1, tm, sw), tok)] * 2
    out_shape = [jax.ShapeDtypeStruct((b, n, aw), MXU_DTYPE)] * 3 + [jax.ShapeDtypeStruct((b, n, sw), MXU_DTYPE)] * 2
    if has_res:
        out_specs = [pl.BlockSpec((1, tm, d), tok)] + out_specs
        out_shape = [jax.ShapeDtypeStruct((b, n, d), F32)] + out_shape
    outs = pl.pallas_call(
        functools.partial(_proj_kernel, has_res=has_res, rope=rope, aw=aw, sw=sw),
        grid=(b, n // tm),
        in_specs=in_specs, out_specs=out_specs, out_shape=out_shape,
        compiler_params=_cparams("parallel", "parallel"), name="proj_lat" if rope else "proj_ctx",
    )(*args)
    if has_res:
        return outs[0], outs[1:]
    return x, outs


def _attn_kernel(lam_ref, q0_ref, q1_ref, *refs, tq, tkc, lam_init):
    *key_refs, g_ref, o_ref, s0_ref, s1_ref, m0_ref, m1_ref = refs
    sources = [key_refs[i:i + 3] for i in range(0, len(key_refs), 3)]
    chunks, col = [], 0
    for k0_ref, k1_ref, v_ref in sources:
        for r0 in range(0, v_ref.shape[1], tkc):
            chunks.append(((k0_ref, k1_ref), v_ref, r0, col))
            col += tkc
    q_ref = (q0_ref, q1_ref)
    s_ref, m_ref = (s0_ref, s1_ref), (m0_ref, m1_ref)
    hw = q0_ref.shape[2]
    nt = q0_ref.shape[1] // tq
    lane = lax.broadcasted_iota(jnp.int32, (tq, hw), 1)
    mine = (lane >= HEAD_DIM) == (pl.program_id(1) % 2 == 1)
    lp = lam_ref[...]
    lam = (jnp.exp(jnp.sum(lp[0:1] * lp[1:2], axis=-1, keepdims=True))
           - jnp.exp(jnp.sum(lp[2:3] * lp[3:4], axis=-1, keepdims=True)) + lam_init)

    def rows(t):
        return pl.ds(t * tq if isinstance(t, int) else pl.multiple_of(t * tq, tq), tq)

    def phase(t, slot, *, do_scores=True, do_values=True):
        other = 1 - slot
        if do_scores:
            qc = []
            for c in range(2):
                q = q_ref[c][0, rows(t + 1), :]
                qc.append(jnp.where(mine, q, jnp.zeros_like(q)))
            run = [None, None]
        if do_values:
            top = [jnp.max(m_ref[slot][c], axis=-1, keepdims=True) for c in range(2)]
            acc = [jnp.zeros((tq, 2 * hw), F32), jnp.zeros((tq, 2 * hw), F32)]
        for k_ref, v_ref, r0, col in chunks:
            cols = slice(col, col + tkc)
            if do_values:
                vals = v_ref[0, r0:r0 + tkc, :]
                vals1 = jnp.concatenate([vals, jnp.ones_like(vals)], axis=1)
            for c in range(2):
                if do_scores:
                    s = _dot_nt(qc[c], k_ref[c][0, r0:r0 + tkc, :])
                    s_ref[other][c, :, cols] = s
                    for j in range(tkc // LANES):
                        part = s[:, j * LANES:(j + 1) * LANES]
                        run[c] = part if run[c] is None else jnp.maximum(run[c], part)
                if do_values:
                    p = jnp.exp2(s_ref[slot][c, :, cols] - top[c])
                    acc[c] = acc[c] + _dot(p, vals1)
        if do_scores:
            for c in range(2):
                m_ref[other][c] = run[c]
        if do_values:
            o = acc[0][:, :hw] / acc[0][:, hw:] - lam * (acc[1][:, :hw] / acc[1][:, hw:])
            o = o * lax.rsqrt(jnp.mean(o * o, axis=-1, keepdims=True) + EPS) * g_ref[...] * (1.0 - lam_init)
            o_ref[0, rows(t), :] = o.astype(o_ref.dtype)

    phase(-1, 1, do_values=False)

    def step(t, carry):
        for slot in range(2):
            @pl.when(t % 2 == slot)
            def _():
                phase(t, slot)
        return carry

    lax.fori_loop(0, nt - 1, step, 0)
    phase(nt - 1, (nt - 1) % 2, do_scores=False)


def _attn(lam_params, q, kv_sources, subln_g, *, lam_init):
    b, n, aw = q.shape
    lens = [k.shape[1] for k, _ in kv_sources]
    nkeys = sum(lens)
    hw = 2 * HEAD_DIM
    heads = aw // hw
    pairs = heads // 2
    tq = _tile(n, TOKEN_TILE)
    tkc = min(_tile(m, ATTN_KEY_CHUNK) for m in lens)
    comp0 = lambda bi, h: (bi, 0, h // 2)
    comp1 = lambda bi, h: (bi, 0, pairs + h // 2)
    key_specs, key_args = [], []
    for (k, v), m in zip(kv_sources, lens):
        key_specs += [pl.BlockSpec((1, m, hw), comp0), pl.BlockSpec((1, m, hw), comp1),
                      pl.BlockSpec((1, m, hw), lambda bi, h: (bi, 0, h))]
        key_args += [k, k, v]
    return pl.pallas_call(
        functools.partial(_attn_kernel, tq=tq, tkc=tkc, lam_init=lam_init),
        grid=(b, heads),
        in_specs=[pl.BlockSpec(lam_params.shape, lambda bi, h: (0, 0)),
                  pl.BlockSpec((1, n, hw), comp0), pl.BlockSpec((1, n, hw), comp1), *key_specs,
                  pl.BlockSpec((1, hw), lambda bi, h: (0, 0))],
        out_specs=pl.BlockSpec((1, n, hw), lambda bi, h: (bi, 0, h)),
        out_shape=jax.ShapeDtypeStruct((b, n, aw), MXU_DTYPE),
        scratch_shapes=[pltpu.VMEM((2, tq, nkeys), F32)] * 2 + [pltpu.VMEM((2, tq, LANES), F32)] * 2,
        compiler_params=_cparams("parallel", "parallel"), name=f"attn_{n}",
    )(lam_params, q, q, *key_args, subln_g)


def _out_kernel(o_ref, u_ref, gvn_ref, ws_ref, bs_ref, w_ref, x_ref, g1_ref, ng_ref, sh_ref, sc_ref, wr_ref,
                xo_ref, hm_ref, aff_ref, *, aw):
    tm, sw = u_ref.shape[1], u_ref.shape[2]
    head_of_lane = lax.broadcasted_iota(jnp.int32, (CHUNK, sw), 1) // (sw // SGU_HEADS)
    gated = []
    for ci in range(tm // CHUNK):
        rows = slice(ci * CHUNK, (ci + 1) * CHUNK)
        r = _dot(ws_ref[...], gvn_ref[0, rows, :])
        s = bs_ref[...]
        for hh in range(SGU_HEADS):
            s = s + jnp.where(head_of_lane == hh, r[hh * CHUNK:(hh + 1) * CHUNK], 0.0)
        gated.append((u_ref[0, rows, :].astype(F32) * s).astype(MXU_DTYPE))
    y = _dot(o_ref[0], w_ref[:aw, :]) + _dot(jnp.concatenate(gated, axis=0), w_ref[aw:, :])
    x = x_ref[0] + g1_ref[0] * y
    xo_ref[0] = x
    h = x * lax.rsqrt(jnp.mean(x * x, axis=-1, keepdims=True) + EPS) * ng_ref[...]
    h = h * (1.0 + sc_ref[0]) + sh_ref[0]
    hm_ref[0] = h.astype(hm_ref.dtype)
    logits = _dot_nt(wr_ref[...], h)
    e = jnp.exp(logits - jnp.max(logits, axis=0, keepdims=True))
    aff_ref[0] = e / jnp.sum(e, axis=0, keepdims=True)


def _outproj(o, u, gvn, ws, bs, w_out, x, g1, ng, sh, sc, wr_t):
    b, n, d = x.shape
    aw = o.shape[2]
    sw = u.shape[2]
    ne = wr_t.shape[0]
    tm = _tile(n, TOKEN_TILE)
    tok = lambda bi, i: (bi, i, 0)
    per_b = lambda bi, i: (bi, 0, 0)
    const2 = lambda bi, i: (0, 0)
    return pl.pallas_call(
        functools.partial(_out_kernel, aw=aw),
        grid=(b, n // tm),
        in_specs=[pl.BlockSpec((1, tm, aw), tok), pl.BlockSpec((1, tm, sw), tok), pl.BlockSpec((1, tm, sw), tok),
                  pl.BlockSpec((SGU_HEADS * CHUNK, CHUNK), const2), pl.BlockSpec((CHUNK, sw), const2),
                  pl.BlockSpec((aw + sw, d), const2), pl.BlockSpec((1, tm, d), tok),
                  pl.BlockSpec((1, 1, d), per_b), pl.BlockSpec((1, d), const2),
                  pl.BlockSpec((1, 1, d), per_b), pl.BlockSpec((1, 1, d), per_b),
                  pl.BlockSpec((ne, d), const2)],
        out_specs=[pl.BlockSpec((1, tm, d), tok), pl.BlockSpec((1, tm, d), tok),
                   pl.BlockSpec((1, ne, tm), lambda bi, i: (bi, 0, i))],
        out_shape=[jax.ShapeDtypeStruct((b, n, d), F32), jax.ShapeDtypeStruct((b, n, d), MXU_DTYPE),
                   jax.ShapeDtypeStruct((b, ne, n), F32)],
        compiler_params=_cparams("parallel", "parallel"), name=f"outproj_{n}",
    )(o, u, gvn, ws, bs, w_out, x, g1, ng, sh, sc, wr_t)


def _route_kernel(aff_ref, tri_ref, pos_ref, start_ref, *, cap, tt):
    a = aff_ref[0]
    ne, n = a.shape
    bits = lax.bitcast_convert_type(a, jnp.int32)
    t = jnp.zeros((ne, 1), jnp.int32)
    for bit in range(30, -1, -1):
        cand = t | (1 << bit)
        cnt = jnp.sum((bits >= cand).astype(jnp.int32), axis=1, keepdims=True)
        t = jnp.where(cnt >= cap, cand, t)
    gt = bits > t
    eq = bits == t
    need = cap - jnp.sum(gt.astype(jnp.int32), axis=1, keepdims=True)

    def exclusive_count(mask):
        out = []
        runs = []
        run = jnp.zeros((ne, 1), F32)
        for ci in range(n // LANES):
            runs.append(run)
            m = mask[:, ci * LANES:(ci + 1) * LANES].astype(F32)
            incl = _dot(m, tri_ref[...])
            out.append(run + incl - m)
            run = run + incl[:, LANES - 1:LANES]
        return jnp.concatenate(out, axis=1), runs

    sel = gt | (eq & (exclusive_count(eq)[0] < need.astype(F32)))
    pos, runs = exclusive_count(sel)
    pos_ref[0] = jnp.where(sel, pos, -1.0)
    lane = lax.broadcasted_iota(jnp.int32, (ne, LANES), 1)
    first = jnp.full((ne, LANES), float(cap), F32)
    for j in range(n // tt):
        first = jnp.where(lane == j, runs[j * tt // LANES], first)
    start_ref[0] = first


def _route(aff_t, tri, *, cap, tt):
    b, ne, n = aff_t.shape
    rows = b * ne
    pos, first = pl.pallas_call(
        functools.partial(_route_kernel, cap=cap, tt=tt),
        grid=(1,),
        in_specs=[pl.BlockSpec((1, rows, n), lambda i: (0, 0, 0)), pl.BlockSpec((LANES, LANES), lambda i: (0, 0))],
        out_specs=[pl.BlockSpec((1, rows, n), lambda i: (0, 0, 0)), pl.BlockSpec((1, rows, LANES), lambda i: (0, 0, 0))],
        out_shape=[jax.ShapeDtypeStruct((1, rows, n), F32), jax.ShapeDtypeStruct((1, rows, LANES), F32)],
        compiler_params=_cparams("arbitrary"), name=f"route_{n}",
    )(aff_t.reshape(1, rows, n), tri)
    return pos.reshape(b, ne, n), first.reshape(b, ne, LANES)


def _window_hits(a_ref, pos_ref, *, win):
    bi, t, nt = pl.program_id(0), pl.program_id(1), pl.num_programs(1)
    ne, tt = pos_ref.shape[1], pos_ref.shape[2]
    base = lax.broadcasted_iota(jnp.int32, (win, tt), 0)
    offs, hits = [], []
    for e in range(ne):
        a_e = a_ref[(bi * ne + e) * nt + t]
        offs.append(a_e)
        hits.append(pos_ref[0, e:e + 1, :] == (base + a_e).astype(F32))
    return offs, hits


def _gather_win_kernel(a_ref, pos_ref, aff_ref, hm_ref, xs_ref, gs_ref, *, cap, win):
    @pl.when(pl.program_id(1) == 0)
    def _():
        xs_ref[...] = jnp.zeros_like(xs_ref)
        gs_ref[...] = jnp.zeros_like(gs_ref)

    ne = pos_ref.shape[1]
    offs, hits = _window_hits(a_ref, pos_ref, win=win)
    group = 8
    for e0 in range(0, ne, group):
        onehot = jnp.concatenate([h.astype(MXU_DTYPE) for h in hits[e0:e0 + group]], axis=0)
        picked = _dot(onehot, hm_ref[0]).astype(xs_ref.dtype)
        for e in range(e0, min(e0 + group, ne)):
            rows = pl.ds(pl.multiple_of(e * cap + offs[e], PACKED_ROWS), win)
            gate = jnp.sum(jnp.where(hits[e], aff_ref[0, e:e + 1, :], 0.0), axis=1, keepdims=True)
            xs_ref[0, rows, :] += picked[(e - e0) * win:(e - e0 + 1) * win]
            gs_ref[0, rows, :] += jnp.broadcast_to(gate, (win, LANES))


def _gather_win(a_flat, pos, aff_t, hm, *, cap, win, tt):
    b, ne, n = pos.shape
    d = hm.shape[2]
    tile = lambda bi, t, a: (bi, 0, t)
    whole = lambda bi, t, a: (bi, 0, 0)
    return pl.pallas_call(
        functools.partial(_gather_win_kernel, cap=cap, win=win),
        grid_spec=pltpu.PrefetchScalarGridSpec(
            num_scalar_prefetch=1, grid=(b, n // tt),
            in_specs=[pl.BlockSpec((1, ne, tt), tile), pl.BlockSpec((1, ne, tt), tile),
                      pl.BlockSpec((1, tt, d), lambda bi, t, a: (bi, t, 0))],
            out_specs=[pl.BlockSpec((1, ne * cap, d), whole), pl.BlockSpec((1, ne * cap, LANES), whole)]),
        out_shape=[jax.ShapeDtypeStruct((b, ne * cap, d), MXU_DTYPE),
                   jax.ShapeDtypeStruct((b, ne * cap, LANES), F32)],
        compiler_params=_cparams("parallel", "arbitrary"), name=f"gather_win_{n}",
    )(a_flat, pos, aff_t, hm)


def _gather_kernel(pos_ref, aff_ref, hm_ref, xs_ref, gs_ref, *, cap, tn):
    n = hm_ref.shape[1]
    d = hm_ref.shape[2]
    slot = lax.broadcasted_iota(jnp.int32, (cap, tn), 0).astype(F32)
    xs = jnp.zeros((cap, d), F32)
    gs = jnp.zeros((cap, 1), F32)
    for ci in range(n // tn):
        cols = slice(ci * tn, (ci + 1) * tn)
        hit = pos_ref[0, 0, :, cols] == slot
        xs = xs + _dot(hit.astype(MXU_DTYPE), hm_ref[0, cols, :])
        gs = gs + jnp.sum(jnp.where(hit, aff_ref[0, 0, :, cols], 0.0), axis=1, keepdims=True)
    xs_ref[0] = xs.astype(xs_ref.dtype)
    gs_ref[0] = jnp.broadcast_to(gs, (cap, LANES))


def _gather(pos, aff_t, hm, *, cap):
    b, ne, n = pos.shape
    d = hm.shape[2]
    tn = _tile(n, TOKEN_TILE)
    pos4 = pos.reshape(b, ne, 1, n)
    aff4 = aff_t.reshape(b, ne, 1, n)
    return pl.pallas_call(
        functools.partial(_gather_kernel, cap=cap, tn=tn),
        grid=(b, ne),
        in_specs=[pl.BlockSpec((1, 1, 1, n), lambda bi, e: (bi, e, 0, 0)),
                  pl.BlockSpec((1, 1, 1, n), lambda bi, e: (bi, e, 0, 0)),
                  pl.BlockSpec((1, n, d), lambda bi, e: (bi, 0, 0))],
        out_specs=[pl.BlockSpec((1, cap, d), lambda bi, e: (bi, e, 0)),
                   pl.BlockSpec((1, cap, LANES), lambda bi, e: (bi, e, 0))],
        out_shape=[jax.ShapeDtypeStruct((b, ne * cap, d), MXU_DTYPE),
                   jax.ShapeDtypeStruct((b, ne * cap, LANES), F32)],
        compiler_params=_cparams("parallel", "parallel"), name=f"gather_{n}",
    )(pos4, aff4, hm)


def _ffn_kernel(x_ref, wg_ref, wu_ref, wd_ref, gs_ref, y_ref, *, nf):
    f = pl.program_id(2)
    _, tm, d = y_ref.shape

    def partial_out():
        x = x_ref[...].reshape(tm, d)
        hid = _silu(_dot(x, wg_ref[0, 0])) * _dot(x, wu_ref[0, 0])
        return _dot(hid, wd_ref[0, 0])

    def finish(total):
        y_ref[0] = total * gs_ref[...].reshape(tm, LANES)[:, :1]

    if nf == 1:
        finish(partial_out())
        return

    @pl.when(f == 0)
    def _():
        y_ref[0] = partial_out()

    if nf > 2:
        @pl.when((f > 0) & (f < nf - 1))
        def _():
            y_ref[0] += partial_out()

    @pl.when(f == nf - 1)
    def _():
        finish(y_ref[0] + partial_out())


def _ffn(xs, w_gate, w_up, w_down, gs, *, layer, cap):
    b, rows, d = xs.shape
    _, ne, _, ff = w_gate.shape
    bt = b
    while bt * cap > FFN_ROWS and bt % 2 == 0:
        bt //= 2
    tm = bt * cap
    tf = _tile(ff, FFN_HIDDEN_TILE)
    return pl.pallas_call(
        functools.partial(_ffn_kernel, nf=ff // tf),
        grid=(ne, b // bt, ff // tf),
        in_specs=[pl.BlockSpec((bt, 1, cap, d), lambda e, i, f: (i, e, 0, 0)),
                  pl.BlockSpec((1, 1, d, tf), lambda e, i, f: (layer, e, 0, f)),
                  pl.BlockSpec((1, 1, d, tf), lambda e, i, f: (layer, e, 0, f)),
                  pl.BlockSpec((1, 1, tf, d), lambda e, i, f: (layer, e, f, 0)),
                  pl.BlockSpec((bt, 1, cap, LANES), lambda e, i, f: (i, e, 0, 0))],
        out_specs=pl.BlockSpec((1, tm, d), lambda e, i, f: (e, i, 0)),
        out_shape=jax.ShapeDtypeStruct((ne, b * cap, d), F32),
        compiler_params=_cparams("parallel", "parallel", "arbitrary"), name=f"ffn_{b * cap}",
    )(xs.reshape(b, ne, cap, d), w_gate, w_up, w_down, gs.reshape(b, ne, cap, LANES))


def _combine_win_kernel(a_ref, pos_ref, *refs, win):
    y_refs, o_ref = refs[:-1], refs[-1]
    d = o_ref.shape[2]
    _, hits = _window_hits(a_ref, pos_ref, win=win)
    onehot = jnp.concatenate([h.astype(MXU_DTYPE) for h in hits], axis=0)
    parts = [_split_hi_lo(y_ref[0]) for y_ref in y_refs]
    y_all = jnp.concatenate([jnp.concatenate([p[0] for p in parts], axis=0),
                             jnp.concatenate([p[1] for p in parts], axis=0)], axis=1)
    r = _dot_tn(onehot, y_all)
    o_ref[0] = r[:, :d] + r[:, d:]


def _combine_win(a_flat, pos, yg, *, cap, win, tt):
    b, ne, n = pos.shape
    d = yg.shape[2]
    nt = n // tt

    def window(e):
        return pl.BlockSpec((pl.Element(1), pl.Element(win), pl.Element(d)),
                            lambda bi, t, a: (e, pl.multiple_of(bi * cap + a[(bi * ne + e) * nt + t], PACKED_ROWS), 0))

    return pl.pallas_call(
        functools.partial(_combine_win_kernel, win=win),
        grid_spec=pltpu.PrefetchScalarGridSpec(
            num_scalar_prefetch=1, grid=(b, nt),
            in_specs=[pl.BlockSpec((1, ne, tt), lambda bi, t, a: (bi, 0, t))] + [window(e) for e in range(ne)],
            out_specs=pl.BlockSpec((1, tt, d), lambda bi, t, a: (bi, t, 0))),
        out_shape=jax.ShapeDtypeStruct((b, n, d), F32),
        compiler_params=_cparams("parallel", "parallel"), name=f"combine_win_{n}",
    )(a_flat, pos, *([yg] * ne))


def _combine_kernel(pos_ref, y_ref, o_ref, *, cap, tn):
    e = pl.program_id(1)

    @pl.when(e == 0)
    def _():
        o_ref[...] = jnp.zeros_like(o_ref)

    n = o_ref.shape[1]
    y_hi, y_lo = _split_hi_lo(y_ref[0])
    slot = lax.broadcasted_iota(jnp.int32, (cap, tn), 0).astype(F32)
    for ci in range(n // tn):
        cols = slice(ci * tn, (ci + 1) * tn)
        hit = (pos_ref[0, 0, :, cols] == slot).astype(MXU_DTYPE)
        o_ref[0, cols, :] += _dot_tn(hit, y_hi) + _dot_tn(hit, y_lo)


def _combine(pos, yg, *, cap):
    b, ne, n = pos.shape
    d = yg.shape[2]
    tn = _tile(n, TOKEN_TILE)
    return pl.pallas_call(
        functools.partial(_combine_kernel, cap=cap, tn=tn),
        grid=(b, ne),
        in_specs=[pl.BlockSpec((1, 1, 1, n), lambda bi, e: (bi, e, 0, 0)),
                  pl.BlockSpec((1, cap, d), lambda bi, e: (e, bi, 0))],
        out_specs=pl.BlockSpec((1, n, d), lambda bi, e: (bi, 0, 0)),
        out_shape=jax.ShapeDtypeStruct((b, n, d), F32),
        compiler_params=_cparams("parallel", "arbitrary"), name=f"combine_{n}",
    )(pos.reshape(b, ne, 1, n), yg)


def _final_kernel(x_ref, moe_ref, g2_ref, ng_ref, o_ref):
    x = x_ref[0] + g2_ref[0] * moe_ref[0]
    o_ref[0] = x * lax.rsqrt(jnp.mean(x * x, axis=-1, keepdims=True) + EPS) * ng_ref[...]


def _final(x, moe, g2, ng):
    b, n, d = x.shape
    tm = _tile(n, TOKEN_TILE)
    tok = lambda bi, i: (bi, i, 0)
    return pl.pallas_call(
        _final_kernel,
        grid=(b, n // tm),
        in_specs=[pl.BlockSpec((1, tm, d), tok), pl.BlockSpec((1, tm, d), tok),
                  pl.BlockSpec((1, 1, d), lambda bi, i: (bi, 0, 0)), pl.BlockSpec((1, d), lambda bi, i: (0, 0))],
        out_specs=pl.BlockSpec((1, tm, d), tok),
        out_shape=jax.ShapeDtypeStruct((b, n, d), F32),
        compiler_params=_cparams("parallel", "parallel"), name="final",
    )(x, moe, g2, ng)


def _rope_tables(n, heads):
    rows = n // GRID_W
    row = jnp.repeat(jnp.arange(rows), GRID_W).astype(F32)
    col = jnp.tile(jnp.arange(GRID_W), rows).astype(F32)
    n_freq = HEAD_DIM // 4
    freqs = ROPE_THETA ** (-jnp.arange(n_freq, dtype=F32) / n_freq)
    ang_r = row[:, None] * freqs
    ang_c = col[:, None] * freqs
    ang = jnp.concatenate([ang_r, ang_r, ang_c, ang_c], axis=-1)
    sign = jnp.where((jnp.arange(HEAD_DIM) % ROPE_GROUP) < ROPE_GROUP // 2, -1.0, 1.0).astype(F32)
    return jnp.tile(jnp.cos(ang), (1, 2 * heads)), jnp.tile(jnp.sin(ang) * sign, (1, 2 * heads))


def _moe(hm, aff_t, tri, w_gate, w_up, w_down, layer):
    b, n, _ = hm.shape
    ne = aff_t.shape[1]
    cap = CAPACITY_FACTOR * n // ne
    tt = _tile(n, MOE_TOKEN_TILE)
    nt = n // tt
    win = min(SLOT_WINDOW, cap)
    pos, first = _route(aff_t, tri, cap=cap, tt=tt)
    first = first[:, :, :nt + 1].astype(jnp.int32)
    offs = jnp.minimum(first[:, :, :nt] // PACKED_ROWS * PACKED_ROWS, cap - win)
    fits = jnp.all(first[:, :, 1:] <= offs + win)
    a_flat = offs.reshape(-1)
    xs, gs = lax.cond(fits,
                      lambda: _gather_win(a_flat, pos, aff_t, hm, cap=cap, win=win, tt=tt),
                      lambda: _gather(pos, aff_t, hm, cap=cap))
    yg = _ffn(xs, w_gate, w_up, w_down, gs, layer=layer, cap=cap)
    return lax.cond(fits,
                    lambda: _combine_win(a_flat, pos, yg, cap=cap, win=win, tt=tt),
                    lambda: _combine(pos, yg, cap=cap))


def kernel(x, c, ctx, c_ctx, w_ada, b_ada, norm1_g, norm2_g, w_in, w_out, lambda_q1, lambda_k1, lambda_q2, lambda_k2, subln_g, sgu_norm_g, sgu_w, sgu_b, w_router, w_gate, w_up, w_down, norm_f_g):
    b, n, d = x.shape
    m = ctx.shape[1]
    depth = w_in.shape[0]
    sw = sgu_norm_g.shape[1]
    aw = (w_in.shape[2] - 2 * sw) // 3
    heads = aw // (2 * HEAD_DIM)
    ne = w_router.shape[2]
    bf = lambda t: t.astype(MXU_DTYPE)
    assert aw == heads * 2 * HEAD_DIM and heads % 2 == 0, "q/k blocks pair two heads per 128 lanes"
    assert d % LANES == 0 and sw % LANES == 0 and sw % SGU_HEADS == 0
    assert n % GRID_W == 0, "latent tokens form whole rows of the rope grid"
    for tokens in (n, m):
        assert tokens % CHUNK == 0, "token counts are whole spatial-gating chunks"
        assert (CAPACITY_FACTOR * tokens) % (ne * PACKED_ROWS) == 0, "expert capacity is a whole number of packed rows"

    rows = -(-(b + 1) // 8) * 8
    cc = jnp.concatenate([c, c_ctx[None, :], jnp.zeros((rows - b - 1, d), F32)], axis=0)
    mod = _ada(cc, w_ada, b_ada)

    w_in_all = bf(w_in)
    cos, sin = _rope_tables(n, heads)
    cos_c = jnp.ones((m, aw), F32)
    sin_c = jnp.zeros((m, aw), F32)
    hw_s = sw // SGU_HEADS
    ones_blk = bf(jnp.kron(jnp.eye(SGU_HEADS, dtype=F32), jnp.full((hw_s, hw_s), 1.0 / hw_s, F32)))
    tri = bf(jnp.triu(jnp.ones((LANES, LANES), F32)))

    cx = ctx
    res_x = res_c = None
    for l in range(depth):
        last = l == depth - 1
        lam_init = 0.8 - 0.6 * math.exp(-0.3 * l)
        parts = [mod[l, :, i * d:(i + 1) * d] for i in range(6)]
        sh1, sc1, g1, sh2, sc2, g2 = [t[:b, None, :] for t in parts]
        csh1, csc1, cg1, csh2, csc2, cg2 = [jnp.broadcast_to(t[b][None, None, :], (b, 1, d)) for t in parts]

        w_in_p = w_in_all[l]
        w_out_l = bf(w_out[l])
        wr_t = bf(w_router[l].T)
        ng1, ng2 = norm1_g[l][None, :], norm2_g[l][None, :]
        sng = sgu_norm_g[l][None, :]
        ws = bf(sgu_w[l].reshape(SGU_HEADS * CHUNK, CHUNK))
        bs = jnp.repeat(sgu_b[l].T, hw_s, axis=1)
        lam_params = jnp.zeros((8, LANES), F32).at[:4, :HEAD_DIM].set(
            jnp.stack([lambda_q1[l], lambda_k1[l], lambda_q2[l], lambda_k2[l]]))
        sub_g = subln_g[l][None, :]
        shared = dict(aw=aw, sw=sw)

        x, (q, k, v, u, gvn) = _proj(x, res_x, ng1, sh1, sc1, w_in_p, cos, sin, sng, ones_blk, rope=True, **shared)
        cx, (qc, kc, vc, uc, gvnc) = _proj(cx, res_c, ng1, csh1, csc1, w_in_p, cos_c, sin_c, sng, ones_blk,
                                           rope=False, **shared)
        o = _attn(lam_params, q, [(kc, vc), (k, v)], sub_g, lam_init=lam_init)
        x, hm, aff_t = _outproj(o, u, gvn, ws, bs, w_out_l, x, g1, ng2, sh2, sc2, wr_t)
        res_x = (_moe(hm, aff_t, tri, w_gate, w_up, w_down, l), g2)
        if not last:
            oc = _attn(lam_params, qc, [(kc, vc)], sub_g, lam_init=lam_init)
            cx, hmc, aff_c = _outproj(oc, uc, gvnc, ws, bs, w_out_l, cx, cg1, ng2, csh2, csc2, wr_t)
            res_c = (_moe(hmc, aff_c, tri, w_gate, w_up, w_down, l), cg2)
    return _final(x, res_x[0], res_x[1], norm_f_g[None, :])
```

```python
import functools
import math

import jax
import jax.numpy as jnp
from jax import lax
from jax.experimental import pallas as pl
from jax.experimental.pallas import tpu as pltpu

F32 = jnp.float32
MXU_DTYPE = jnp.bfloat16

EPS = 1e-6
GRID_W = 64
ROPE_THETA = 10000.0
HEAD_DIM = 64
SGU_HEADS = 8
CHUNK = 128
ROPE_GROUP = HEAD_DIM // 2
CAPACITY_FACTOR = 2
LANES = 128
PACKED_ROWS = 16
TOKEN_TILE = 512
ATTN_KEY_CHUNK = 256
ADA_TILE = 1536
FFN_ROWS = 1024
FFN_HIDDEN_TILE = 1024
MOE_TOKEN_TILE = 256
SLOT_WINDOW = 80
VMEM_LIMIT = 56 * 1024 * 1024


def _cparams(*sem):
    return pltpu.CompilerParams(dimension_semantics=sem, vmem_limit_bytes=VMEM_LIMIT)


def _dot(a, b):
    return jnp.dot(a.astype(MXU_DTYPE), b.astype(MXU_DTYPE), preferred_element_type=F32)


def _dot_nt(a, b):
    return lax.dot_general(a.astype(MXU_DTYPE), b.astype(MXU_DTYPE), (((1,), (1,)), ((), ())),
                           preferred_element_type=F32)


def _dot_tn(a, b):
    return lax.dot_general(a.astype(MXU_DTYPE), b.astype(MXU_DTYPE), (((0,), (0,)), ((), ())),
                           preferred_element_type=F32)


def _split_hi_lo(x):
    hi = x.astype(MXU_DTYPE)
    lo = (x - hi.astype(F32)).astype(MXU_DTYPE)
    return hi, lo


def _silu(x):
    return x * jax.nn.sigmoid(x)


def _gelu(x):
    return 0.5 * x * (1.0 + lax.erf(x * (2.0 ** -0.5)))


def _tile(n, pref):
    t = pref
    while t > 8 and n % t:
        t //= 2
    return t if n % t == 0 else n


def _ada_kernel(c_ref, w_ref, b_ref, o_ref):
    o_ref[0] = _dot(_silu(c_ref[...]), w_ref[0]) + b_ref[0]


def _ada(cc, w_ada, b_ada):
    depth, d, n6 = w_ada.shape
    rows = cc.shape[0]
    tn = _tile(n6, ADA_TILE)
    return pl.pallas_call(
        _ada_kernel,
        grid=(depth, n6 // tn),
        in_specs=[pl.BlockSpec((rows, d), lambda l, j: (0, 0)),
                  pl.BlockSpec((1, d, tn), lambda l, j: (l, 0, j)),
                  pl.BlockSpec((1, 1, tn), lambda l, j: (l, 0, j))],
        out_specs=pl.BlockSpec((1, rows, tn), lambda l, j: (l, 0, j)),
        out_shape=jax.ShapeDtypeStruct((depth, rows, n6), F32),
        compiler_params=_cparams("parallel", "parallel"), name="ada",
    )(cc, w_ada, b_ada.reshape(depth, 1, n6))


def _proj_kernel(*refs, has_res, rope, aw, sw):
    if has_res:
        x_ref, moe_ref, g2_ref, *refs = refs
    else:
        x_ref, *refs = refs
    (ng_ref, sh_ref, sc_ref, w_ref, cos_ref, sin_ref, sng_ref, ones_ref, *outs) = refs
    if has_res:
        xo_ref, q_ref, k_ref, v_ref, u_ref, gvn_ref = outs
    else:
        q_ref, k_ref, v_ref, u_ref, gvn_ref = outs

    x = x_ref[0]
    if has_res:
        x = x + g2_ref[0] * moe_ref[0]
        xo_ref[0] = x
    h = x * lax.rsqrt(jnp.mean(x * x, axis=-1, keepdims=True) + EPS) * ng_ref[...]
    h = h * (1.0 + sc_ref[0]) + sh_ref[0]
    p = _dot(h, w_ref[...])

    q = p[:, :aw]
    k = p[:, aw:2 * aw]
    if rope:
        half = ROPE_GROUP // 2
        first = (lax.broadcasted_iota(jnp.int32, q.shape, 1) % ROPE_GROUP) < half
        cos = cos_ref[...]
        sin = sin_ref[...]

        def rot(t):
            return jnp.where(first, pltpu.roll(t, aw - half, 1), pltpu.roll(t, half, 1))

        q = q * cos + rot(q) * sin
        k = k * cos + rot(k) * sin
    q_ref[0] = (q * (HEAD_DIM ** -0.5 * math.log2(math.e))).astype(q_ref.dtype)
    k_ref[0] = k.astype(k_ref.dtype)
    v_ref[0] = p[:, 2 * aw:3 * aw].astype(v_ref.dtype)

    u_ref[0] = _gelu(p[:, 3 * aw:3 * aw + sw]).astype(u_ref.dtype)
    gv = _gelu(p[:, 3 * aw + sw:])
    sq_hi, sq_lo = _split_hi_lo(gv * gv)
    ms = _dot(sq_hi, ones_ref[...]) + _dot(sq_lo, ones_ref[...])
    gvn_ref[0] = (gv * lax.rsqrt(ms + EPS) * sng_ref[...]).astype(gvn_ref.dtype)


def _proj(x, res, ng, sh, sc, w_in_p, cos, sin, sng, ones_blk, *, rope, aw, sw):
    b, n, d = x.shape
    pw = w_in_p.shape[1]
    tm = _tile(n, TOKEN_TILE)
    has_res = res is not None
    tok = lambda bi, i: (bi, i, 0)
    per_b = lambda bi, i: (bi, 0, 0)
    const2 = lambda bi, i: (0, 0)
    in_specs = [pl.BlockSpec((1, tm, d), tok)]
    args = [x]
    if has_res:
        moe, g2 = res
        in_specs += [pl.BlockSpec((1, tm, d), tok), pl.BlockSpec((1, 1, d), per_b)]
        args += [moe, g2]
    in_specs += [pl.BlockSpec((1, d), const2), pl.BlockSpec((1, 1, d), per_b), pl.BlockSpec((1, 1, d), per_b),
                 pl.BlockSpec((d, pw), const2),
                 pl.BlockSpec((tm, aw), lambda bi, i: (i, 0)), pl.BlockSpec((tm, aw), lambda bi, i: (i, 0)),
                 pl.BlockSpec((1, sw), const2), pl.BlockSpec((sw, sw), const2)]
    args += [ng, sh, sc, w_in_p, cos, sin, sng, ones_blk]
    out_specs = [pl.BlockSpec((1, tm, aw), tok)] * 3 + [pl.BlockSpec((1, tm, sw), tok)] * 2
    out_shape = [jax.ShapeDtypeStruct((b, n, aw), MXU_DTYPE)] * 3 + [jax.ShapeDtypeStruct((b, n, sw), MXU_DTYPE)] * 2
    if has_res:
        out_specs = [pl.BlockSpec((1, tm, d), tok)] + out_specs
        out_shape = [jax.ShapeDtypeStruct((b, n, d), F32)] + out_shape
    outs = pl.pallas_call(
        functools.partial(_proj_kernel, has_res=has_res, rope=rope, aw=aw, sw=sw),
        grid=(b, n // tm),
        in_specs=in_specs, out_specs=out_specs, out_shape=out_shape,
        compiler_params=_cparams("parallel", "parallel"), name="proj_lat" if rope else "proj_ctx",
    )(*args)
    if has_res:
        return outs[0], outs[1:]
    return x, outs


def _attn_kernel(lam_ref, q0_ref, q1_ref, *refs, tq, tkc, lam_init):
    *key_refs, g_ref, o_ref, s0_ref, s1_ref, m0_ref, m1_ref = refs
    sources = [key_refs[i:i + 3] for i in range(0, len(key_refs), 3)]
    chunks, col = [], 0
    for k0_ref, k1_ref, v_ref in sources:
        for r0 in range(0, v_ref.shape[1], tkc):
            chunks.append(((k0_ref, k1_ref), v_ref, r0, col))
            col += tkc
    q_ref = (q0_ref, q1_ref)
    s_ref, m_ref = (s0_ref, s1_ref), (m0_ref, m1_ref)
    hw = q0_ref.shape[2]
    nt = q0_ref.shape[1] // tq
    lane = lax.broadcasted_iota(jnp.int32, (tq, hw), 1)
    mine = (lane >= HEAD_DIM) == (pl.program_id(1) % 2 == 1)
    lp = lam_ref[...]
    lam = (jnp.exp(jnp.sum(lp[0:1] * lp[1:2], axis=-1, keepdims=True))
           - jnp.exp(jnp.sum(lp[2:3] * lp[3:4], axis=-1, keepdims=True)) + lam_init)

    def rows(t):
        return pl.ds(t * tq if isinstance(t, int) else pl.multiple_of(t * tq, tq), tq)

    def phase(t, slot, *, do_scores=True, do_values=True):
        other = 1 - slot
        if do_scores:
            qc = []
            for c in range(2):
                q = q_ref[c][0, rows(t + 1), :]
                qc.append(jnp.where(mine, q, jnp.zeros_like(q)))
            run = [None, None]
        if do_values:
            top = [jnp.max(m_ref[slot][c], axis=-1, keepdims=True) for c in range(2)]
            acc = [jnp.zeros((tq, 2 * hw), F32), jnp.zeros((tq, 2 * hw), F32)]
        for c in range(2):
            for k_ref, v_ref, r0, col in chunks:
                cols = slice(col, col + tkc)
                if do_values:
                    vals = v_ref[0, r0:r0 + tkc, :]
                    vals1 = jnp.concatenate([vals, jnp.ones_like(vals)], axis=1)
                if do_scores:
                    s = _dot_nt(qc[c], k_ref[c][0, r0:r0 + tkc, :])
                    s_ref[other][c, :, cols] = s
                    for j in range(tkc // LANES):
                        part = s[:, j * LANES:(j + 1) * LANES]
                        run[c] = part if run[c] is None else jnp.maximum(run[c], part)
                if do_values:
                    p = jnp.exp2(s_ref[slot][c, :, cols] - top[c])
                    acc[c] = acc[c] + _dot(p, vals1)
        if do_scores:
            for c in range(2):
                m_ref[other][c] = run[c]
        if do_values:
            o = acc[0][:, :hw] / acc[0][:, hw:] - lam * (acc[1][:, :hw] / acc[1][:, hw:])
            o = o * lax.rsqrt(jnp.mean(o * o, axis=-1, keepdims=True) + EPS) * g_ref[...] * (1.0 - lam_init)
            o_ref[0, rows(t), :] = o.astype(o_ref.dtype)

    phase(-1, 1, do_values=False)

    def step(t, carry):
        for slot in range(2):
            @pl.when(t % 2 == slot)
            def _():
                phase(t, slot)
        return carry

    lax.fori_loop(0, nt - 1, step, 0)
    phase(nt - 1, (nt - 1) % 2, do_scores=False)


def _attn(lam_params, q, kv_sources, subln_g, *, lam_init):
    b, n, aw = q.shape
    lens = [k.shape[1] for k, _ in kv_sources]
    nkeys = sum(lens)
    hw = 2 * HEAD_DIM
    heads = aw // hw
    pairs = heads // 2
    tq = _tile(n, TOKEN_TILE)
    tkc = min(_tile(m, ATTN_KEY_CHUNK) for m in lens)
    comp0 = lambda bi, h: (bi, 0, h // 2)
    comp1 = lambda bi, h: (bi, 0, pairs + h // 2)
    key_specs, key_args = [], []
    for (k, v), m in zip(kv_sources, lens):
        key_specs += [pl.BlockSpec((1, m, hw), comp0), pl.BlockSpec((1, m, hw), comp1),
                      pl.BlockSpec((1, m, hw), lambda bi, h: (bi, 0, h))]
        key_args += [k, k, v]
    return pl.pallas_call(
        functools.partial(_attn_kernel, tq=tq, tkc=tkc, lam_init=lam_init),
        grid=(b, heads),
        in_specs=[pl.BlockSpec(lam_params.shape, lambda bi, h: (0, 0)),
                  pl.BlockSpec((1, n, hw), comp0), pl.BlockSpec((1, n, hw), comp1), *key_specs,
                  pl.BlockSpec((1, hw), lambda bi, h: (0, 0))],
        out_specs=pl.BlockSpec((1, n, hw), lambda bi, h: (bi, 0, h)),
        out_shape=jax.ShapeDtypeStruct((b, n, aw), MXU_DTYPE),
        scratch_shapes=[pltpu.VMEM((2, tq, nkeys), F32)] * 2 + [pltpu.VMEM((2, tq, LANES), F32)] * 2,
        compiler_params=_cparams("parallel", "parallel"), name=f"attn_{n}",
    )(lam_params, q, q, *key_args, subln_g)


def _out_kernel(o_ref, u_ref, gvn_ref, ws_ref, bs_ref, w_ref, x_ref, g1_ref, ng_ref, sh_ref, sc_ref, wr_ref,
                xo_ref, hm_ref, aff_ref, *, aw):
    tm, sw = u_ref.shape[1], u_ref.shape[2]
    head_of_lane = lax.broadcasted_iota(jnp.int32, (CHUNK, sw), 1) // (sw // SGU_HEADS)
    gated = []
    for ci in range(tm // CHUNK):
        rows = slice(ci * CHUNK, (ci + 1) * CHUNK)
        r = _dot(ws_ref[...], gvn_ref[0, rows, :])
        s = bs_ref[...]
        for hh in range(SGU_HEADS):
            s = s + jnp.where(head_of_lane == hh, r[hh * CHUNK:(hh + 1) * CHUNK], 0.0)
        gated.append((u_ref[0, rows, :].astype(F32) * s).astype(MXU_DTYPE))
    y = _dot(o_ref[0], w_ref[:aw, :]) + _dot(jnp.concatenate(gated, axis=0), w_ref[aw:, :])
    x = x_ref[0] + g1_ref[0] * y
    xo_ref[0] = x
    h = x * lax.rsqrt(jnp.mean(x * x, axis=-1, keepdims=True) + EPS) * ng_ref[...]
    h = h * (1.0 + sc_ref[0]) + sh_ref[0]
    hm_ref[0] = h.astype(hm_ref.dtype)
    logits = _dot_nt(wr_ref[...], h)
    e = jnp.exp(logits - jnp.max(logits, axis=0, keepdims=True))
    aff_ref[0] = e / jnp.sum(e, axis=0, keepdims=True)


def _outproj(o, u, gvn, ws, bs, w_out, x, g1, ng, sh, sc, wr_t):
    b, n, d = x.shape
    aw = o.shape[2]
    sw = u.shape[2]
    ne = wr_t.shape[0]
    tm = _tile(n, TOKEN_TILE)
    tok = lambda bi, i: (bi, i, 0)
    per_b = lambda bi, i: (bi, 0, 0)
    const2 = lambda bi, i: (0, 0)
    return pl.pallas_call(
        functools.partial(_out_kernel, aw=aw),
        grid=(b, n // tm),
        in_specs=[pl.BlockSpec((1, tm, aw), tok), pl.BlockSpec((1, tm, sw), tok), pl.BlockSpec((1, tm, sw), tok),
                  pl.BlockSpec((SGU_HEADS * CHUNK, CHUNK), const2), pl.BlockSpec((CHUNK, sw), const2),
                  pl.BlockSpec((aw + sw, d), const2), pl.BlockSpec((1, tm, d), tok),
                  pl.BlockSpec((1, 1, d), per_b), pl.BlockSpec((1, d), const2),
                  pl.BlockSpec((1, 1, d), per_b), pl.BlockSpec((1, 1, d), per_b),
                  pl.BlockSpec((ne, d), const2)],
        out_specs=[pl.BlockSpec((1, tm, d), tok), pl.BlockSpec((1, tm, d), tok),
                   pl.BlockSpec((1, ne, tm), lambda bi, i: (bi, 0, i))],
        out_shape=[jax.ShapeDtypeStruct((b, n, d), F32), jax.ShapeDtypeStruct((b, n, d), MXU_DTYPE),
                   jax.ShapeDtypeStruct((b, ne, n), F32)],
        compiler_params=_cparams("parallel", "parallel"), name=f"outproj_{n}",
    )(o, u, gvn, ws, bs, w_out, x, g1, ng, sh, sc, wr_t)


def _route_kernel(aff_ref, tri_ref, pos_ref, start_ref, *, cap, tt):
    a = aff_ref[0]
    ne, n = a.shape
    bits = lax.bitcast_convert_type(a, jnp.int32)
    t = jnp.zeros((ne, 1), jnp.int32)
    for bit in range(30, -1, -1):
        cand = t | (1 << bit)
        cnt = jnp.sum((bits >= cand).astype(jnp.int32), axis=1, keepdims=True)
        t = jnp.where(cnt >= cap, cand, t)
    gt = bits > t
    eq = bits == t
    need = cap - jnp.sum(gt.astype(jnp.int32), axis=1, keepdims=True)

    def exclusive_count(mask):
        out = []
        runs = []
        run = jnp.zeros((ne, 1), F32)
        for ci in range(n // LANES):
            runs.append(run)
            m = mask[:, ci * LANES:(ci + 1) * LANES].astype(F32)
            incl = _dot(m, tri_ref[...])
            out.append(run + incl - m)
            run = run + incl[:, LANES - 1:LANES]
        return jnp.concatenate(out, axis=1), runs

    sel = gt | (eq & (exclusive_count(eq)[0] < need.astype(F32)))
    pos, runs = exclusive_count(sel)
    pos_ref[0] = jnp.where(sel, pos, -1.0)
    lane = lax.broadcasted_iota(jnp.int32, (ne, LANES), 1)
    first = jnp.full((ne, LANES), float(cap), F32)
    for j in range(n // tt):
        first = jnp.where(lane == j, runs[j * tt // LANES], first)
    start_ref[0] = first


def _route(aff_t, tri, *, cap, tt):
    b, ne, n = aff_t.shape
    rows = b * ne
    pos, first = pl.pallas_call(
        functools.partial(_route_kernel, cap=cap, tt=tt),
        grid=(1,),
        in_specs=[pl.BlockSpec((1, rows, n), lambda i: (0, 0, 0)), pl.BlockSpec((LANES, LANES), lambda i: (0, 0))],
        out_specs=[pl.BlockSpec((1, rows, n), lambda i: (0, 0, 0)), pl.BlockSpec((1, rows, LANES), lambda i: (0, 0, 0))],
        out_shape=[jax.ShapeDtypeStruct((1, rows, n), F32), jax.ShapeDtypeStruct((1, rows, LANES), F32)],
        compiler_params=_cparams("arbitrary"), name=f"route_{n}",
    )(aff_t.reshape(1, rows, n), tri)
    return pos.reshape(b, ne, n), first.reshape(b, ne, LANES)


def _window_hits(a_ref, pos_ref, *, win):
    bi, t, nt = pl.program_id(0), pl.program_id(1), pl.num_programs(1)
    ne, tt = pos_ref.shape[1], pos_ref.shape[2]
    base = lax.broadcasted_iota(jnp.int32, (win, tt), 0)
    offs, hits = [], []
    for e in range(ne):
        a_e = a_ref[(bi * ne + e) * nt + t]
        offs.append(a_e)
        hits.append(pos_ref[0, e:e + 1, :] == (base + a_e).astype(F32))
    return offs, hits


def _gather_win_kernel(a_ref, pos_ref, aff_ref, hm_ref, xs_ref, gs_ref, *, cap, win):
    @pl.when(pl.program_id(1) == 0)
    def _():
        xs_ref[...] = jnp.zeros_like(xs_ref)
        gs_ref[...] = jnp.zeros_like(gs_ref)

    ne = pos_ref.shape[1]
    offs, hits = _window_hits(a_ref, pos_ref, win=win)
    group = 8
    for e0 in range(0, ne, group):
        onehot = jnp.concatenate([h.astype(MXU_DTYPE) for h in hits[e0:e0 + group]], axis=0)
        picked = _dot(onehot, hm_ref[0]).astype(xs_ref.dtype)
        for e in range(e0, min(e0 + group, ne)):
            rows = pl.ds(pl.multiple_of(e * cap + offs[e], PACKED_ROWS), win)
            gate = jnp.sum(jnp.where(hits[e], aff_ref[0, e:e + 1, :], 0.0), axis=1, keepdims=True)
            xs_ref[0, rows, :] += picked[(e - e0) * win:(e - e0 + 1) * win]
            gs_ref[0, rows, :] += jnp.broadcast_to(gate, (win, LANES))


def _gather_win(a_flat, pos, aff_t, hm, *, cap, win, tt):
    b, ne, n = pos.shape
    d = hm.shape[2]
    tile = lambda bi, t, a: (bi, 0, t)
    whole = lambda bi, t, a: (bi, 0, 0)
    return pl.pallas_call(
        functools.partial(_gather_win_kernel, cap=cap, win=win),
        grid_spec=pltpu.PrefetchScalarGridSpec(
            num_scalar_prefetch=1, grid=(b, n // tt),
            in_specs=[pl.BlockSpec((1, ne, tt), tile), pl.BlockSpec((1, ne, tt), tile),
                      pl.BlockSpec((1, tt, d), lambda bi, t, a: (bi, t, 0))],
            out_specs=[pl.BlockSpec((1, ne * cap, d), whole), pl.BlockSpec((1, ne * cap, LANES), whole)]),
        out_shape=[jax.ShapeDtypeStruct((b, ne * cap, d), MXU_DTYPE),
                   jax.ShapeDtypeStruct((b, ne * cap, LANES), F32)],
        compiler_params=_cparams("parallel", "arbitrary"), name=f"gather_win_{n}",
    )(a_flat, pos, aff_t, hm)


def _gather_kernel(pos_ref, aff_ref, hm_ref, xs_ref, gs_ref, *, cap, tn):
    n = hm_ref.shape[1]
    d = hm_ref.shape[2]
    slot = lax.broadcasted_iota(jnp.int32, (cap, tn), 0).astype(F32)
    xs = jnp.zeros((cap, d), F32)
    gs = jnp.zeros((cap, 1), F32)
    for ci in range(n // tn):
        cols = slice(ci * tn, (ci + 1) * tn)
        hit = pos_ref[0, 0, :, cols] == slot
        xs = xs + _dot(hit.astype(MXU_DTYPE), hm_ref[0, cols, :])
        gs = gs + jnp.sum(jnp.where(hit, aff_ref[0, 0, :, cols], 0.0), axis=1, keepdims=True)
    xs_ref[0] = xs.astype(xs_ref.dtype)
    gs_ref[0] = jnp.broadcast_to(gs, (cap, LANES))


def _gather(pos, aff_t, hm, *, cap):
    b, ne, n = pos.shape
    d = hm.shape[2]
    tn = _tile(n, TOKEN_TILE)
    pos4 = pos.reshape(b, ne, 1, n)
    aff4 = aff_t.reshape(b, ne, 1, n)
    return pl.pallas_call(
        functools.partial(_gather_kernel, cap=cap, tn=tn),
        grid=(b, ne),
        in_specs=[pl.BlockSpec((1, 1, 1, n), lambda bi, e: (bi, e, 0, 0)),
                  pl.BlockSpec((1, 1, 1, n), lambda bi, e: (bi, e, 0, 0)),
                  pl.BlockSpec((1, n, d), lambda bi, e: (bi, 0, 0))],
        out_specs=[pl.BlockSpec((1, cap, d), lambda bi, e: (bi, e, 0)),
                   pl.BlockSpec((1, cap, LANES), lambda bi, e: (bi, e, 0))],
        out_shape=[jax.ShapeDtypeStruct((b, ne * cap, d), MXU_DTYPE),
                   jax.ShapeDtypeStruct((b, ne * cap, LANES), F32)],
        compiler_params=_cparams("parallel", "parallel"), name=f"gather_{n}",
    )(pos4, aff4, hm)


def _ffn_kernel(x_ref, wg_ref, wu_ref, wd_ref, gs_ref, y_ref, *, nf):
    f = pl.program_id(2)
    _, tm, d = y_ref.shape

    def partial_out():
        x = x_ref[...].reshape(tm, d)
        hid = _silu(_dot(x, wg_ref[0, 0])) * _dot(x, wu_ref[0, 0])
        return _dot(hid, wd_ref[0, 0])

    def finish(total):
        y_ref[0] = total * gs_ref[...].reshape(tm, LANES)[:, :1]

    if nf == 1:
        finish(partial_out())
        return

    @pl.when(f == 0)
    def _():
        y_ref[0] = partial_out()

    if nf > 2:
        @pl.when((f > 0) & (f < nf - 1))
        def _():
            y_ref[0] += partial_out()

    @pl.when(f == nf - 1)
    def _():
        finish(y_ref[0] + partial_out())


def _ffn(xs, w_gate, w_up, w_down, gs, *, layer, cap):
    b, rows, d = xs.shape
    _, ne, _, ff = w_gate.shape
    bt = b
    while bt * cap > FFN_ROWS and bt % 2 == 0:
        bt //= 2
    tm = bt * cap
    tf = _tile(ff, FFN_HIDDEN_TILE)
    return pl.pallas_call(
        functools.partial(_ffn_kernel, nf=ff // tf),
        grid=(ne, b // bt, ff // tf),
        in_specs=[pl.BlockSpec((bt, 1, cap, d), lambda e, i, f: (i, e, 0, 0)),
                  pl.BlockSpec((1, 1, d, tf), lambda e, i, f: (layer, e, 0, f)),
                  pl.BlockSpec((1, 1, d, tf), lambda e, i, f: (layer, e, 0, f)),
                  pl.BlockSpec((1, 1, tf, d), lambda e, i, f: (layer, e, f, 0)),
                  pl.BlockSpec((bt, 1, cap, LANES), lambda e, i, f: (i, e, 0, 0))],
        out_specs=pl.BlockSpec((1, tm, d), lambda e, i, f: (e, i, 0)),
        out_shape=jax.ShapeDtypeStruct((ne, b * cap, d), F32),
        compiler_params=_cparams("parallel", "parallel", "arbitrary"), name=f"ffn_{b * cap}",
    )(xs.reshape(b, ne, cap, d), w_gate, w_up, w_down, gs.reshape(b, ne, cap, LANES))


def _combine_win_kernel(a_ref, pos_ref, *refs, win):
    y_refs, o_ref = refs[:-1], refs[-1]
    d = o_ref.shape[2]
    _, hits = _window_hits(a_ref, pos_ref, win=win)
    onehot = jnp.concatenate([h.astype(MXU_DTYPE) for h in hits], axis=0)
    parts = [_split_hi_lo(y_ref[0]) for y_ref in y_refs]
    y_all = jnp.concatenate([jnp.concatenate([p[0] for p in parts], axis=0),
                             jnp.concatenate([p[1] for p in parts], axis=0)], axis=1)
    r = _dot_tn(onehot, y_all)
    o_ref[0] = r[:, :d] + r[:, d:]


def _combine_win(a_flat, pos, yg, *, cap, win, tt):
    b, ne, n = pos.shape
    d = yg.shape[2]
    nt = n // tt

    def window(e):
        return pl.BlockSpec((pl.Element(1), pl.Element(win), pl.Element(d)),
                            lambda bi, t, a: (e, pl.multiple_of(bi * cap + a[(bi * ne + e) * nt + t], PACKED_ROWS), 0))

    return pl.pallas_call(
        functools.partial(_combine_win_kernel, win=win),
        grid_spec=pltpu.PrefetchScalarGridSpec(
            num_scalar_prefetch=1, grid=(b, nt),
            in_specs=[pl.BlockSpec((1, ne, tt), lambda bi, t, a: (bi, 0, t))] + [window(e) for e in range(ne)],
            out_specs=pl.BlockSpec((1, tt, d), lambda bi, t, a: (bi, t, 0))),
        out_shape=jax.ShapeDtypeStruct((b, n, d), F32),
        compiler_params=_cparams("parallel", "parallel"), name=f"combine_win_{n}",
    )(a_flat, pos, *([yg] * ne))


def _combine_kernel(pos_ref, y_ref, o_ref, *, cap, tn):
    e = pl.program_id(1)

    @pl.when(e == 0)
    def _():
        o_ref[...] = jnp.zeros_like(o_ref)

    n = o_ref.shape[1]
    y_hi, y_lo = _split_hi_lo(y_ref[0])
    slot = lax.broadcasted_iota(jnp.int32, (cap, tn), 0).astype(F32)
    for ci in range(n // tn):
        cols = slice(ci * tn, (ci + 1) * tn)
        hit = (pos_ref[0, 0, :, cols] == slot).astype(MXU_DTYPE)
        o_ref[0, cols, :] += _dot_tn(hit, y_hi) + _dot_tn(hit, y_lo)


def _combine(pos, yg, *, cap):
    b, ne, n = pos.shape
    d = yg.shape[2]
    tn = _tile(n, TOKEN_TILE)
    return pl.pallas_call(
        functools.partial(_combine_kernel, cap=cap, tn=tn),
        grid=(b, ne),
        in_specs=[pl.BlockSpec((1, 1, 1, n), lambda bi, e: (bi, e, 0, 0)),
                  pl.BlockSpec((1, cap, d), lambda bi, e: (e, bi, 0))],
        out_specs=pl.BlockSpec((1, n, d), lambda bi, e: (bi, 0, 0)),
        out_shape=jax.ShapeDtypeStruct((b, n, d), F32),
        compiler_params=_cparams("parallel", "arbitrary"), name=f"combine_{n}",
    )(pos.reshape(b, ne, 1, n), yg)


def _final_kernel(x_ref, moe_ref, g2_ref, ng_ref, o_ref):
    x = x_ref[0] + g2_ref[0] * moe_ref[0]
    o_ref[0] = x * lax.rsqrt(jnp.mean(x * x, axis=-1, keepdims=True) + EPS) * ng_ref[...]


def _final(x, moe, g2, ng):
    b, n, d = x.shape
    tm = _tile(n, TOKEN_TILE)
    tok = lambda bi, i: (bi, i, 0)
    return pl.pallas_call(
        _final_kernel,
        grid=(b, n // tm),
        in_specs=[pl.BlockSpec((1, tm, d), tok), pl.BlockSpec((1, tm, d), tok),
                  pl.BlockSpec((1, 1, d), lambda bi, i: (bi, 0, 0)), pl.BlockSpec((1, d), lambda bi, i: (0, 0))],
        out_specs=pl.BlockSpec((1, tm, d), tok),
        out_shape=jax.ShapeDtypeStruct((b, n, d), F32),
        compiler_params=_cparams("parallel", "parallel"), name="final",
    )(x, moe, g2, ng)


def _rope_tables(n, heads):
    rows = n // GRID_W
    row = jnp.repeat(jnp.arange(rows), GRID_W).astype(F32)
    col = jnp.tile(jnp.arange(GRID_W), rows).astype(F32)
    n_freq = HEAD_DIM // 4
    freqs = ROPE_THETA ** (-jnp.arange(n_freq, dtype=F32) / n_freq)
    ang_r = row[:, None] * freqs
    ang_c = col[:, None] * freqs
    ang = jnp.concatenate([ang_r, ang_r, ang_c, ang_c], axis=-1)
    sign = jnp.where((jnp.arange(HEAD_DIM) % ROPE_GROUP) < ROPE_GROUP // 2, -1.0, 1.0).astype(F32)
    return jnp.tile(jnp.cos(ang), (1, 2 * heads)), jnp.tile(jnp.sin(ang) * sign, (1, 2 * heads))


def _moe(hm, aff_t, tri, w_gate, w_up, w_down, layer):
    b, n, _ = hm.shape
    ne = aff_t.shape[1]
    cap = CAPACITY_FACTOR * n // ne
    tt = _tile(n, MOE_TOKEN_TILE)
    nt = n // tt
    win = min(SLOT_WINDOW, cap)
    pos, first = _route(aff_t, tri, cap=cap, tt=tt)
    first = first[:, :, :nt + 1].astype(jnp.int32)
    offs = jnp.minimum(first[:, :, :nt] // PACKED_ROWS * PACKED_ROWS, cap - win)
    fits = jnp.all(first[:, :, 1:] <= offs + win)
    a_flat = offs.reshape(-1)
    xs, gs = lax.cond(fits,
                      lambda: _gather_win(a_flat, pos, aff_t, hm, cap=cap, win=win, tt=tt),
                      lambda: _gather(pos, aff_t, hm, cap=cap))
    yg = _ffn(xs, w_gate, w_up, w_down, gs, layer=layer, cap=cap)
    return lax.cond(fits,
                    lambda: _combine_win(a_flat, pos, yg, cap=cap, win=win, tt=tt),
                    lambda: _combine(pos, yg, cap=cap))


def kernel(x, c, ctx, c_ctx, w_ada, b_ada, norm1_g, norm2_g, w_in, w_out, lambda_q1, lambda_k1, lambda_q2, lambda_k2, subln_g, sgu_norm_g, sgu_w, sgu_b, w_router, w_gate, w_up, w_down, norm_f_g):
    b, n, d = x.shape
    m = ctx.shape[1]
    depth = w_in.shape[0]
    sw = sgu_norm_g.shape[1]
    aw = (w_in.shape[2] - 2 * sw) // 3
    heads = aw // (2 * HEAD_DIM)
    ne = w_router.shape[2]
    bf = lambda t: t.astype(MXU_DTYPE)
    assert aw == heads * 2 * HEAD_DIM and heads % 2 == 0, "q/k blocks pair two heads per 128 lanes"
    assert d % LANES == 0 and sw % LANES == 0 and sw % SGU_HEADS == 0
    assert n % GRID_W == 0, "latent tokens form whole rows of the rope grid"
    for tokens in (n, m):
        assert tokens % CHUNK == 0, "token counts are whole spatial-gating chunks"
        assert (CAPACITY_FACTOR * tokens) % (ne * PACKED_ROWS) == 0, "expert capacity is a whole number of packed rows"

    rows = -(-(b + 1) // 8) * 8
    cc = jnp.concatenate([c, c_ctx[None, :], jnp.zeros((rows - b - 1, d), F32)], axis=0)
    mod = _ada(cc, w_ada, b_ada)

    w_in_all = bf(w_in)
    cos, sin = _rope_tables(n, heads)
    cos_c = jnp.ones((m, aw), F32)
    sin_c = jnp.zeros((m, aw), F32)
    hw_s = sw // SGU_HEADS
    ones_blk = bf(jnp.kron(jnp.eye(SGU_HEADS, dtype=F32), jnp.full((hw_s, hw_s), 1.0 / hw_s, F32)))
    tri = bf(jnp.triu(jnp.ones((LANES, LANES), F32)))

    cx = ctx
    res_x = res_c = None
    for l in range(depth):
        last = l == depth - 1
        lam_init = 0.8 - 0.6 * math.exp(-0.3 * l)
        parts = [mod[l, :, i * d:(i + 1) * d] for i in range(6)]
        sh1, sc1, g1, sh2, sc2, g2 = [t[:b, None, :] for t in parts]
        csh1, csc1, cg1, csh2, csc2, cg2 = [jnp.broadcast_to(t[b][None, None, :], (b, 1, d)) for t in parts]

        w_in_p = w_in_all[l]
        w_out_l = bf(w_out[l])
        wr_t = bf(w_router[l].T)
        ng1, ng2 = norm1_g[l][None, :], norm2_g[l][None, :]
        sng = sgu_norm_g[l][None, :]
        ws = bf(sgu_w[l].reshape(SGU_HEADS * CHUNK, CHUNK))
        bs = jnp.repeat(sgu_b[l].T, hw_s, axis=1)
        lam_params = jnp.zeros((8, LANES), F32).at[:4, :HEAD_DIM].set(
            jnp.stack([lambda_q1[l], lambda_k1[l], lambda_q2[l], lambda_k2[l]]))
        sub_g = subln_g[l][None, :]
        shared = dict(aw=aw, sw=sw)

        x, (q, k, v, u, gvn) = _proj(x, res_x, ng1, sh1, sc1, w_in_p, cos, sin, sng, ones_blk, rope=True, **shared)
        cx, (qc, kc, vc, uc, gvnc) = _proj(cx, res_c, ng1, csh1, csc1, w_in_p, cos_c, sin_c, sng, ones_blk,
                                           rope=False, **shared)
        o = _attn(lam_params, q, [(kc, vc), (k, v)], sub_g, lam_init=lam_init)
        x, hm, aff_t = _outproj(o, u, gvn, ws, bs, w_out_l, x, g1, ng2, sh2, sc2, wr_t)
        res_x = (_moe(hm, aff_t, tri, w_gate, w_up, w_down, l), g2)
        if not last:
            oc = _attn(lam_params, qc, [(kc, vc)], sub_g, lam_init=lam_init)
            cx, hmc, aff_c = _outproj(oc, uc, gvnc, ws, bs, w_out_l, cx, cg1, ng2, csh2, csc2, wr_t)
            res_c = (_moe(hmc, aff_c, tri, w_gate, w_up, w_down, l), cg2)
    return _final(x, res_x[0], res_x[1], norm_f_g[None, :])
```

```python
import functools
import math

import jax
import jax.numpy as jnp
from jax import lax
from jax.experimental import pallas as pl
from jax.experimental.pallas import tpu as pltpu

F32 = jnp.float32
MXU_DTYPE = jnp.bfloat16

EPS = 1e-6
GRID_W = 64
ROPE_THETA = 10000.0
HEAD_DIM = 64
SGU_HEADS = 8
CHUNK = 128
ROPE_GROUP = HEAD_DIM // 2
CAPACITY_FACTOR = 2
LANES = 128
PACKED_ROWS = 16
TOKEN_TILE = 512
ATTN_KEY_CHUNK = 256
ADA_TILE = 1536
FFN_ROWS = 1024
FFN_HIDDEN_TILE = 1024
MOE_TOKEN_TILE = 256
SLOT_WINDOW = 80
VMEM_LIMIT = 56 * 1024 * 1024


def _cparams(*sem):
    return pltpu.CompilerParams(dimension_semantics=sem, vmem_limit_bytes=VMEM_LIMIT)


def _dot(a, b):
    return jnp.dot(a.astype(MXU_DTYPE), b.astype(MXU_DTYPE), preferred_element_type=F32)


def _dot_nt(a, b):
    return lax.dot_general(a.astype(MXU_DTYPE), b.astype(MXU_DTYPE), (((1,), (1,)), ((), ())),
                           preferred_element_type=F32)


def _dot_tn(a, b):
    return lax.dot_general(a.astype(MXU_DTYPE), b.astype(MXU_DTYPE), (((0,), (0,)), ((), ())),
                           preferred_element_type=F32)


def _split_hi_lo(x):
    hi = x.astype(MXU_DTYPE)
    lo = (x - hi.astype(F32)).astype(MXU_DTYPE)
    return hi, lo


def _silu(x):
    return x * jax.nn.sigmoid(x)


def _gelu(x):
    return 0.5 * x * (1.0 + lax.erf(x * (2.0 ** -0.5)))


def _tile(n, pref):
    t = pref
    while t > 8 and n % t:
        t //= 2
    return t if n % t == 0 else n


def _ada_kernel(c_ref, w_ref, b_ref, o_ref):
    o_ref[0] = _dot(_silu(c_ref[...]), w_ref[0]) + b_ref[0]


def _ada(cc, w_ada, b_ada):
    depth, d, n6 = w_ada.shape
    rows = cc.shape[0]
    tn = _tile(n6, ADA_TILE)
    return pl.pallas_call(
        _ada_kernel,
        grid=(depth, n6 // tn),
        in_specs=[pl.BlockSpec((rows, d), lambda l, j: (0, 0)),
                  pl.BlockSpec((1, d, tn), lambda l, j: (l, 0, j)),
                  pl.BlockSpec((1, 1, tn), lambda l, j: (l, 0, j))],
        out_specs=pl.BlockSpec((1, rows, tn), lambda l, j: (l, 0, j)),
        out_shape=jax.ShapeDtypeStruct((depth, rows, n6), F32),
        compiler_params=_cparams("parallel", "parallel"), name="ada",
    )(cc, w_ada, b_ada.reshape(depth, 1, n6))


def _proj_kernel(*refs, has_res, rope, aw, sw):
    if has_res:
        x_ref, moe_ref, g2_ref, *refs = refs
    else:
        x_ref, *refs = refs
    (ng_ref, sh_ref, sc_ref, w_ref, cos_ref, sin_ref, sng_ref, ones_ref, *outs) = refs
    if has_res:
        xo_ref, q_ref, k_ref, v_ref, u_ref, gvn_ref = outs
    else:
        q_ref, k_ref, v_ref, u_ref, gvn_ref = outs

    x = x_ref[0]
    if has_res:
        x = x + g2_ref[0] * moe_ref[0]
        xo_ref[0] = x
    h = x * lax.rsqrt(jnp.mean(x * x, axis=-1, keepdims=True) + EPS) * ng_ref[...]
    h = h * (1.0 + sc_ref[0]) + sh_ref[0]
    p = _dot(h, w_ref[...])

    q = p[:, :aw]
    k = p[:, aw:2 * aw]
    if rope:
        half = ROPE_GROUP // 2
        first = (lax.broadcasted_iota(jnp.int32, q.shape, 1) % ROPE_GROUP) < half
        cos = cos_ref[...]
        sin = sin_ref[...]

        def rot(t):
            return jnp.where(first, pltpu.roll(t, aw - half, 1), pltpu.roll(t, half, 1))

        q = q * cos + rot(q) * sin
        k = k * cos + rot(k) * sin
    q_ref[0] = (q * (HEAD_DIM ** -0.5 * math.log2(math.e))).astype(q_ref.dtype)
    k_ref[0] = k.astype(k_ref.dtype)
    v_ref[0] = p[:, 2 * aw:3 * aw].astype(v_ref.dtype)

    u_ref[0] = _gelu(p[:, 3 * aw:3 * aw + sw]).astype(u_ref.dtype)
    gv = _gelu(p[:, 3 * aw + sw:])
    sq_hi, sq_lo = _split_hi_lo(gv * gv)
    ms = _dot(sq_hi, ones_ref[...]) + _dot(sq_lo, ones_ref[...])
    gvn_ref[0] = (gv * lax.rsqrt(ms + EPS) * sng_ref[...]).astype(gvn_ref.dtype)


def _proj(x, res, ng, sh, sc, w_in_p, cos, sin, sng, ones_blk, *, rope, aw, sw):
    b, n, d = x.shape
    pw = w_in_p.shape[1]
    tm = _tile(n, TOKEN_TILE)
    has_res = res is not None
    tok = lambda bi, i: (bi, i, 0)
    per_b = lambda bi, i: (bi, 0, 0)
    const2 = lambda bi, i: (0, 0)
    in_specs = [pl.BlockSpec((1, tm, d), tok)]
    args = [x]
    if has_res:
        moe, g2 = res
        in_specs += [pl.BlockSpec((1, tm, d), tok), pl.BlockSpec((1, 1, d), per_b)]
        args += [moe, g2]
    in_specs += [pl.BlockSpec((1, d), const2), pl.BlockSpec((1, 1, d), per_b), pl.BlockSpec((1, 1, d), per_b),
                 pl.BlockSpec((d, pw), const2),
                 pl.BlockSpec((tm, aw), lambda bi, i: (i, 0)), pl.BlockSpec((tm, aw), lambda bi, i: (i, 0)),
                 pl.BlockSpec((1, sw), const2), pl.BlockSpec((sw, sw), const2)]
    args += [ng, sh, sc, w_in_p, cos, sin, sng, ones_blk]
    out_specs = [pl.BlockSpec((1, tm, aw), tok)] * 3 + [pl.BlockSpec((1, tm, sw), tok)] * 2
    out_shape = [jax.ShapeDtypeStruct((b, n, aw), MXU_DTYPE)] * 3 + [jax.ShapeDtypeStruct((b, n, sw), MXU_DTYPE)] * 2
    if has_res:
        out_specs = [pl.BlockSpec((1, tm, d), tok)] + out_specs
        out_shape = [jax.ShapeDtypeStruct((b, n, d), F32)] + out_shape
    outs = pl.pallas_call(
        functools.partial(_proj_kernel, has_res=has_res, rope=rope, aw=aw, sw=sw),
        grid=(b, n // tm),
        in_specs=in_specs, out_specs=out_specs, out_shape=out_shape,
        compiler_params=_cparams("parallel", "parallel"), name="proj_lat" if rope else "proj_ctx",
    )(*args)
    if has_res:
        return outs[0], outs[1:]
    return x, outs


def _attn_kernel(lam_ref, q0_ref, q1_ref, *refs, tq, tkc, lam_init):
    *key_refs, g_ref, o_ref, s0_ref, s1_ref, m0_ref, m1_ref = refs
    sources = [key_refs[i:i + 3] for i in range(0, len(key_refs), 3)]
    chunks, col = [], 0
    for k0_ref, k1_ref, v_ref in sources:
        for r0 in range(0, v_ref.shape[1], tkc):
            chunks.append(((k0_ref, k1_ref), v_ref, r0, col))
            col += tkc
    q_ref = (q0_ref, q1_ref)
    s_ref, m_ref = (s0_ref, s1_ref), (m0_ref, m1_ref)
    hw = q0_ref.shape[2]
    nt = q0_ref.shape[1] // tq
    lane = lax.broadcasted_iota(jnp.int32, (tq, hw), 1)
    mine = (lane >= HEAD_DIM) == (pl.program_id(1) % 2 == 1)
    lp = lam_ref[...]
    lam = (jnp.exp(jnp.sum(lp[0:1] * lp[1:2], axis=-1, keepdims=True))
           - jnp.exp(jnp.sum(lp[2:3] * lp[3:4], axis=-1, keepdims=True)) + lam_init)

    def rows(t):
        return pl.ds(t * tq if isinstance(t, int) else pl.multiple_of(t * tq, tq), tq)

    def phase(t, slot, *, do_scores=True, do_values=True):
        other = 1 - slot
        if do_scores:
            qc = []
            for c in range(2):
                q = q_ref[c][0, rows(t + 1), :]
                qc.append(jnp.where(mine, q, jnp.zeros_like(q)))
            run = [None, None]
        if do_values:
            top = [jnp.max(m_ref[slot][c], axis=-1, keepdims=True) for c in range(2)]
            acc = [jnp.zeros((tq, 2 * hw), F32), jnp.zeros((tq, 2 * hw), F32)]
        for c in range(2):
            for k_ref, v_ref, r0, col in chunks:
                cols = slice(col, col + tkc)
                if do_values:
                    vals = v_ref[0, r0:r0 + tkc, :]
                    vals1 = jnp.concatenate([vals, jnp.ones_like(vals)], axis=1)
                if do_scores:
                    s = _dot_nt(qc[c], k_ref[c][0, r0:r0 + tkc, :])
                    s_ref[other][c, :, cols] = s
                    for j in range(tkc // LANES):
                        part = s[:, j * LANES:(j + 1) * LANES]
                        run[c] = part if run[c] is None else jnp.maximum(run[c], part)
                if do_values:
                    p = jnp.exp2(s_ref[slot][c, :, cols] - top[c])
                    acc[c] = acc[c] + _dot(p, vals1)
        if do_scores:
            for c in range(2):
                m_ref[other][c] = run[c]
        if do_values:
            o = acc[0][:, :hw] / acc[0][:, hw:] - lam * (acc[1][:, :hw] / acc[1][:, hw:])
            o = o * lax.rsqrt(jnp.mean(o * o, axis=-1, keepdims=True) + EPS) * g_ref[...] * (1.0 - lam_init)
            o_ref[0, rows(t), :] = o.astype(o_ref.dtype)

    phase(-1, 1, do_values=False)

    def step(t, carry):
        for slot in range(2):
            @pl.when(t % 2 == slot)
            def _():
                phase(t, slot)
        return carry

    lax.fori_loop(0, nt - 1, step, 0)
    phase(nt - 1, (nt - 1) % 2, do_scores=False)


def _attn(lam_params, q, kv_sources, subln_g, *, lam_init):
    b, n, aw = q.shape
    lens = [k.shape[1] for k, _ in kv_sources]
    nkeys = sum(lens)
    hw = 2 * HEAD_DIM
    heads = aw // hw
    pairs = heads // 2
    tq = _tile(n, TOKEN_TILE)
    tkc = min(_tile(m, ATTN_KEY_CHUNK) for m in lens)
    comp0 = lambda bi, h: (bi, 0, h // 2)
    comp1 = lambda bi, h: (bi, 0, pairs + h // 2)
    key_specs, key_args = [], []
    for (k, v), m in zip(kv_sources, lens):
        key_specs += [pl.BlockSpec((1, m, hw), comp0), pl.BlockSpec((1, m, hw), comp1),
                      pl.BlockSpec((1, m, hw), lambda bi, h: (bi, 0, h))]
        key_args += [k, k, v]
    return pl.pallas_call(
        functools.partial(_attn_kernel, tq=tq, tkc=tkc, lam_init=lam_init),
        grid=(b, heads),
        in_specs=[pl.BlockSpec(lam_params.shape, lambda bi, h: (0, 0)),
                  pl.BlockSpec((1, n, hw), comp0), pl.BlockSpec((1, n, hw), comp1), *key_specs,
                  pl.BlockSpec((1, hw), lambda bi, h: (0, 0))],
        out_specs=pl.BlockSpec((1, n, hw), lambda bi, h: (bi, 0, h)),
        out_shape=jax.ShapeDtypeStruct((b, n, aw), MXU_DTYPE),
        scratch_shapes=[pltpu.VMEM((2, tq, nkeys), F32)] * 2 + [pltpu.VMEM((2, tq, LANES), F32)] * 2,
        compiler_params=_cparams("parallel", "parallel"), name=f"attn_{n}",
    )(lam_params, q, q, *key_args, subln_g)


def _out_kernel(o_ref, u_ref, gvn_ref, ws_ref, bs_ref, w_ref, x_ref, g1_ref, ng_ref, sh_ref, sc_ref, wr_ref,
                xo_ref, hm_ref, aff_ref, *, aw):
    tm, sw = u_ref.shape[1], u_ref.shape[2]
    head_of_lane = lax.broadcasted_iota(jnp.int32, (CHUNK, sw), 1) // (sw // SGU_HEADS)
    gated = []
    for ci in range(tm // CHUNK):
        rows = slice(ci * CHUNK, (ci + 1) * CHUNK)
        r = _dot(ws_ref[...], gvn_ref[0, rows, :])
        s = bs_ref[...]
        for hh in range(SGU_HEADS):
            s = s + jnp.where(head_of_lane == hh, r[hh * CHUNK:(hh + 1) * CHUNK], 0.0)
        gated.append((u_ref[0, rows, :].astype(F32) * s).astype(MXU_DTYPE))
    y = _dot(o_ref[0], w_ref[:aw, :]) + _dot(jnp.concatenate(gated, axis=0), w_ref[aw:, :])
    x = x_ref[0] + g1_ref[0] * y
    xo_ref[0] = x
    h = x * lax.rsqrt(jnp.mean(x * x, axis=-1, keepdims=True) + EPS) * ng_ref[...]
    h = h * (1.0 + sc_ref[0]) + sh_ref[0]
    hm_ref[0] = h.astype(hm_ref.dtype)
    logits = _dot_nt(wr_ref[...], h)
    e = jnp.exp(logits - jnp.max(logits, axis=0, keepdims=True))
    aff_ref[0] = e / jnp.sum(e, axis=0, keepdims=True)


def _outproj(o, u, gvn, ws, bs, w_out, x, g1, ng, sh, sc, wr_t):
    b, n, d = x.shape
    aw = o.shape[2]
    sw = u.shape[2]
    ne = wr_t.shape[0]
    tm = _tile(n, TOKEN_TILE)
    tok = lambda bi, i: (bi, i, 0)
    per_b = lambda bi, i: (bi, 0, 0)
    const2 = lambda bi, i: (0, 0)
    return pl.pallas_call(
        functools.partial(_out_kernel, aw=aw),
        grid=(b, n // tm),
        in_specs=[pl.BlockSpec((1, tm, aw), tok), pl.BlockSpec((1, tm, sw), tok), pl.BlockSpec((1, tm, sw), tok),
                  pl.BlockSpec((SGU_HEADS * CHUNK, CHUNK), const2), pl.BlockSpec((CHUNK, sw), const2),
                  pl.BlockSpec((aw + sw, d), const2), pl.BlockSpec((1, tm, d), tok),
                  pl.BlockSpec((1, 1, d), per_b), pl.BlockSpec((1, d), const2),
                  pl.BlockSpec((1, 1, d), per_b), pl.BlockSpec((1, 1, d), per_b),
                  pl.BlockSpec((ne, d), const2)],
        out_specs=[pl.BlockSpec((1, tm, d), tok), pl.BlockSpec((1, tm, d), tok),
                   pl.BlockSpec((1, ne, tm), lambda bi, i: (bi, 0, i))],
        out_shape=[jax.ShapeDtypeStruct((b, n, d), F32), jax.ShapeDtypeStruct((b, n, d), MXU_DTYPE),
                   jax.ShapeDtypeStruct((b, ne, n), F32)],
        compiler_params=_cparams("parallel", "parallel"), name=f"outproj_{n}",
    )(o, u, gvn, ws, bs, w_out, x, g1, ng, sh, sc, wr_t)


def _route_kernel(aff_ref, tri_ref, pos_ref, start_ref, *, cap, tt):
    a = aff_ref[0]
    ne, n = a.shape
    bits = lax.bitcast_convert_type(a, jnp.int32)
    t = jnp.zeros((ne, 1), jnp.int32)
    for bit in range(30, -1, -1):
        cand = t | (1 << bit)
        cnt = jnp.sum((bits >= cand).astype(jnp.int32), axis=1, keepdims=True)
        t = jnp.where(cnt >= cap, cand, t)
    gt = bits > t
    eq = bits == t
    need = cap - jnp.sum(gt.astype(jnp.int32), axis=1, keepdims=True)

    def exclusive_count(mask):
        out = []
        runs = []
        run = jnp.zeros((ne, 1), F32)
        for ci in range(n // LANES):
            runs.append(run)
            m = mask[:, ci * LANES:(ci + 1) * LANES].astype(F32)
            incl = _dot(m, tri_ref[...])
            out.append(run + incl - m)
            run = run + incl[:, LANES - 1:LANES]
        return jnp.concatenate(out, axis=1), runs

    sel = gt | (eq & (exclusive_count(eq)[0] < need.astype(F32)))
    pos, runs = exclusive_count(sel)
    pos_ref[0] = jnp.where(sel, pos, -1.0)
    lane = lax.broadcasted_iota(jnp.int32, (ne, LANES), 1)
    first = jnp.full((ne, LANES), float(cap), F32)
    for j in range(n // tt):
        first = jnp.where(lane == j, runs[j * tt // LANES], first)
    start_ref[0] = first


def _route(aff_t, tri, *, cap, tt):
    b, ne, n = aff_t.shape
    rows = b * ne
    pos, first = pl.pallas_call(
        functools.partial(_route_kernel, cap=cap, tt=tt),
        grid=(1,),
        in_specs=[pl.BlockSpec((1, rows, n), lambda i: (0, 0, 0)), pl.BlockSpec((LANES, LANES), lambda i: (0, 0))],
        out_specs=[pl.BlockSpec((1, rows, n), lambda i: (0, 0, 0)), pl.BlockSpec((1, rows, LANES), lambda i: (0, 0, 0))],
        out_shape=[jax.ShapeDtypeStruct((1, rows, n), F32), jax.ShapeDtypeStruct((1, rows, LANES), F32)],
        compiler_params=_cparams("arbitrary"), name=f"route_{n}",
    )(aff_t.reshape(1, rows, n), tri)
    return pos.reshape(b, ne, n), first.reshape(b, ne, LANES)


def _window_hits(a_ref, pos_ref, *, win):
    bi, t, nt = pl.program_id(0), pl.program_id(1), pl.num_programs(1)
    ne, tt = pos_ref.shape[1], pos_ref.shape[2]
    base = lax.broadcasted_iota(jnp.int32, (win, tt), 0)
    offs, hits = [], []
    for e in range(ne):
        a_e = a_ref[(bi * ne + e) * nt + t]
        offs.append(a_e)
        hits.append(pos_ref[0, e:e + 1, :] == (base + a_e).astype(F32))
    return offs, hits


def _gather_win_kernel(a_ref, pos_ref, aff_ref, hm_ref, xs_ref, gs_ref, *, cap, win):
    @pl.when(pl.program_id(1) == 0)
    def _():
        xs_ref[...] = jnp.zeros_like(xs_ref)
        gs_ref[...] = jnp.zeros_like(gs_ref)

    ne = pos_ref.shape[1]
    offs, hits = _window_hits(a_ref, pos_ref, win=win)
    group = 8
    for e0 in range(0, ne, group):
        onehot = jnp.concatenate([h.astype(MXU_DTYPE) for h in hits[e0:e0 + group]], axis=0)
        picked = _dot(onehot, hm_ref[0]).astype(xs_ref.dtype)
        for e in range(e0, min(e0 + group, ne)):
            rows = pl.ds(pl.multiple_of(e * cap + offs[e], PACKED_ROWS), win)
            gate = jnp.sum(jnp.where(hits[e], aff_ref[0, e:e + 1, :], 0.0), axis=1, keepdims=True)
            xs_ref[0, rows, :] += picked[(e - e0) * win:(e - e0 + 1) * win]
            gs_ref[0, rows, :] += jnp.broadcast_to(gate, (win, LANES))


def _gather_win(a_flat, pos, aff_t, hm, *, cap, win, tt):
    b, ne, n = pos.shape
    d = hm.shape[2]
    tile = lambda bi, t, a: (bi, 0, t)
    whole = lambda bi, t, a: (bi, 0, 0)
    return pl.pallas_call(
        functools.partial(_gather_win_kernel, cap=cap, win=win),
        grid_spec=pltpu.PrefetchScalarGridSpec(
            num_scalar_prefetch=1, grid=(b, n // tt),
            in_specs=[pl.BlockSpec((1, ne, tt), tile), pl.BlockSpec((1, ne, tt), tile),
                      pl.BlockSpec((1, tt, d), lambda bi, t, a: (bi, t, 0))],
            out_specs=[pl.BlockSpec((1, ne * cap, d), whole), pl.BlockSpec((1, ne * cap, LANES), whole)]),
        out_shape=[jax.ShapeDtypeStruct((b, ne * cap, d), MXU_DTYPE),
                   jax.ShapeDtypeStruct((b, ne * cap, LANES), F32)],
        compiler_params=_cparams("parallel", "arbitrary"), name=f"gather_win_{n}",
    )(a_flat, pos, aff_t, hm)


def _gather_kernel(pos_ref, aff_ref, hm_ref, xs_ref, gs_ref, *, cap, tn):
    n = hm_ref.shape[1]
    d = hm_ref.shape[2]
    slot = lax.broadcasted_iota(jnp.int32, (cap, tn), 0).astype(F32)
    xs = jnp.zeros((cap, d), F32)
    gs = jnp.zeros((cap, 1), F32)
    for ci in range(n // tn):
        cols = slice(ci * tn, (ci + 1) * tn)
        hit = pos_ref[0, 0, :, cols] == slot
        xs = xs + _dot(hit.astype(MXU_DTYPE), hm_ref[0, cols, :])
        gs = gs + jnp.sum(jnp.where(hit, aff_ref[0, 0, :, cols], 0.0), axis=1, keepdims=True)
    xs_ref[0] = xs.astype(xs_ref.dtype)
    gs_ref[0] = jnp.broadcast_to(gs, (cap, LANES))


def _gather(pos, aff_t, hm, *, cap):
    b, ne, n = pos.shape
    d = hm.shape[2]
    tn = _tile(n, TOKEN_TILE)
    pos4 = pos.reshape(b, ne, 1, n)
    aff4 = aff_t.reshape(b, ne, 1, n)
    return pl.pallas_call(
        functools.partial(_gather_kernel, cap=cap, tn=tn),
        grid=(b, ne),
        in_specs=[pl.BlockSpec((1, 1, 1, n), lambda bi, e: (bi, e, 0, 0)),
                  pl.BlockSpec((1, 1, 1, n), lambda bi, e: (bi, e, 0, 0)),
                  pl.BlockSpec((1, n, d), lambda bi, e: (bi, 0, 0))],
        out_specs=[pl.BlockSpec((1, cap, d), lambda bi, e: (bi, e, 0)),
                   pl.BlockSpec((1, cap, LANES), lambda bi, e: (bi, e, 0))],
        out_shape=[jax.ShapeDtypeStruct((b, ne * cap, d), MXU_DTYPE),
                   jax.ShapeDtypeStruct((b, ne * cap, LANES), F32)],
        compiler_params=_cparams("parallel", "parallel"), name=f"gather_{n}",
    )(pos4, aff4, hm)


def _ffn_kernel(x_ref, wg_ref, wu_ref, wd_ref, gs_ref, y_ref, *, nf):
    f = pl.program_id(2)
    _, tm, d = y_ref.shape

    def partial_out():
        x = x_ref[...].reshape(tm, d)
        hid = _silu(_dot(x, wg_ref[0, 0])) * _dot(x, wu_ref[0, 0])
        return _dot(hid, wd_ref[0, 0])

    def finish(total):
        y_ref[0] = total * gs_ref[...].reshape(tm, LANES)[:, :1]

    if nf == 1:
        finish(partial_out())
        return

    @pl.when(f == 0)
    def _():
        y_ref[0] = partial_out()

    if nf > 2:
        @pl.when((f > 0) & (f < nf - 1))
        def _():
            y_ref[0] += partial_out()

    @pl.when(f == nf - 1)
    def _():
        finish(y_ref[0] + partial_out())


def _ffn(xs, w_gate, w_up, w_down, gs, *, layer, cap):
    b, rows, d = xs.shape
    _, ne, _, ff = w_gate.shape
    bt = b
    while bt * cap > FFN_ROWS and bt % 2 == 0:
        bt //= 2
    tm = bt * cap
    tf = _tile(ff, FFN_HIDDEN_TILE)
    return pl.pallas_call(
        functools.partial(_ffn_kernel, nf=ff // tf),
        grid=(ne, b // bt, ff // tf),
        in_specs=[pl.BlockSpec((bt, 1, cap, d), lambda e, i, f: (i, e, 0, 0)),
                  pl.BlockSpec((1, 1, d, tf), lambda e, i, f: (layer, e, 0, f)),
                  pl.BlockSpec((1, 1, d, tf), lambda e, i, f: (layer, e, 0, f)),
                  pl.BlockSpec((1, 1, tf, d), lambda e, i, f: (layer, e, f, 0)),
                  pl.BlockSpec((bt, 1, cap, LANES), lambda e, i, f: (i, e, 0, 0))],
        out_specs=pl.BlockSpec((1, tm, d), lambda e, i, f: (e, i, 0)),
        out_shape=jax.ShapeDtypeStruct((ne, b * cap, d), F32),
        compiler_params=_cparams("parallel", "parallel", "arbitrary"), name=f"ffn_{b * cap}",
    )(xs.reshape(b, ne, cap, d), w_gate, w_up, w_down, gs.reshape(b, ne, cap, LANES))


def _combine_win_kernel(a_ref, pos_ref, x_ref, g2_ref, *refs, win):
    y_refs, o_ref = refs[:-1], refs[-1]
    d = o_ref.shape[2]
    _, hits = _window_hits(a_ref, pos_ref, win=win)
    onehot = jnp.concatenate([h.astype(MXU_DTYPE) for h in hits], axis=0)
    parts = [_split_hi_lo(y_ref[0]) for y_ref in y_refs]
    y_all = jnp.concatenate([jnp.concatenate([p[0] for p in parts], axis=0),
                             jnp.concatenate([p[1] for p in parts], axis=0)], axis=1)
    r = _dot_tn(onehot, y_all)
    o_ref[0] = x_ref[0] + g2_ref[0] * (r[:, :d] + r[:, d:])


def _combine_win(a_flat, pos, yg, x, g2, *, cap, win, tt):
    b, ne, n = pos.shape
    d = yg.shape[2]
    nt = n // tt

    def window(e):
        return pl.BlockSpec((pl.Element(1), pl.Element(win), pl.Element(d)),
                            lambda bi, t, a: (e, pl.multiple_of(bi * cap + a[(bi * ne + e) * nt + t], PACKED_ROWS), 0))

    return pl.pallas_call(
        functools.partial(_combine_win_kernel, win=win),
        grid_spec=pltpu.PrefetchScalarGridSpec(
            num_scalar_prefetch=1, grid=(b, nt),
            in_specs=[pl.BlockSpec((1, ne, tt), lambda bi, t, a: (bi, 0, t)),
                      pl.BlockSpec((1, tt, d), lambda bi, t, a: (bi, t, 0)),
                      pl.BlockSpec((1, 1, d), lambda bi, t, a: (bi, 0, 0))] + [window(e) for e in range(ne)],
            out_specs=pl.BlockSpec((1, tt, d), lambda bi, t, a: (bi, t, 0))),
        out_shape=jax.ShapeDtypeStruct((b, n, d), F32),
        compiler_params=_cparams("parallel", "parallel"), name=f"combine_win_{n}",
    )(a_flat, pos, x, g2, *([yg] * ne))


def _combine_kernel(pos_ref, y_ref, o_ref, *, cap, tn):
    e = pl.program_id(1)

    @pl.when(e == 0)
    def _():
        o_ref[...] = jnp.zeros_like(o_ref)

    n = o_ref.shape[1]
    y_hi, y_lo = _split_hi_lo(y_ref[0])
    slot = lax.broadcasted_iota(jnp.int32, (cap, tn), 0).astype(F32)
    for ci in range(n // tn):
        cols = slice(ci * tn, (ci + 1) * tn)
        hit = (pos_ref[0, 0, :, cols] == slot).astype(MXU_DTYPE)
        o_ref[0, cols, :] += _dot_tn(hit, y_hi) + _dot_tn(hit, y_lo)


def _combine(pos, yg, *, cap):
    b, ne, n = pos.shape
    d = yg.shape[2]
    tn = _tile(n, TOKEN_TILE)
    return pl.pallas_call(
        functools.partial(_combine_kernel, cap=cap, tn=tn),
        grid=(b, ne),
        in_specs=[pl.BlockSpec((1, 1, 1, n), lambda bi, e: (bi, e, 0, 0)),
                  pl.BlockSpec((1, cap, d), lambda bi, e: (e, bi, 0))],
        out_specs=pl.BlockSpec((1, n, d), lambda bi, e: (bi, 0, 0)),
        out_shape=jax.ShapeDtypeStruct((b, n, d), F32),
        compiler_params=_cparams("parallel", "arbitrary"), name=f"combine_{n}",
    )(pos.reshape(b, ne, 1, n), yg)


def _final_kernel(x_ref, ng_ref, o_ref):
    x = x_ref[0]
    o_ref[0] = x * lax.rsqrt(jnp.mean(x * x, axis=-1, keepdims=True) + EPS) * ng_ref[...]


def _final(x, ng):
    b, n, d = x.shape
    tm = _tile(n, TOKEN_TILE)
    tok = lambda bi, i: (bi, i, 0)
    return pl.pallas_call(
        _final_kernel,
        grid=(b, n // tm),
        in_specs=[pl.BlockSpec((1, tm, d), tok), pl.BlockSpec((1, d), lambda bi, i: (0, 0))],
        out_specs=pl.BlockSpec((1, tm, d), tok),
        out_shape=jax.ShapeDtypeStruct((b, n, d), F32),
        compiler_params=_cparams("parallel", "parallel"), name="final",
    )(x, ng)


def _residual_kernel(x_ref, moe_ref, g2_ref, o_ref):
    o_ref[0] = x_ref[0] + g2_ref[0] * moe_ref[0]


def _residual(x, moe, g2):
    b, n, d = x.shape
    tm = _tile(n, TOKEN_TILE)
    tok = lambda bi, i: (bi, i, 0)
    return pl.pallas_call(
        _residual_kernel,
        grid=(b, n // tm),
        in_specs=[pl.BlockSpec((1, tm, d), tok), pl.BlockSpec((1, tm, d), tok),
                  pl.BlockSpec((1, 1, d), lambda bi, i: (bi, 0, 0))],
        out_specs=pl.BlockSpec((1, tm, d), tok),
        out_shape=jax.ShapeDtypeStruct((b, n, d), F32),
        compiler_params=_cparams("parallel", "parallel"), name=f"residual_{n}",
    )(x, moe, g2)


def _rope_tables(n, heads):
    rows = n // GRID_W
    row = jnp.repeat(jnp.arange(rows), GRID_W).astype(F32)
    col = jnp.tile(jnp.arange(GRID_W), rows).astype(F32)
    n_freq = HEAD_DIM // 4
    freqs = ROPE_THETA ** (-jnp.arange(n_freq, dtype=F32) / n_freq)
    ang_r = row[:, None] * freqs
    ang_c = col[:, None] * freqs
    ang = jnp.concatenate([ang_r, ang_r, ang_c, ang_c], axis=-1)
    sign = jnp.where((jnp.arange(HEAD_DIM) % ROPE_GROUP) < ROPE_GROUP // 2, -1.0, 1.0).astype(F32)
    return jnp.tile(jnp.cos(ang), (1, 2 * heads)), jnp.tile(jnp.sin(ang) * sign, (1, 2 * heads))


def _moe(hm, aff_t, tri, w_gate, w_up, w_down, layer, x, g2):
    b, n, _ = hm.shape
    ne = aff_t.shape[1]
    cap = CAPACITY_FACTOR * n // ne
    tt = _tile(n, MOE_TOKEN_TILE)
    nt = n // tt
    win = min(SLOT_WINDOW, cap)
    pos, first = _route(aff_t, tri, cap=cap, tt=tt)
    first = first[:, :, :nt + 1].astype(jnp.int32)
    offs = jnp.minimum(first[:, :, :nt] // PACKED_ROWS * PACKED_ROWS, cap - win)
    fits = jnp.all(first[:, :, 1:] <= offs + win)
    a_flat = offs.reshape(-1)
    xs, gs = lax.cond(fits,
                      lambda: _gather_win(a_flat, pos, aff_t, hm, cap=cap, win=win, tt=tt),
                      lambda: _gather(pos, aff_t, hm, cap=cap))
    yg = _ffn(xs, w_gate, w_up, w_down, gs, layer=layer, cap=cap)
    return lax.cond(fits,
                    lambda: _combine_win(a_flat, pos, yg, x, g2, cap=cap, win=win, tt=tt),
                    lambda: _residual(x, _combine(pos, yg, cap=cap), g2))


def kernel(x, c, ctx, c_ctx, w_ada, b_ada, norm1_g, norm2_g, w_in, w_out, lambda_q1, lambda_k1, lambda_q2, lambda_k2, subln_g, sgu_norm_g, sgu_w, sgu_b, w_router, w_gate, w_up, w_down, norm_f_g):
    b, n, d = x.shape
    m = ctx.shape[1]
    depth = w_in.shape[0]
    sw = sgu_norm_g.shape[1]
    aw = (w_in.shape[2] - 2 * sw) // 3
    heads = aw // (2 * HEAD_DIM)
    ne = w_router.shape[2]
    bf = lambda t: t.astype(MXU_DTYPE)
    assert aw == heads * 2 * HEAD_DIM and heads % 2 == 0, "q/k blocks pair two heads per 128 lanes"
    assert d % LANES == 0 and sw % LANES == 0 and sw % SGU_HEADS == 0
    assert n % GRID_W == 0, "latent tokens form whole rows of the rope grid"
    for tokens in (n, m):
        assert tokens % CHUNK == 0, "token counts are whole spatial-gating chunks"
        assert (CAPACITY_FACTOR * tokens) % (ne * PACKED_ROWS) == 0, "expert capacity is a whole number of packed rows"

    rows = -(-(b + 1) // 8) * 8
    cc = jnp.concatenate([c, c_ctx[None, :], jnp.zeros((rows - b - 1, d), F32)], axis=0)
    mod = _ada(cc, w_ada, b_ada)

    w_in_all = bf(w_in)
    cos, sin = _rope_tables(n, heads)
    cos_c = jnp.ones((m, aw), F32)
    sin_c = jnp.zeros((m, aw), F32)
    hw_s = sw // SGU_HEADS
    ones_blk = bf(jnp.kron(jnp.eye(SGU_HEADS, dtype=F32), jnp.full((hw_s, hw_s), 1.0 / hw_s, F32)))
    tri = bf(jnp.triu(jnp.ones((LANES, LANES), F32)))

    cx = ctx
    for l in range(depth):
        last = l == depth - 1
        lam_init = 0.8 - 0.6 * math.exp(-0.3 * l)
        parts = [mod[l, :, i * d:(i + 1) * d] for i in range(6)]
        sh1, sc1, g1, sh2, sc2, g2 = [t[:b, None, :] for t in parts]
        csh1, csc1, cg1, csh2, csc2, cg2 = [jnp.broadcast_to(t[b][None, None, :], (b, 1, d)) for t in parts]

        w_in_p = w_in_all[l]
        w_out_l = bf(w_out[l])
        wr_t = bf(w_router[l].T)
        ng1, ng2 = norm1_g[l][None, :], norm2_g[l][None, :]
        sng = sgu_norm_g[l][None, :]
        ws = bf(sgu_w[l].reshape(SGU_HEADS * CHUNK, CHUNK))
        bs = jnp.repeat(sgu_b[l].T, hw_s, axis=1)
        lam_params = jnp.zeros((8, LANES), F32).at[:4, :HEAD_DIM].set(
            jnp.stack([lambda_q1[l], lambda_k1[l], lambda_q2[l], lambda_k2[l]]))
        sub_g = subln_g[l][None, :]
        shared = dict(aw=aw, sw=sw)

        x, (q, k, v, u, gvn) = _proj(x, None, ng1, sh1, sc1, w_in_p, cos, sin, sng, ones_blk, rope=True, **shared)
        cx, (qc, kc, vc, uc, gvnc) = _proj(cx, None, ng1, csh1, csc1, w_in_p, cos_c, sin_c, sng, ones_blk,
                                           rope=False, **shared)
        o = _attn(lam_params, q, [(kc, vc), (k, v)], sub_g, lam_init=lam_init)
        x, hm, aff_t = _outproj(o, u, gvn, ws, bs, w_out_l, x, g1, ng2, sh2, sc2, wr_t)
        x = _moe(hm, aff_t, tri, w_gate, w_up, w_down, l, x, g2)
        if not last:
            oc = _attn(lam_params, qc, [(kc, vc)], sub_g, lam_init=lam_init)
            cx, hmc, aff_c = _outproj(oc, uc, gvnc, ws, bs, w_out_l, cx, cg1, ng2, csh2, csc2, wr_t)
            cx = _moe(hmc, aff_c, tri, w_gate, w_up, w_down, l, cx, cg2)
    return _final(x, norm_f_g[None, :])
```
